```python
import math
import jax, jax.numpy as jnp
from jax import lax
import numpy as np

D_MODEL = 4096
BATCH = 4
SEQ = 2048
DEPTH = 2
DEC_BATCH = 128
DEC_SEQ = 8
PAST_LEN = 16384
PAGE_SIZE = 128

D_RWKV = D_MODEL // 2
RWKV_HEAD = 64
N_RWKV_HEADS = D_RWKV // RWKV_HEAD
W_LORA = max(32, int(round(1.8 * D_RWKV ** 0.5 / 32)) * 32)
A_LORA = max(32, int(round(1.8 * D_RWKV ** 0.5 / 32)) * 32)
G_LORA = max(32, int(round(0.6 * D_RWKV ** 0.8 / 32)) * 32)
LNX_EPS = 64e-5
D_MLSTM = D_MODEL // 2
N_MLSTM_HEADS = 8
MLSTM_DK = D_MLSTM // N_MLSTM_HEADS
MLSTM_DV = D_MLSTM // N_MLSTM_HEADS
MLSTM_CHUNK = 64
MH_EPS = 1e-6
RWKV_COLS = 3 * D_RWKV + W_LORA + A_LORA + G_LORA
MLSTM_COLS = 4 * D_MLSTM + 2 * N_MLSTM_HEADS
GATE_COLS = 2 * D_MODEL
IN_COLS = RWKV_COLS + MLSTM_COLS + GATE_COLS
N_GROUPS = 4
EXPERTS_PER_GROUP = 8
N_EXPERTS = N_GROUPS * EXPERTS_PER_GROUP
TOP_K = 2
D_EXPERT = D_MODEL // 4
MOE_BLOCK = 64
N_ADA = 6
RMS_EPS = 1e-6

kernel_name = 'rwkv7_mlstm_gated_moe_adaln_step'


def rms_norm(x, w):
    xf = x.astype(jnp.float32)
    y = xf * lax.rsqrt(jnp.mean(xf * xf, axis=-1, keepdims=True) + RMS_EPS)
    return (y * w.astype(jnp.float32)).astype(x.dtype)


def rwkv7_scan(r, decay, k, v, a_vec, b_vec, S0):
    def step(S, inp):
        r_t, w_t, k_t, v_t, a_t, b_t = inp
        Sa = jnp.einsum('bhvk,bhk->bhv', S, a_t)
        S = S * w_t[:, :, None, :] + Sa[..., None] * b_t[:, :, None, :] + v_t[..., None] * k_t[:, :, None, :]
        return S, jnp.einsum('bhvk,bhk->bhv', S, r_t)
    xs = tuple(jnp.moveaxis(t, 1, 0) for t in (r, decay, k, v, a_vec, b_vec))
    S, y = lax.scan(step, S0, xs)
    return jnp.moveaxis(y, 0, 1), S


def rwkv7_branch(z, z_prev, mu, w0, w2, a0, a2, g2, k_k, k_a, r_k, lnx_w, lnx_b, S0):
    B, T, _ = z.shape
    f32 = jnp.float32
    zs = jnp.concatenate([z_prev[:, None, :], z[:, :-1, :]], axis=1)
    zm = z + (zs - z) * mu
    o1 = D_RWKV
    o2 = o1 + W_LORA
    o3 = o2 + D_RWKV
    o4 = o3 + D_RWKV
    o5 = o4 + A_LORA
    r, wl, k, v, al, gl = zm[..., :o1], zm[..., o1:o2], zm[..., o2:o3], zm[..., o3:o4], zm[..., o4:o5], zm[..., o5:]
    w_raw = -jax.nn.softplus(-(w0 + jnp.tanh(wl) @ w2).astype(f32)) - 0.5
    decay = jnp.exp(-jnp.exp(w_raw))
    a = jax.nn.sigmoid((a0 + al @ a2).astype(f32))
    g = (jax.nn.sigmoid(gl) @ g2).astype(f32)
    kf = k.astype(f32)
    heads = lambda t: t.reshape(B, T, N_RWKV_HEADS, RWKV_HEAD)
    kk = heads(kf * k_k)
    kk = kk / jnp.maximum(jnp.sqrt(jnp.sum(kk * kk, axis=-1, keepdims=True)), 1e-12)
    kf = kf * (1.0 + (a - 1.0) * k_a)
    rh, kh, vh, ah = heads(r.astype(f32)), heads(kf), heads(v.astype(f32)), heads(a)
    y, S = rwkv7_scan(rh, heads(decay), kh, vh, -kk, kk * ah, S0.astype(f32))
    mean = jnp.mean(y, axis=-1, keepdims=True)
    var = jnp.mean(jnp.square(y - mean), axis=-1, keepdims=True)
    yn = ((y - mean) * lax.rsqrt(var + LNX_EPS)).reshape(B, T, D_RWKV) * lnx_w + lnx_b
    bonus = (jnp.sum(rh * kh * r_k, axis=-1, keepdims=True) * vh).reshape(B, T, D_RWKV)
    return ((yn + bonus) * g).astype(z.dtype), S


def mlstm_chunkwise(q, k, v, ig, lf, C0, n0, m0):
    B, T, H, DK = q.shape
    DV = v.shape[-1]
    L = math.gcd(T, MLSTM_CHUNK)
    nc = T // L
    def to_chunks(t):
        t = t.reshape((B, nc, L, H) + t.shape[3:])
        return jnp.moveaxis(jnp.moveaxis(t, 1, 0), 3, 2)
    causal = jnp.tril(jnp.ones((L, L), dtype=bool))
    def step(carry, inp):
        C, n, m = carry
        qc, kc, vc, ic, fc = inp
        F = jnp.cumsum(fc, axis=-1)
        Dm = jnp.where(causal, F[..., :, None] - F[..., None, :] + ic[..., None, :], -jnp.inf)
        inter = F + m[..., None]
        m_t = jnp.maximum(inter, jnp.max(Dm, axis=-1))
        w_intra = jnp.exp(Dm - m_t[..., None])
        w_inter = jnp.exp(inter - m_t)
        s = jnp.einsum('bhtk,bhsk->bhts', qc, kc) * w_intra
        num = jnp.einsum('bhts,bhsv->bhtv', s, vc) + w_inter[..., None] * jnp.einsum('bhtk,bhkv->bhtv', qc, C)
        den = jnp.sum(s, axis=-1) + w_inter * jnp.einsum('bhtk,bhk->bht', qc, n)
        h = num / jnp.maximum(jnp.abs(den), jnp.exp(-m_t))[..., None]
        m_new = m_t[..., -1]
        w_s = jnp.exp(F[..., -1:] - F + ic - m_new[..., None])
        w_c = jnp.exp(inter[..., -1] - m_new)
        C_new = w_c[..., None, None] * C + jnp.einsum('bhs,bhsk,bhsv->bhkv', w_s, kc, vc)
        n_new = w_c[..., None] * n + jnp.einsum('bhs,bhsk->bhk', w_s, kc)
        return (C_new, n_new, m_new), h
    (C, n, m), h = lax.scan(step, (C0, n0, m0), tuple(to_chunks(t) for t in (q, k, v, ig, lf)))
    h = jnp.moveaxis(jnp.moveaxis(h, 2, 3), 0, 1).reshape(B, T, H, DV)
    return h, C, n, m


def mlstm_branch(z, i_b, f_b, norm_w, C0, n0, m0):
    B, T, _ = z.shape
    f32 = jnp.float32
    H = N_MLSTM_HEADS
    q = z[..., :D_MLSTM].astype(f32).reshape(B, T, H, MLSTM_DK)
    k = z[..., D_MLSTM:2 * D_MLSTM].astype(f32).reshape(B, T, H, MLSTM_DK) * (MLSTM_DK ** -0.5)
    v = z[..., 2 * D_MLSTM:3 * D_MLSTM].astype(f32).reshape(B, T, H, MLSTM_DV)
    o3 = 3 * D_MLSTM
    ig = z[..., o3:o3 + H].astype(f32) + i_b
    lf = jax.nn.log_sigmoid(z[..., o3 + H:o3 + 2 * H].astype(f32) + f_b)
    o_gate = jax.nn.sigmoid(z[..., o3 + 2 * H:].astype(f32))
    h, C, n, m = mlstm_chunkwise(q, k, v, ig, lf, C0.astype(f32), n0.astype(f32), m0.astype(f32))
    mean = jnp.mean(h, axis=-1, keepdims=True)
    var = jnp.mean(jnp.square(h - mean), axis=-1, keepdims=True)
    hn = ((h - mean) * lax.rsqrt(var + MH_EPS)).reshape(B, T, D_MLSTM) * norm_w
    return (hn * o_gate).astype(z.dtype), C, n, m


def route(h2d, router_w, router_b):
    M = h2d.shape[0]
    probs = jax.nn.softmax((h2d @ router_w).astype(jnp.float32) + router_b, axis=-1)
    gp = probs.reshape(M, N_GROUPS, EXPERTS_PER_GROUP)
    g_score = jnp.sum(lax.top_k(gp, TOP_K)[0], axis=-1)
    g_sel = jnp.argmax(g_score, axis=-1).astype(jnp.int32)
    sel = gp[jnp.arange(M), g_sel]
    vals, idx = lax.top_k(sel, TOP_K)
    expert_idx = g_sel[:, None] * EXPERTS_PER_GROUP + idx.astype(jnp.int32)
    return expert_idx, vals / jnp.sum(vals, axis=-1, keepdims=True)


def moe_ffn(h2d, expert_idx, weights, w_in, w_out):
    M, D = h2d.shape
    A = M * TOP_K
    n_blocks = -(-(A + N_EXPERTS * (MOE_BLOCK - 1)) // MOE_BLOCK)
    P = n_blocks * MOE_BLOCK
    flat_e = expert_idx.reshape(A)
    order = jnp.argsort(flat_e).astype(jnp.int32)
    sorted_e = flat_e[order]
    counts = jnp.bincount(flat_e, length=N_EXPERTS).astype(jnp.int32)
    padded = (counts + MOE_BLOCK - 1) // MOE_BLOCK * MOE_BLOCK
    start = jnp.cumsum(counts) - counts
    pad_end = jnp.cumsum(padded)
    pad_start = pad_end - padded
    dest = pad_start[sorted_e] + jnp.arange(A, dtype=jnp.int32) - start[sorted_e]
    slot_tok = jnp.full((P,), M, jnp.int32).at[dest].set(order // TOP_K)
    slot_w = jnp.zeros((P,), jnp.float32).at[dest].set(weights.reshape(A)[order])
    block_e = jnp.minimum(jnp.searchsorted(pad_end, jnp.arange(n_blocks, dtype=jnp.int32) * MOE_BLOCK, side='right'), N_EXPERTS - 1).astype(jnp.int32)
    h_pad = jnp.concatenate([h2d, jnp.zeros((1, D), h2d.dtype)], axis=0)
    def expert_block(args):
        tok, e = args
        hb = h_pad[tok]
        gu = hb @ lax.dynamic_index_in_dim(w_in, e, 0, keepdims=False)
        act = jax.nn.silu(gu[:, :D_EXPERT]) * gu[:, D_EXPERT:]
        return act @ lax.dynamic_index_in_dim(w_out, e, 0, keepdims=False)
    yb = lax.map(expert_block, (slot_tok.reshape(n_blocks, MOE_BLOCK), block_e))
    y = jnp.zeros((M + 1, D), jnp.float32).at[slot_tok].add(yb.reshape(P, D).astype(jnp.float32) * slot_w[:, None])
    return y[:M].astype(h2d.dtype)


def decoder_layer(x, c, shift_prev, wkv0, C0, n0, m0, p, router_w, router_b):
    B, T, D = x.shape
    ada = (jax.nn.silu(c) @ p['ada_w'] + p['ada_b']).reshape(B, N_ADA, 1, D)
    sh1, sc1, g1, sh2, sc2, g2 = (ada[:, i] for i in range(N_ADA))
    xn = rms_norm(x, p['norm1_w']) * (1.0 + sc1) + sh1
    z = xn @ p['w_in']
    z_r = z[..., :RWKV_COLS]
    z_m = z[..., RWKV_COLS:RWKV_COLS + MLSTM_COLS]
    z_g = z[..., RWKV_COLS + MLSTM_COLS:]
    z_prev = shift_prev @ p['w_in'][:, :RWKV_COLS]
    y_r, wkv = rwkv7_branch(z_r, z_prev, p['rwkv_mu'], p['rwkv_w0'], p['rwkv_w2'], p['rwkv_a0'], p['rwkv_a2'],
                            p['rwkv_g2'], p['rwkv_k_k'], p['rwkv_k_a'], p['rwkv_r_k'], p['rwkv_lnx_w'],
                            p['rwkv_lnx_b'], wkv0)
    y_m, C, n, m = mlstm_branch(z_m, p['mlstm_i_b'], p['mlstm_f_b'], p['mlstm_norm_w'], C0, n0, m0)
    gate = jax.nn.sigmoid(z_g.astype(jnp.float32)).astype(x.dtype)
    merged = gate[..., :D] * (y_r @ p['w_branch_r']) + gate[..., D:] * (y_m @ p['w_branch_m'])
    x = x + g1 * (merged @ p['w_out'])
    hn = rms_norm(x, p['norm2_w']) * (1.0 + sc2) + sh2
    h2d = hn.reshape(B * T, D)
    e_idx, e_w = route(h2d, router_w, router_b)
    x = x + g2 * moe_ffn(h2d, e_idx, e_w, p['moe_w_in'], p['moe_w_out']).reshape(B, T, D)
    dt = x.dtype
    return x, xn[:, -1], wkv.astype(dt), C.astype(dt), n.astype(dt), m.astype(dt)


def setup_inputs(seed: int = 0) -> dict:
    key = jax.random.key(seed)
    ks = jax.random.split(key, 40)
    f32 = jnp.float32
    L = DEPTH
    def nrm(i, shape, scale):
        return jax.random.normal(ks[i], shape, f32) * scale
    def gain(i, shape):
        return 1.0 + 0.05 * jax.random.normal(ks[i], shape, f32)
    return {
        'x_prompt': nrm(0, (BATCH, SEQ, D_MODEL), 1.0),
        'x_sample': nrm(1, (DEC_BATCH, DEC_SEQ, D_MODEL), 1.0),
        'c_prompt': nrm(2, (BATCH, D_MODEL), 1.0),
        'c_sample': nrm(3, (DEC_BATCH, D_MODEL), 1.0),
        'state_wkv': nrm(4, (L, DEC_BATCH, N_RWKV_HEADS, RWKV_HEAD, RWKV_HEAD), 0.3),
        'state_shift': nrm(5, (L, DEC_BATCH, D_MODEL), 1.0),
        'state_C': nrm(6, (L, DEC_BATCH, N_MLSTM_HEADS, MLSTM_DK, MLSTM_DV), 0.1),
        'state_n': nrm(7, (L, DEC_BATCH, N_MLSTM_HEADS, MLSTM_DK), 0.1),
        'state_m': nrm(8, (L, DEC_BATCH, N_MLSTM_HEADS), 1.0),
        'ada_w': nrm(9, (L, D_MODEL, N_ADA * D_MODEL), 0.3 * D_MODEL ** -0.5),
        'ada_b': nrm(10, (L, N_ADA * D_MODEL), 0.02),
        'norm1_w': gain(11, (L, D_MODEL)),
        'norm2_w': gain(12, (L, D_MODEL)),
        'w_in': nrm(13, (L, D_MODEL, IN_COLS), D_MODEL ** -0.5),
        'rwkv_mu': jax.random.uniform(ks[14], (L, RWKV_COLS), f32),
        'rwkv_w0': nrm(15, (L, D_RWKV), 1.0) - 1.0,
        'rwkv_w2': nrm(16, (L, W_LORA, D_RWKV), 0.1 * W_LORA ** -0.5),
        'rwkv_a0': nrm(17, (L, D_RWKV), 0.5),
        'rwkv_a2': nrm(18, (L, A_LORA, D_RWKV), 0.1 * A_LORA ** -0.5),
        'rwkv_g2': nrm(19, (L, G_LORA, D_RWKV), G_LORA ** -0.5),
        'rwkv_k_k': 0.85 + nrm(20, (L, D_RWKV), 0.05),
        'rwkv_k_a': gain(21, (L, D_RWKV)),
        'rwkv_r_k': nrm(22, (L, N_RWKV_HEADS, RWKV_HEAD), 0.1),
        'rwkv_lnx_w': gain(23, (L, D_RWKV)),
        'rwkv_lnx_b': nrm(24, (L, D_RWKV), 0.02),
        'mlstm_i_b': nrm(25, (L, N_MLSTM_HEADS), 0.1),
        'mlstm_f_b': 3.0 + nrm(26, (L, N_MLSTM_HEADS), 0.5),
        'mlstm_norm_w': gain(27, (L, D_MLSTM)),
        'w_branch_r': nrm(28, (L, D_RWKV, D_MODEL), D_RWKV ** -0.5),
        'w_branch_m': nrm(29, (L, D_MLSTM, D_MODEL), D_MLSTM ** -0.5),
        'w_out': nrm(30, (L, D_MODEL, D_MODEL), D_MODEL ** -0.5),
        'router_w': nrm(31, (D_MODEL, N_EXPERTS), D_MODEL ** -0.5),
        'router_b': nrm(32, (N_EXPERTS,), 0.01),
        'moe_w_in': nrm(33, (L, N_EXPERTS, D_MODEL, 2 * D_EXPERT), D_MODEL ** -0.5),
        'moe_w_out': nrm(34, (L, N_EXPERTS, D_EXPERT, D_MODEL), D_EXPERT ** -0.5),
        'final_norm_w': gain(35, (D_MODEL,)),
    }


def reference(x_prompt, x_sample, c_prompt, c_sample, state_wkv, state_shift, state_C, state_n, state_m,
              ada_w, ada_b, norm1_w, norm2_w, w_in, rwkv_mu, rwkv_w0, rwkv_w2, rwkv_a0, rwkv_a2, rwkv_g2,
              rwkv_k_k, rwkv_k_a, rwkv_r_k, rwkv_lnx_w, rwkv_lnx_b, mlstm_i_b, mlstm_f_b, mlstm_norm_w,
              w_branch_r, w_branch_m, w_out, router_w, router_b, moe_w_in, moe_w_out, final_norm_w):
    dt = x_prompt.dtype
    B = x_prompt.shape[0]
    xp, xs = x_prompt, x_sample
    wkv_p, shift_p, C_p, n_p, m_p = [], [], [], [], []
    wkv_s, shift_s, C_s, n_s, m_s = [], [], [], [], []
    for l in range(DEPTH):
        p = {
            'ada_w': ada_w[l], 'ada_b': ada_b[l], 'norm1_w': norm1_w[l], 'norm2_w': norm2_w[l],
            'w_in': w_in[l], 'rwkv_mu': rwkv_mu[l], 'rwkv_w0': rwkv_w0[l], 'rwkv_w2': rwkv_w2[l],
            'rwkv_a0': rwkv_a0[l], 'rwkv_a2': rwkv_a2[l], 'rwkv_g2': rwkv_g2[l], 'rwkv_k_k': rwkv_k_k[l],
            'rwkv_k_a': rwkv_k_a[l], 'rwkv_r_k': rwkv_r_k[l], 'rwkv_lnx_w': rwkv_lnx_w[l],
            'rwkv_lnx_b': rwkv_lnx_b[l], 'mlstm_i_b': mlstm_i_b[l], 'mlstm_f_b': mlstm_f_b[l],
            'mlstm_norm_w': mlstm_norm_w[l], 'w_branch_r': w_branch_r[l], 'w_branch_m': w_branch_m[l],
            'w_out': w_out[l], 'moe_w_in': moe_w_in[l], 'moe_w_out': moe_w_out[l],
        }
        xp, sh, wk, C, n, m = decoder_layer(
            xp, c_prompt, jnp.zeros((B, D_MODEL), dt),
            jnp.zeros((B, N_RWKV_HEADS, RWKV_HEAD, RWKV_HEAD), dt),
            jnp.zeros((B, N_MLSTM_HEADS, MLSTM_DK, MLSTM_DV), dt),
            jnp.zeros((B, N_MLSTM_HEADS, MLSTM_DK), dt),
            jnp.zeros((B, N_MLSTM_HEADS), dt), p, router_w, router_b)
        wkv_p.append(wk); shift_p.append(sh); C_p.append(C); n_p.append(n); m_p.append(m)
        xs, sh, wk, C, n, m = decoder_layer(
            xs, c_sample, state_shift[l], state_wkv[l], state_C[l], state_n[l], state_m[l], p, router_w, router_b)
        wkv_s.append(wk); shift_s.append(sh); C_s.append(C); n_s.append(n); m_s.append(m)
    y_prompt = rms_norm(xp, final_norm_w)
    y_sample = rms_norm(xs, final_norm_w)
    return (y_prompt, y_sample,
            jnp.stack(wkv_p), jnp.stack(shift_p), jnp.stack(C_p), jnp.stack(n_p), jnp.stack(m_p),
            jnp.stack(wkv_s), jnp.stack(shift_s), jnp.stack(C_s), jnp.stack(n_s), jnp.stack(m_s))
```

```python
import functools
import math

import jax
import jax.numpy as jnp
from jax import lax
from jax.experimental import pallas as pl
from jax.experimental.pallas import tpu as pltpu

f32 = jnp.float32
bf16 = jnp.bfloat16
i32 = jnp.int32

D = 4096
DEPTH = 2
D_RWKV = 2048
HEAD_R = 64
NH_R = D_RWKV // HEAD_R
W_LORA = 96
A_LORA = 96
G_LORA = 256
LNX_EPS = 64e-5
D_MLSTM = 2048
NH_M = 8
DK = D_MLSTM // NH_M
MH_EPS = 1e-6
RWKV_COLS = 3 * D_RWKV + W_LORA + A_LORA + G_LORA
MLSTM_COLS = 4 * D_MLSTM + 2 * NH_M
N_EXPERTS = 32
EXPERTS_PER_GROUP = 8
N_GROUPS = 4
D_EXPERT = D // 4
N_ADA = 6
RMS_EPS = 1e-6

Z_RR, Z_RK, Z_RV = 0, 2048, 4096
Z_MQ, Z_MK, Z_MV, Z_MO = 6144, 8192, 10240, 12288
Z_GR, Z_GM = 14336, 18432
Z_SM = 22528
Z_IF = 23040
NZ = 23552
SM_W = 512
ZP_COLS = 3 * D_RWKV + SM_W

ROW_TILE = 256
SEQ_PER_TILE = 32
MOE_TM = 256
VMEM_LIMIT = 48 * 1024 * 1024

NN = (((1,), (0,)), ((), ()))
NT = (((1,), (1,)), ((), ()))
TN = (((0,), (0,)), ((), ()))


def _cparams(sem, vmem=VMEM_LIMIT):
    return pltpu.CompilerParams(dimension_semantics=sem, vmem_limit_bytes=vmem)


def _bdot(a, b, dims=NN):
    return lax.dot_general(a.astype(bf16), b.astype(bf16), dims, preferred_element_type=f32)


def _softplus(x):
    return jnp.maximum(x, 0.0) + jnp.log1p(jnp.exp(-jnp.abs(x)))


def _ada_kernel(c_ref, w_ref, b_ref, o_ref):
    c = c_ref[...]
    a = c * jax.nn.sigmoid(c)
    o_ref[0] = _bdot(a, w_ref[0]) + b_ref[0]


def _ada_call(c_all, ada_w, ada_b):
    nb = c_all.shape[0]
    n = ada_w.shape[2]
    tn = 512
    return pl.pallas_call(
        _ada_kernel,
        grid=(DEPTH, n // tn),
        in_specs=[
            pl.BlockSpec((nb, D), lambda l, j: (0, 0)),
            pl.BlockSpec((1, D, tn), lambda l, j: (l, 0, j)),
            pl.BlockSpec((1, 1, tn), lambda l, j: (l, 0, j)),
        ],
        out_specs=pl.BlockSpec((1, nb, tn), lambda l, j: (l, 0, j)),
        out_shape=jax.ShapeDtypeStruct((DEPTH, nb, n), f32),
        compiler_params=_cparams(("parallel", "parallel")),
        name="ada",
    )(c_all, ada_w, ada_b.reshape(DEPTH, 1, n))


def _mod_specs(n_prompt_tiles, n_prompt_seq, prompt_len, width, col=None):
    tiles_per_seq = prompt_len // ROW_TILE
    if col is None:
        pmap = lambda i: (jnp.minimum(i // tiles_per_seq, n_prompt_seq - 1), 0, 0)
        smap = lambda i: (jnp.maximum(i - n_prompt_tiles, 0), 0, 0)
    else:
        pmap = lambda j, i: (jnp.minimum(i // tiles_per_seq, n_prompt_seq - 1), 0, j)
        smap = lambda j, i: (jnp.maximum(i - n_prompt_tiles, 0), 0, j)
    return pl.BlockSpec((1, 1, width), pmap), pl.BlockSpec((SEQ_PER_TILE, 1, width), smap)


def _pick_mod(is_prompt, p_ref, s_ref):
    return jnp.where(is_prompt, jnp.broadcast_to(p_ref[...], s_ref.shape), s_ref[...])


def _rms(x, w):
    return x * lax.rsqrt(jnp.mean(x * x, axis=-1, keepdims=True) + RMS_EPS) * w


def _norm_mod_kernel(x_ref, w_ref, scp_ref, scs_ref, shp_ref, shs_ref, o_ref, *, n_prompt_tiles):
    is_prompt = pl.program_id(0) < n_prompt_tiles
    y = _rms(x_ref[...], w_ref[...]).reshape(SEQ_PER_TILE, 8, D)
    sc = _pick_mod(is_prompt, scp_ref, scs_ref)
    sh = _pick_mod(is_prompt, shp_ref, shs_ref)
    o_ref[...] = (y * (1.0 + sc) + sh).reshape(ROW_TILE, D).astype(o_ref.dtype)


def _norm_mod_call(x, w, sc_p, sc_s, sh_p, sh_s, prompt_len):
    m = x.shape[0]
    n_prompt_seq = sc_p.shape[0]
    npt = n_prompt_seq * prompt_len // ROW_TILE
    pspec, sspec = _mod_specs(npt, n_prompt_seq, prompt_len, D)
    return pl.pallas_call(
        functools.partial(_norm_mod_kernel, n_prompt_tiles=npt),
        grid=(m // ROW_TILE,),
        in_specs=[pl.BlockSpec((ROW_TILE, D), lambda i: (i, 0)), pl.BlockSpec((1, D), lambda i: (0, 0)),
                  pspec, sspec, pspec, sspec],
        out_specs=pl.BlockSpec((ROW_TILE, D), lambda i: (i, 0)),
        out_shape=jax.ShapeDtypeStruct((m, D), bf16),
        compiler_params=_cparams(("parallel",)),
        name="norm1",
    )(x, w.reshape(1, D), sc_p, sc_s, sh_p, sh_s)


def _norm_rows_kernel(x_ref, w_ref, sc_ref, sh_ref, o_ref):
    o_ref[...] = _rms(x_ref[...], w_ref[...]) * (1.0 + sc_ref[...]) + sh_ref[...]


def _norm_rows_call(x_rows, w, sc_rows, sh_rows):
    return pl.pallas_call(
        _norm_rows_kernel,
        out_shape=jax.ShapeDtypeStruct(x_rows.shape, f32),
        name="norm_rows",
    )(x_rows, w.reshape(1, D), sc_rows, sh_rows)


def _final_norm_kernel(x_ref, w_ref, o_ref):
    o_ref[...] = _rms(x_ref[...], w_ref[...])


def _final_norm_call(x, w, row0, rows):
    off = row0 // ROW_TILE
    return pl.pallas_call(
        _final_norm_kernel,
        grid=(rows // ROW_TILE,),
        in_specs=[pl.BlockSpec((ROW_TILE, D), lambda i: (i + off, 0)), pl.BlockSpec((1, D), lambda i: (0, 0))],
        out_specs=pl.BlockSpec((ROW_TILE, D), lambda i: (i, 0)),
        out_shape=jax.ShapeDtypeStruct((rows, D), f32),
        compiler_params=_cparams(("parallel",)),
        name="final_norm",
    )(x, w.reshape(1, D))


def _mm_kernel(a_ref, w_ref, o_ref):
    o_ref[...] = jnp.dot(a_ref[...], w_ref[...], preferred_element_type=f32)


def _matmul_call(a, w, tm, tn, n_out=None, col_map=None, name="matmul"):
    m, k = a.shape
    n_out = w.shape[1] if n_out is None else n_out
    wmap = (lambda j, i: (0, j)) if col_map is None else (lambda j, i: (0, col_map(j)))
    return pl.pallas_call(
        _mm_kernel,
        grid=(n_out // tn, m // tm),
        in_specs=[pl.BlockSpec((tm, k), lambda j, i: (i, 0)), pl.BlockSpec((k, tn), wmap)],
        out_specs=pl.BlockSpec((tm, tn), lambda j, i: (i, j)),
        out_shape=jax.ShapeDtypeStruct((m, n_out), f32),
        compiler_params=_cparams(("parallel", "parallel")),
        name=name,
    )(a, w)


def _merge_kernel(yr_ref, ym_ref, wr_ref, wm_ref, gr_ref, gm_ref, o_ref):
    pr = jnp.dot(yr_ref[...], wr_ref[...], preferred_element_type=f32)
    pm = jnp.dot(ym_ref[...], wm_ref[...], preferred_element_type=f32)
    o_ref[...] = (jax.nn.sigmoid(gr_ref[...]) * pr + jax.nn.sigmoid(gm_ref[...]) * pm).astype(o_ref.dtype)


def _merge_call(yr, ym, wr, wm, z):
    m = yr.shape[0]
    tm, tn = 512, 1024
    return pl.pallas_call(
        _merge_kernel,
        grid=(D // tn, m // tm),
        in_specs=[
            pl.BlockSpec((tm, D_RWKV), lambda j, i: (i, 0)),
            pl.BlockSpec((tm, D_MLSTM), lambda j, i: (i, 0)),
            pl.BlockSpec((D_RWKV, tn), lambda j, i: (0, j)),
            pl.BlockSpec((D_MLSTM, tn), lambda j, i: (0, j)),
            pl.BlockSpec((tm, tn), lambda j, i: (i, Z_GR // tn + j)),
            pl.BlockSpec((tm, tn), lambda j, i: (i, Z_GM // tn + j)),
        ],
        out_specs=pl.BlockSpec((tm, tn), lambda j, i: (i, j)),
        out_shape=jax.ShapeDtypeStruct((m, D), bf16),
        compiler_params=_cparams(("parallel", "parallel")),
        name="merge",
    )(yr, ym, wr, wm, z, z)


def _wout_kernel(a_ref, w_ref, x_ref, gp_ref, gs_ref, o_ref, *, n_prompt_tiles):
    is_prompt = pl.program_id(1) < n_prompt_tiles
    tn = o_ref.shape[1]
    acc = jnp.dot(a_ref[...], w_ref[...], preferred_element_type=f32).reshape(SEQ_PER_TILE, 8, tn)
    g = _pick_mod(is_prompt, gp_ref, gs_ref)
    o_ref[...] = (x_ref[...].reshape(SEQ_PER_TILE, 8, tn) + g * acc).reshape(ROW_TILE, tn)


def _wout_call(merged, w, x, g_p, g_s, prompt_len):
    m = x.shape[0]
    tn = 1024
    n_prompt_seq = g_p.shape[0]
    npt = n_prompt_seq * prompt_len // ROW_TILE
    pspec, sspec = _mod_specs(npt, n_prompt_seq, prompt_len, tn, col=True)
    return pl.pallas_call(
        functools.partial(_wout_kernel, n_prompt_tiles=npt),
        grid=(D // tn, m // ROW_TILE),
        in_specs=[
            pl.BlockSpec((ROW_TILE, D), lambda j, i: (i, 0)),
            pl.BlockSpec((D, tn), lambda j, i: (0, j)),
            pl.BlockSpec((ROW_TILE, tn), lambda j, i: (i, j)),
            pspec, sspec,
        ],
        out_specs=pl.BlockSpec((ROW_TILE, tn), lambda j, i: (i, j)),
        out_shape=jax.ShapeDtypeStruct((m, D), f32),
        compiler_params=_cparams(("parallel", "parallel")),
        name="wout",
    )(merged, w, x, g_p, g_s)


def _tri_masks(n):
    t = lax.broadcasted_iota(i32, (n, n), 0)
    s = lax.broadcasted_iota(i32, (n, n), 1)
    return s <= t, s < t, s == t


def _rwkv_chunk(r, lw, k, v, a, b, s0):
    n = r.shape[0]
    incl, strict, _ = _tri_masks(n)
    cum = lax.dot_general(incl.astype(f32), lw, NN, precision=lax.Precision.HIGHEST, preferred_element_type=f32)
    e_in = jnp.exp(cum)
    e_out = jnp.exp(-cum)
    at = a * jnp.exp(cum - lw)
    bt = b * e_out
    kt = k * e_out
    rt = r * e_in
    lhs = jnp.concatenate([at, rt], axis=0)
    rhs = jnp.concatenate([bt, kt], axis=0)
    g = _bdot(lhs, rhs, NT)
    n_mat = jnp.where(strict, g[:n, :n], 0.0)
    m_mat = jnp.where(strict, g[:n, n:], 0.0)
    pb = jnp.where(incl, g[n:, :n], 0.0)
    pk = jnp.where(incl, g[n:, n:], 0.0)
    a_s = _bdot(lhs, s0, NT)
    u = a_s[:n] + _bdot(m_mat, v)
    n_pow = n_mat
    lv = 1
    while lv < n:
        u = u + _bdot(n_pow, u)
        lv *= 2
        if lv < n:
            n_pow = _bdot(n_pow, n_pow)
    y = a_s[n:] + _bdot(pb, u) + _bdot(pk, v)
    c_last = cum[n - 1:n, :]
    tail = jnp.exp(c_last - cum)
    s_new = s0 * jnp.exp(c_last) + _bdot(u, b * tail, TN) + _bdot(v, k * tail, TN)
    return y, s_new


def _rwkv_kernel(zr_ref, zk_ref, zv_ref, zs_ref, pr_ref, pk_ref, pv_ref, ps_ref,
                 mur_ref, muk_ref, muv_ref, mus_ref, w0_ref, a0_ref, kk_ref, ka_ref, rk_ref, lw_ref, lb_ref,
                 w2_ref, a2_ref, g2_ref, s0_ref,
                 y_ref, so_ref,
                 s_scr, prev_scr, r_scr, w_scr, k_scr, v_scr, c_scr, e_scr, g_scr, o_scr,
                 *, nb, tb_rows, chunk, hb):
    tb = pl.program_id(2)
    n_tb = pl.num_programs(2)
    rows = nb * tb_rows
    per_seq = tb_rows // chunk

    wc = hb * HEAD_R

    @pl.when(tb == 0)
    def _():
        s_scr[...] = s0_ref[0]
        if nb == 1:
            for slot, (p_ref, width) in enumerate(((pr_ref, wc), (pk_ref, wc), (pv_ref, wc), (ps_ref, SM_W))):
                prev_scr[slot, 0:1, 0:width] = p_ref[0:1, :]

    def shifted(z_ref, p_ref, slot, width):
        z = z_ref[...]
        rolled = pltpu.roll(z, 1, 0)
        row = lax.broadcasted_iota(i32, z.shape, 0)
        if nb == 1:
            zs = jnp.where(row == 0, prev_scr[slot, 0:1, 0:width], rolled)
            prev_scr[slot, 0:1, 0:width] = z[rows - 1:rows, :]
        else:
            zs = jnp.where(row % tb_rows == 0, p_ref[...], rolled)
        return z, zs

    def lerp(z_ref, p_ref, mu_ref, slot, width):
        z, zs = shifted(z_ref, p_ref, slot, width)
        return z + (zs - z) * mu_ref[...]

    r = lerp(zr_ref, pr_ref, mur_ref, 0, wc)
    k = lerp(zk_ref, pk_ref, muk_ref, 1, wc)
    v = lerp(zv_ref, pv_ref, muv_ref, 2, wc)
    sm = lerp(zs_ref, ps_ref, mus_ref, 3, SM_W)
    gl = sm[:, 0:G_LORA]
    wl = sm[:, G_LORA:G_LORA + 128]
    al = sm[:, G_LORA + 128:G_LORA + 256]
    w_raw = -_softplus(-(w0_ref[...] + _bdot(jnp.tanh(wl), w2_ref[...]))) - 0.5
    eta = jax.nn.sigmoid(a0_ref[...] + _bdot(al, a2_ref[...]))
    r_scr[...] = r
    w_scr[...] = -jnp.exp(w_raw)
    k_scr[...] = k * (1.0 + (eta - 1.0) * ka_ref[...])
    v_scr[...] = v
    c_scr[...] = k * kk_ref[...]
    e_scr[...] = eta
    g_scr[...] = _bdot(jax.nn.sigmoid(gl), g2_ref[...])

    def body(it, carry):
        row0 = pl.multiple_of(it * chunk, chunk)
        sl = pl.ds(row0, chunk)
        seq = it // per_seq if nb > 1 else 0
        for h in range(hb):
            ls = slice(h * HEAD_R, (h + 1) * HEAD_R)
            r_h = r_scr[sl, ls]
            k_h = k_scr[sl, ls]
            v_h = v_scr[sl, ls]
            c_h = c_scr[sl, ls]
            c_h = c_h / jnp.maximum(jnp.sqrt(jnp.sum(c_h * c_h, axis=-1, keepdims=True)), 1e-12)
            y, s_new = _rwkv_chunk(r_h, w_scr[sl, ls], k_h, v_h, -c_h, c_h * e_scr[sl, ls], s_scr[seq, h])
            s_scr[seq, h] = s_new
            mean = jnp.mean(y, axis=-1, keepdims=True)
            var = jnp.mean(jnp.square(y - mean), axis=-1, keepdims=True)
            yn = (y - mean) * lax.rsqrt(var + LNX_EPS) * lw_ref[:, ls] + lb_ref[:, ls]
            bonus = jnp.sum(r_h * k_h * rk_ref[:, ls], axis=-1, keepdims=True) * v_h
            o_scr[sl, ls] = (yn + bonus) * g_scr[sl, ls]
        return carry

    lax.fori_loop(0, nb * per_seq, body, 0)
    y_ref[...] = o_scr[...].astype(y_ref.dtype)

    @pl.when(tb == n_tb - 1)
    def _():
        so_ref[...] = s_scr[...]


def _rwkv_call(z, zprev_rows, s0, layer, p, *, nseq, seq_len, row0, nb, tb_rows, chunk):
    hb = 4
    wc = hb * HEAD_R
    rows = nb * tb_rows
    n_tb = seq_len // tb_rows
    rb0 = row0 // rows
    zrow = lambda sb, h, tb: rb0 + sb * n_tb + tb

    def zspec(col0, width):
        cb = col0 // width
        if width == wc:
            return pl.BlockSpec((rows, width), lambda sb, h, tb: (zrow(sb, h, tb), cb + h))
        return pl.BlockSpec((rows, width), lambda sb, h, tb: (zrow(sb, h, tb), cb))

    def pspec(col0, width):
        cb = col0 // width
        if width == wc:
            return pl.BlockSpec((nb * 8, width), lambda sb, h, tb: (sb, cb + h))
        return pl.BlockSpec((nb * 8, width), lambda sb, h, tb: (sb, cb))

    vec = lambda: pl.BlockSpec((1, wc), lambda sb, h, tb: (0, h))
    vec_at = lambda col0: pl.BlockSpec((1, wc), lambda sb, h, tb: (0, col0 // wc + h))
    lora = lambda kdim: pl.BlockSpec((kdim, wc), lambda sb, h, tb: (0, h))
    kern = functools.partial(_rwkv_kernel, nb=nb, tb_rows=tb_rows, chunk=chunk, hb=hb)
    scr = lambda: pltpu.VMEM((rows, wc), f32)
    return pl.pallas_call(
        kern,
        grid=(nseq // nb, NH_R // hb, n_tb),
        in_specs=[
            zspec(Z_RR, wc), zspec(Z_RK, wc), zspec(Z_RV, wc), zspec(Z_SM, SM_W),
            pspec(0, wc), pspec(D_RWKV, wc), pspec(2 * D_RWKV, wc), pspec(3 * D_RWKV, SM_W),
            vec_at(0), vec_at(D_RWKV), vec_at(2 * D_RWKV),
            pl.BlockSpec((1, SM_W), lambda sb, h, tb: (0, 3 * D_RWKV // SM_W)),
            vec(), vec(), vec(), vec(), vec(), vec(), vec(),
            lora(128), lora(128), lora(G_LORA),
            pl.BlockSpec((1, nb, hb, HEAD_R, HEAD_R), lambda sb, h, tb: (layer, sb, h, 0, 0)),
        ],
        out_specs=[
            pl.BlockSpec((rows, wc), lambda sb, h, tb: (sb * n_tb + tb, h)),
            pl.BlockSpec((nb, hb, HEAD_R, HEAD_R), lambda sb, h, tb: (sb, h, 0, 0)),
        ],
        out_shape=[
            jax.ShapeDtypeStruct((nseq * seq_len, D_RWKV), bf16),
            jax.ShapeDtypeStruct((nseq, NH_R, HEAD_R, HEAD_R), f32),
        ],
        scratch_shapes=[
            pltpu.VMEM((nb, hb, HEAD_R, HEAD_R), f32),
            pltpu.VMEM((4, 8, SM_W), f32),
            scr(), scr(), scr(), scr(), scr(), scr(), scr(), scr(),
        ],
        compiler_params=_cparams(("parallel", "parallel", "arbitrary")),
        name="rwkv",
    )(z, z, z, z, zprev_rows, zprev_rows, zprev_rows, zprev_rows,
      p["mu"], p["mu"], p["mu"], p["mu"], p["w0"], p["a0"], p["k_k"], p["k_a"], p["r_k"], p["lnx_w"], p["lnx_b"],
      p["w2"], p["a2"], p["g2"], s0)


def _mlstm_kernel(zq_ref, zk_ref, zv_ref, zo_ref, gate_ref, bias_ref, nw_ref, c0_ref, n0_ref, m0_ref,
                  y_ref, co_ref, no_ref, mo_ref,
                  c_scr, n_scr, m_scr, o_scr, *, nb, tb_rows, chunk):
    tb = pl.program_id(2)
    n_tb = pl.num_programs(2)
    per_seq = tb_rows // chunk

    @pl.when(tb == 0)
    def _():
        c_scr[...] = c0_ref[0, :, 0]
        n_scr[...] = n0_ref[0, :, 0]
        m_scr[...] = m0_ref[0, :, 0]

    incl, _, eye = _tri_masks(chunk)
    upper = lax.broadcasted_iota(i32, (chunk, chunk), 0) <= lax.broadcasted_iota(i32, (chunk, chunk), 1)
    i_b = bias_ref[0, 0:1, 0:1]
    f_b = bias_ref[0, 1:2, 0:1]

    def body(it, carry):
        row0 = pl.multiple_of(it * chunk, chunk)
        sl = pl.ds(row0, chunk)
        seq = it // per_seq if nb > 1 else 0
        q = zq_ref[sl, :]
        k = zk_ref[sl, :] * (DK ** -0.5)
        v = zv_ref[sl, :]
        gch = gate_ref[0, it]
        ic_row = gch[0:1, :] + i_b
        fc_row = -_softplus(-(gch[1:2, :] + f_b))
        c_mat = c_scr[seq]
        n_row = n_scr[seq]
        m_prev = m_scr[seq][:, 0:1]
        fcb = jnp.broadcast_to(fc_row, (chunk, chunk))
        icb = jnp.broadcast_to(ic_row, (chunk, chunk))
        f_col = jnp.sum(jnp.where(incl, fcb, 0.0), axis=-1, keepdims=True)
        fc_col = jnp.sum(jnp.where(eye, fcb, 0.0), axis=-1, keepdims=True)
        ic_col = jnp.sum(jnp.where(eye, icb, 0.0), axis=-1, keepdims=True)
        f_row = jnp.sum(jnp.where(upper, jnp.broadcast_to(fc_col, (chunk, chunk)), 0.0), axis=0, keepdims=True)
        dm = jnp.where(incl, f_col - f_row + ic_row, -jnp.inf)
        inter = f_col + m_prev
        m_t = jnp.maximum(inter, jnp.max(dm, axis=-1, keepdims=True))
        w_intra = jnp.exp(dm - m_t)
        w_inter = jnp.exp(inter - m_t)
        s = _bdot(q, k, NT) * w_intra
        num = _bdot(s, v) + w_inter * _bdot(q, c_mat)
        den = jnp.sum(s, axis=-1, keepdims=True) + w_inter * jnp.sum(q * n_row, axis=-1, keepdims=True)
        h = num / jnp.maximum(jnp.abs(den), jnp.exp(-m_t))
        m_new = m_t[chunk - 1:chunk, :]
        f_last = f_col[chunk - 1:chunk, :]
        w_s = jnp.exp(f_last - f_col + ic_col - m_new)
        w_c = jnp.exp(inter[chunk - 1:chunk, :] - m_new)
        kw = k * w_s
        c_scr[seq] = w_c * c_mat + _bdot(kw, v, TN)
        n_scr[seq] = w_c * n_row + jnp.sum(kw, axis=0, keepdims=True)
        m_scr[seq] = jnp.broadcast_to(m_new, (1, 128))
        mean = jnp.mean(h, axis=-1, keepdims=True)
        var = jnp.mean(jnp.square(h - mean), axis=-1, keepdims=True)
        hn = (h - mean) * lax.rsqrt(var + MH_EPS) * nw_ref[...]
        o_scr[sl, :] = hn * jax.nn.sigmoid(zo_ref[sl, :])
        return carry

    lax.fori_loop(0, nb * per_seq, body, 0)
    y_ref[...] = o_scr[...].astype(y_ref.dtype)

    @pl.when(tb == n_tb - 1)
    def _():
        co_ref[:, 0] = c_scr[...]
        no_ref[:, 0] = n_scr[...]
        mo_ref[:, 0] = m_scr[...]


def _mlstm_call(z, gates, bias, norm_w, c0, n0, m0, layer, *, nseq, seq_len, row0, nb, tb_rows, chunk):
    rows = nb * tb_rows
    n_tb = seq_len // tb_rows
    rb0 = row0 // rows
    cpb = rows // chunk
    zspec = lambda col0: pl.BlockSpec((rows, DK), lambda sb, h, tb: (rb0 + sb * n_tb + tb, col0 // DK + h))
    st4 = lambda a, b: pl.BlockSpec((nb, 1, a, b), lambda sb, h, tb: (sb, h, 0, 0))
    st5 = lambda a, b: pl.BlockSpec((1, nb, 1, a, b), lambda sb, h, tb: (layer, sb, h, 0, 0))
    kern = functools.partial(_mlstm_kernel, nb=nb, tb_rows=tb_rows, chunk=chunk)
    return pl.pallas_call(
        kern,
        grid=(nseq // nb, NH_M, n_tb),
        in_specs=[
            zspec(Z_MQ), zspec(Z_MK), zspec(Z_MV), zspec(Z_MO),
            pl.BlockSpec((1, cpb, 2, chunk), lambda sb, h, tb: (h, sb * n_tb + tb, 0, 0)),
            pl.BlockSpec((1, 2, 128), lambda sb, h, tb: (h, 0, 0)),
            pl.BlockSpec((1, DK), lambda sb, h, tb: (0, h)),
            st5(DK, DK), st5(1, DK), st5(1, 128),
        ],
        out_specs=[
            pl.BlockSpec((rows, DK), lambda sb, h, tb: (sb * n_tb + tb, h)),
            st4(DK, DK), st4(1, DK), st4(1, 128),
        ],
        out_shape=[
            jax.ShapeDtypeStruct((nseq * seq_len, D_MLSTM), bf16),
            jax.ShapeDtypeStruct((nseq, NH_M, DK, DK), f32),
            jax.ShapeDtypeStruct((nseq, NH_M, 1, DK), f32),
            jax.ShapeDtypeStruct((nseq, NH_M, 1, 128), f32),
        ],
        scratch_shapes=[
            pltpu.VMEM((nb, DK, DK), f32),
            pltpu.VMEM((nb, 1, DK), f32),
            pltpu.VMEM((nb, 1, 128), f32),
            pltpu.VMEM((rows, DK), f32),
        ],
        compiler_params=_cparams(("parallel", "parallel", "arbitrary")),
        name="mlstm",
    )(z, z, z, z, gates, bias, norm_w, c0, n0, m0)


def _norm_route_kernel(x_ref, w_ref, scp_ref, scs_ref, shp_ref, shs_ref, rw_ref, rb_ref,
                       hn_ref, ei_ref, ew_ref, *, n_prompt_tiles):
    is_prompt = pl.program_id(0) < n_prompt_tiles
    y = _rms(x_ref[...], w_ref[...]).reshape(SEQ_PER_TILE, 8, D)
    sc = _pick_mod(is_prompt, scp_ref, scs_ref)
    sh = _pick_mod(is_prompt, shp_ref, shs_ref)
    hn = (y * (1.0 + sc) + sh).reshape(ROW_TILE, D)
    hn_ref[...] = hn
    logits = _bdot(hn, rw_ref[...]) + rb_ref[...]
    lane = lax.broadcasted_iota(i32, logits.shape, 1)
    lane_f = lane.astype(f32)
    valid = lane < N_EXPERTS
    lg = jnp.where(valid, logits, -1e30)
    ex = jnp.where(valid, jnp.exp(lg - jnp.max(lg, axis=-1, keepdims=True)), 0.0)
    probs = ex / jnp.sum(ex, axis=-1, keepdims=True)
    best = None
    for g in range(N_GROUPS):
        in_g = (lane >= g * EXPERTS_PER_GROUP) & (lane < (g + 1) * EXPERTS_PER_GROUP)
        pg = jnp.where(in_g, probs, -1.0)
        m1 = jnp.max(pg, axis=-1, keepdims=True)
        i1 = jnp.min(jnp.where(pg == m1, lane_f, 1e4), axis=-1, keepdims=True)
        pg2 = jnp.where(lane_f == i1, -1.0, pg)
        m2 = jnp.max(pg2, axis=-1, keepdims=True)
        i2 = jnp.min(jnp.where(pg2 == m2, lane_f, 1e4), axis=-1, keepdims=True)
        cand = (m1 + m2, m1, m2, i1, i2)
        if best is None:
            best = cand
        else:
            upd = cand[0] > best[0]
            best = tuple(jnp.where(upd, c, b) for c, b in zip(cand, best))
    _, m1, m2, i1, i2 = best
    tot = m1 + m2
    ei_ref[...] = jnp.where(lane == 0, i1, jnp.where(lane == 1, i2, 0.0)).astype(i32)
    ew_ref[...] = jnp.where(lane == 0, m1 / tot, jnp.where(lane == 1, m2 / tot, 0.0))


def _norm_route_call(x, w, sc_p, sc_s, sh_p, sh_s, rw, rb, prompt_len):
    m = x.shape[0]
    n_prompt_seq = sc_p.shape[0]
    npt = n_prompt_seq * prompt_len // ROW_TILE
    pspec, sspec = _mod_specs(npt, n_prompt_seq, prompt_len, D)
    row = lambda wd: pl.BlockSpec((ROW_TILE, wd), lambda i: (i, 0))
    return pl.pallas_call(
        functools.partial(_norm_route_kernel, n_prompt_tiles=npt),
        grid=(m // ROW_TILE,),
        in_specs=[row(D), pl.BlockSpec((1, D), lambda i: (0, 0)), pspec, sspec, pspec, sspec,
                  pl.BlockSpec((D, 128), lambda i: (0, 0)), pl.BlockSpec((1, 128), lambda i: (0, 0))],
        out_specs=[row(D), row(128), row(128)],
        out_shape=[jax.ShapeDtypeStruct((m, D), f32), jax.ShapeDtypeStruct((m, 128), i32),
                   jax.ShapeDtypeStruct((m, 128), f32)],
        compiler_params=_cparams(("parallel",)),
        name="norm2_route",
    )(x, w.reshape(1, D), sc_p, sc_s, sh_p, sh_s, rw, rb)


def _row_copy(src_hbm, row, dst, r, sem):
    return pltpu.make_async_copy(src_hbm.at[pl.ds(row, 1), :], dst.at[pl.ds(r, 1), :], sem)


def _gather_kernel(tok_ref, hn_hbm, o_ref, buf, sem):
    base = pl.program_id(0) * MOE_TM

    def issue(r, c):
        _row_copy(hn_hbm, tok_ref[base + r], buf, r, sem).start()
        return c

    def wait(r, c):
        _row_copy(hn_hbm, 0, buf, r, sem).wait()
        return c

    lax.fori_loop(0, MOE_TM, issue, 0)
    lax.fori_loop(0, MOE_TM, wait, 0)
    o_ref[...] = buf[...].astype(o_ref.dtype)


def _gather_call(hn, slot_tok):
    p = slot_tok.shape[0]
    return pl.pallas_call(
        _gather_kernel,
        grid_spec=pltpu.PrefetchScalarGridSpec(
            num_scalar_prefetch=1,
            grid=(p // MOE_TM,),
            in_specs=[pl.BlockSpec(memory_space=pl.ANY)],
            out_specs=pl.BlockSpec((MOE_TM, D), lambda j, tok: (j, 0)),
            scratch_shapes=[pltpu.VMEM((MOE_TM, D), f32), pltpu.SemaphoreType.DMA(())],
        ),
        out_shape=jax.ShapeDtypeStruct((p, D), bf16),
        compiler_params=_cparams(("arbitrary",)),
        name="moe_gather",
    )(slot_tok, hn)


def _moe1_kernel(be_ref, first_ref, valid_ref, x_ref, wg_ref, wu_ref, act_ref, wg_bf, wu_bf):
    j = pl.program_id(1)

    @pl.when(first_ref[j] == 1)
    def _():
        wg_bf[...] = wg_ref[0, 0].astype(bf16)
        wu_bf[...] = wu_ref[0, 0].astype(bf16)

    @pl.when(valid_ref[j] == 1)
    def _():
        x = x_ref[...]
        g = jnp.dot(x, wg_bf[...], preferred_element_type=f32)
        u = jnp.dot(x, wu_bf[...], preferred_element_type=f32)
        act_ref[...] = (g * jax.nn.sigmoid(g) * u).astype(act_ref.dtype)

    @pl.when(valid_ref[j] == 0)
    def _():
        act_ref[...] = jnp.zeros_like(act_ref)


def _moe1_call(xs, w_in, layer, block_e, first, valid):
    p = xs.shape[0]
    tn = 256
    n_t = D_EXPERT // tn
    return pl.pallas_call(
        _moe1_kernel,
        grid_spec=pltpu.PrefetchScalarGridSpec(
            num_scalar_prefetch=3,
            grid=(n_t, p // MOE_TM),
            in_specs=[
                pl.BlockSpec((MOE_TM, D), lambda n, j, be, fi, va: (j, 0)),
                pl.BlockSpec((1, 1, D, tn), lambda n, j, be, fi, va: (layer, be[j], 0, n)),
                pl.BlockSpec((1, 1, D, tn), lambda n, j, be, fi, va: (layer, be[j], 0, n_t + n)),
            ],
            out_specs=pl.BlockSpec((MOE_TM, tn), lambda n, j, be, fi, va: (j, n)),
            scratch_shapes=[pltpu.VMEM((D, tn), bf16), pltpu.VMEM((D, tn), bf16)],
        ),
        out_shape=jax.ShapeDtypeStruct((p, D_EXPERT), bf16),
        compiler_params=_cparams(("arbitrary", "arbitrary")),
        name="moe_up",
    )(block_e, first, valid, xs, w_in, w_in)


def _moe2_kernel(be_ref, first_ref, valid_ref, a_ref, w_ref, o_ref, w_bf):
    j = pl.program_id(1)

    @pl.when(first_ref[j] == 1)
    def _():
        w_bf[...] = w_ref[0, 0].astype(bf16)

    @pl.when(valid_ref[j] == 1)
    def _():
        o_ref[...] = jnp.dot(a_ref[...], w_bf[...], preferred_element_type=f32)

    @pl.when(valid_ref[j] == 0)
    def _():
        o_ref[...] = jnp.zeros_like(o_ref)


def _moe2_call(act, w_out, layer, block_e, first, valid):
    p = act.shape[0]
    tn = 1024
    return pl.pallas_call(
        _moe2_kernel,
        grid_spec=pltpu.PrefetchScalarGridSpec(
            num_scalar_prefetch=3,
            grid=(D // tn, p // MOE_TM),
            in_specs=[
                pl.BlockSpec((MOE_TM, D_EXPERT), lambda n, j, be, fi, va: (j, 0)),
                pl.BlockSpec((1, 1, D_EXPERT, tn), lambda n, j, be, fi, va: (layer, be[j], 0, n)),
            ],
            out_specs=pl.BlockSpec((MOE_TM, tn), lambda n, j, be, fi, va: (j, n)),
            scratch_shapes=[pltpu.VMEM((D_EXPERT, tn), bf16)],
        ),
        out_shape=jax.ShapeDtypeStruct((p, D), f32),
        compiler_params=_cparams(("arbitrary", "arbitrary")),
        name="moe_down",
    )(block_e, first, valid, act, w_out)


def _combine_kernel(pos_ref, x_ref, ew_ref, gp_ref, gs_ref, yb_hbm, o_ref, buf0, buf1, sem, *, n_prompt_tiles):
    i = pl.program_id(0)
    base = i * ROW_TILE

    def issue(r, c):
        _row_copy(yb_hbm, pos_ref[2 * (base + r)], buf0, r, sem).start()
        _row_copy(yb_hbm, pos_ref[2 * (base + r) + 1], buf1, r, sem).start()
        return c

    def wait(r, c):
        _row_copy(yb_hbm, 0, buf0, r, sem).wait()
        _row_copy(yb_hbm, 0, buf1, r, sem).wait()
        return c

    lax.fori_loop(0, ROW_TILE, issue, 0)
    lax.fori_loop(0, ROW_TILE, wait, 0)
    ew = ew_ref[...]
    y = buf0[...] * ew[:, 0:1] + buf1[...] * ew[:, 1:2]
    g = _pick_mod(i < n_prompt_tiles, gp_ref, gs_ref)
    o_ref[...] = (x_ref[...].reshape(SEQ_PER_TILE, 8, D) + g * y.reshape(SEQ_PER_TILE, 8, D)).reshape(ROW_TILE, D)


def _combine_call(x, yb, pos, ew, g_p, g_s, prompt_len):
    m = x.shape[0]
    n_prompt_seq = g_p.shape[0]
    npt = n_prompt_seq * prompt_len // ROW_TILE
    tiles_per_seq = prompt_len // ROW_TILE
    return pl.pallas_call(
        functools.partial(_combine_kernel, n_prompt_tiles=npt),
        grid_spec=pltpu.PrefetchScalarGridSpec(
            num_scalar_prefetch=1,
            grid=(m // ROW_TILE,),
            in_specs=[
                pl.BlockSpec((ROW_TILE, D), lambda i, pos: (i, 0)),
                pl.BlockSpec((ROW_TILE, 128), lambda i, pos: (i, 0)),
                pl.BlockSpec((1, 1, D), lambda i, pos: (jnp.minimum(i // tiles_per_seq, n_prompt_seq - 1), 0, 0)),
                pl.BlockSpec((SEQ_PER_TILE, 1, D), lambda i, pos: (jnp.maximum(i - npt, 0), 0, 0)),
                pl.BlockSpec(memory_space=pl.ANY),
            ],
            out_specs=pl.BlockSpec((ROW_TILE, D), lambda i, pos: (i, 0)),
            scratch_shapes=[pltpu.VMEM((ROW_TILE, D), f32), pltpu.VMEM((ROW_TILE, D), f32),
                            pltpu.SemaphoreType.DMA(())],
        ),
        out_shape=jax.ShapeDtypeStruct((m, D), f32),
        compiler_params=_cparams(("arbitrary",)),
        name="moe_combine",
    )(pos, x, ew, g_p, g_s, yb)


def _moe_plan(eidx, n_blocks):
    m = eidx.shape[0]
    a = 2 * m
    flat_e = eidx.reshape(a)
    order = jnp.argsort(flat_e).astype(i32)
    sorted_e = flat_e[order]
    counts = jnp.zeros((N_EXPERTS,), i32).at[flat_e].add(1)
    padded = (counts + MOE_TM - 1) // MOE_TM * MOE_TM
    start = jnp.cumsum(counts) - counts
    pad_end = jnp.cumsum(padded)
    pad_start = pad_end - padded
    dest = pad_start[sorted_e] + jnp.arange(a, dtype=i32) - start[sorted_e]
    slot_tok = jnp.zeros((n_blocks * MOE_TM,), i32).at[dest].set(order // 2)
    pos = jnp.zeros((a,), i32).at[order].set(dest)
    blk = jnp.arange(n_blocks, dtype=i32) * MOE_TM
    block_e = jnp.minimum(jnp.searchsorted(pad_end, blk, side="right"), N_EXPERTS - 1).astype(i32)
    valid = (blk < pad_end[-1]).astype(i32)
    first = jnp.concatenate([jnp.ones((1,), i32), (block_e[1:] != block_e[:-1]).astype(i32)])
    return slot_tok, pos, block_e, first, valid


def _prep_w_in(w):
    zeros = lambda n: jnp.zeros((w.shape[0], n), w.dtype)
    mb = RWKV_COLS
    gb = RWKV_COLS + MLSTM_COLS
    parts = [
        w[:, 0:2048], w[:, 2144:4192], w[:, 4192:6240],
        w[:, mb:mb + 2048], w[:, mb + 2048:mb + 4096], w[:, mb + 4096:mb + 6144], w[:, mb + 6160:mb + 8208],
        w[:, gb:gb + 8192],
        w[:, 6336:6592], w[:, 2048:2144], zeros(32), w[:, 6240:6336], zeros(32),
        w[:, mb + 6144:mb + 6160], zeros(112), zeros(NZ - Z_IF - 128),
    ]
    return jnp.concatenate(parts, axis=1).astype(bf16)


def _prep_mu(mu):
    zeros = jnp.zeros((32,), mu.dtype)
    return jnp.concatenate([mu[0:2048], mu[2144:4192], mu[4192:6240], mu[6336:6592], mu[2048:2144], zeros,
                            mu[6240:6336], zeros]).reshape(1, ZP_COLS)


def _pad_rows(w, rows):
    return jnp.concatenate([w, jnp.zeros((rows - w.shape[0], w.shape[1]), w.dtype)], axis=0).astype(bf16)


def _gate_rows(z, row0, rows, chunk):
    zif = z[row0:row0 + rows, Z_IF:Z_IF + 2 * NH_M].reshape(rows // chunk, chunk, 2, NH_M)
    return jnp.transpose(zif, (3, 0, 2, 1))


def kernel(x_prompt, x_sample, c_prompt, c_sample, state_wkv, state_shift, state_C, state_n, state_m, ada_w, ada_b, norm1_w, norm2_w, w_in, rwkv_mu, rwkv_w0, rwkv_w2, rwkv_a0, rwkv_a2, rwkv_g2, rwkv_k_k, rwkv_k_a, rwkv_r_k, rwkv_lnx_w, rwkv_lnx_b, mlstm_i_b, mlstm_f_b, mlstm_norm_w, w_branch_r, w_branch_m, w_out, router_w, router_b, moe_w_in, moe_w_out, final_norm_w):
    bp, tp, _ = x_prompt.shape
    bs, ts, _ = x_sample.shape
    mp, ms = bp * tp, bs * ts
    m = mp + ms
    assert ts == 8 and tp % ROW_TILE == 0 and ms % ROW_TILE == 0 and bs % SEQ_PER_TILE == 0 and m % 512 == 0
    n_seq = bp + bs
    n_seq_pad = -(-n_seq // 8) * 8

    x = jnp.concatenate([x_prompt.reshape(mp, D), x_sample.reshape(ms, D)], axis=0)
    c_all = jnp.concatenate([c_prompt, c_sample, jnp.zeros((n_seq_pad - n_seq, D), f32)], axis=0)
    ada = _ada_call(c_all, ada_w, ada_b).reshape(DEPTH, n_seq_pad, N_ADA, D)
    last_rows = jnp.concatenate([jnp.arange(bp, dtype=i32) * tp + tp - 1,
                                 mp + jnp.arange(bs, dtype=i32) * ts + ts - 1,
                                 jnp.zeros((n_seq_pad - n_seq,), i32)])
    router_w_p = jnp.concatenate([router_w, jnp.zeros((D, 128 - N_EXPERTS), f32)], axis=1).astype(bf16)
    router_b_p = jnp.concatenate([router_b, jnp.zeros((128 - N_EXPERTS,), f32)]).reshape(1, 128)
    n_blocks = -(-(2 * m + N_EXPERTS * (MOE_TM - 1)) // MOE_TM)

    prompt_geo = dict(nseq=bp, seq_len=tp, row0=0, nb=1, tb_rows=256, chunk=64)
    sample_geo = dict(nseq=bs, seq_len=ts, row0=mp, nb=16, tb_rows=8, chunk=8)
    sample_geo_m = dict(sample_geo, nb=8)

    outs_p = [[] for _ in range(5)]
    outs_s = [[] for _ in range(5)]
    for l in range(DEPTH):
        mod = lambda idx: (ada[l, :bp, idx][:, None, :], ada[l, bp:n_seq, idx][:, None, :])
        sh1, sc1, g1, sh2, sc2, g2 = (mod(i) for i in range(N_ADA))
        wz = _prep_w_in(w_in[l])

        xn = _norm_mod_call(x, norm1_w[l], sc1[0], sc1[1], sh1[0], sh1[1], tp)
        shift_rows = _norm_rows_call(x[last_rows], norm1_w[l], ada[l, :, 1], ada[l, :, 0])
        z = _matmul_call(xn, wz, 512, 1024, name="w_in")
        zprev_s = _matmul_call(state_shift[l].astype(bf16), wz, bs, 512, n_out=ZP_COLS,
                               col_map=lambda j: jnp.where(j < 12, j, Z_SM // 512), name="w_in_prev")
        rp = dict(mu=_prep_mu(rwkv_mu[l]), w0=rwkv_w0[l].reshape(1, -1), a0=rwkv_a0[l].reshape(1, -1),
                  k_k=rwkv_k_k[l].reshape(1, -1), k_a=rwkv_k_a[l].reshape(1, -1), r_k=rwkv_r_k[l].reshape(1, -1),
                  lnx_w=rwkv_lnx_w[l].reshape(1, -1), lnx_b=rwkv_lnx_b[l].reshape(1, -1),
                  w2=_pad_rows(rwkv_w2[l], 128), a2=_pad_rows(rwkv_a2[l], 128), g2=rwkv_g2[l].astype(bf16))
        yr_p, wkv_p = _rwkv_call(z, jnp.zeros((bp * 8, ZP_COLS), f32),
                                 jnp.zeros((1, bp, NH_R, HEAD_R, HEAD_R), f32), 0, rp, **prompt_geo)
        yr_s, wkv_s = _rwkv_call(z, jnp.repeat(zprev_s, 8, axis=0), state_wkv, l, rp, **sample_geo)
        bias = jnp.broadcast_to(jnp.stack([mlstm_i_b[l], mlstm_f_b[l]], axis=1)[:, :, None], (NH_M, 2, 128))
        nw = mlstm_norm_w[l].reshape(1, -1)
        ym_p, c_p, n_p, m_p = _mlstm_call(
            z, _gate_rows(z, 0, mp, 64), bias, nw, jnp.zeros((1, bp, NH_M, DK, DK), f32),
            jnp.zeros((1, bp, NH_M, 1, DK), f32), jnp.zeros((1, bp, NH_M, 1, 128), f32), 0, **prompt_geo)
        ym_s, c_s, n_s, m_s = _mlstm_call(
            z, _gate_rows(z, mp, ms, 8), bias, nw, state_C, state_n[:, :, :, None, :],
            jnp.broadcast_to(state_m[:, :, :, None, None], (DEPTH, bs, NH_M, 1, 128)), l, **sample_geo_m)
        merged = _merge_call(jnp.concatenate([yr_p, yr_s], axis=0), jnp.concatenate([ym_p, ym_s], axis=0),
                             w_branch_r[l].astype(bf16), w_branch_m[l].astype(bf16), z)
        x = _wout_call(merged, w_out[l].astype(bf16), x, g1[0], g1[1], tp)

        hn, eidx, ew = _norm_route_call(x, norm2_w[l], sc2[0], sc2[1], sh2[0], sh2[1], router_w_p, router_b_p, tp)
        slot_tok, pos, block_e, first, valid = _moe_plan(eidx[:, :2], n_blocks)
        xs = _gather_call(hn, slot_tok)
        act = _moe1_call(xs, moe_w_in, l, block_e, first, valid)
        yb = _moe2_call(act, moe_w_out, l, block_e, first, valid)
        x = _combine_call(x, yb, pos, ew, g2[0], g2[1], tp)

        for acc, vals in ((outs_p, (wkv_p, shift_rows[:bp], c_p, n_p[:, :, 0], m_p[:, :, 0, 0])),
                          (outs_s, (wkv_s, shift_rows[bp:n_seq], c_s, n_s[:, :, 0], m_s[:, :, 0, 0]))):
            for lst, v in zip(acc, vals):
                lst.append(v)

    y_prompt = _final_norm_call(x, final_norm_w, 0, mp).reshape(bp, tp, D)
    y_sample = _final_norm_call(x, final_norm_w, mp, ms).reshape(bs, ts, D)
    return (y_prompt, y_sample, *(jnp.stack(v) for v in outs_p), *(jnp.stack(v) for v in outs_s))
```

```python
import functools
import math

import jax
import jax.numpy as jnp
from jax import lax
from jax.experimental import pallas as pl
from jax.experimental.pallas import tpu as pltpu

f32 = jnp.float32
bf16 = jnp.bfloat16
i32 = jnp.int32

D = 4096
DEPTH = 2
D_RWKV = 2048
HEAD_R = 64
NH_R = D_RWKV // HEAD_R
W_LORA = 96
A_LORA = 96
G_LORA = 256
LNX_EPS = 64e-5
D_MLSTM = 2048
NH_M = 8
DK = D_MLSTM // NH_M
MH_EPS = 1e-6
RWKV_COLS = 3 * D_RWKV + W_LORA + A_LORA + G_LORA
MLSTM_COLS = 4 * D_MLSTM + 2 * NH_M
N_EXPERTS = 32
EXPERTS_PER_GROUP = 8
N_GROUPS = 4
D_EXPERT = D // 4
N_ADA = 6
RMS_EPS = 1e-6

Z_RR, Z_RK, Z_RV = 0, 2048, 4096
Z_MQ, Z_MK, Z_MV, Z_MO = 6144, 8192, 10240, 12288
Z_GR, Z_GM = 14336, 18432
Z_SM = 22528
Z_IF = 23040
NZ = 23552
SM_W = 512
ZP_COLS = 3 * D_RWKV + SM_W

ROW_TILE = 256
SEQ_PER_TILE = 32
MOE_TM = 256
VMEM_LIMIT = 48 * 1024 * 1024

NN = (((1,), (0,)), ((), ()))
NT = (((1,), (1,)), ((), ()))
TN = (((0,), (0,)), ((), ()))


def _cparams(sem, vmem=VMEM_LIMIT):
    return pltpu.CompilerParams(dimension_semantics=sem, vmem_limit_bytes=vmem)


def _bdot(a, b, dims=NN):
    return lax.dot_general(a.astype(bf16), b.astype(bf16), dims, preferred_element_type=f32)


def _softplus(x):
    return jnp.maximum(x, 0.0) + jnp.log1p(jnp.exp(-jnp.abs(x)))


def _ada_kernel(c_ref, w_ref, b_ref, o_ref):
    c = c_ref[...]
    a = c * jax.nn.sigmoid(c)
    o_ref[0] = _bdot(a, w_ref[0]) + b_ref[0]


def _ada_call(c_all, ada_w, ada_b):
    nb = c_all.shape[0]
    n = ada_w.shape[2]
    tn = 512
    return pl.pallas_call(
        _ada_kernel,
        grid=(DEPTH, n // tn),
        in_specs=[
            pl.BlockSpec((nb, D), lambda l, j: (0, 0)),
            pl.BlockSpec((1, D, tn), lambda l, j: (l, 0, j)),
            pl.BlockSpec((1, 1, tn), lambda l, j: (l, 0, j)),
        ],
        out_specs=pl.BlockSpec((1, nb, tn), lambda l, j: (l, 0, j)),
        out_shape=jax.ShapeDtypeStruct((DEPTH, nb, n), f32),
        compiler_params=_cparams(("parallel", "parallel")),
        name="ada",
    )(c_all, ada_w, ada_b.reshape(DEPTH, 1, n))


def _mod_specs(n_prompt_tiles, n_prompt_seq, prompt_len, width, col=None):
    tiles_per_seq = prompt_len // ROW_TILE
    if col is None:
        pmap = lambda i: (jnp.minimum(i // tiles_per_seq, n_prompt_seq - 1), 0, 0)
        smap = lambda i: (jnp.maximum(i - n_prompt_tiles, 0), 0, 0)
    else:
        pmap = lambda j, i: (jnp.minimum(i // tiles_per_seq, n_prompt_seq - 1), 0, j)
        smap = lambda j, i: (jnp.maximum(i - n_prompt_tiles, 0), 0, j)
    return pl.BlockSpec((1, 1, width), pmap), pl.BlockSpec((SEQ_PER_TILE, 1, width), smap)


def _pick_mod(is_prompt, p_ref, s_ref):
    return jnp.where(is_prompt, jnp.broadcast_to(p_ref[...], s_ref.shape), s_ref[...])


def _rms(x, w):
    return x * lax.rsqrt(jnp.mean(x * x, axis=-1, keepdims=True) + RMS_EPS) * w


def _norm_mod_kernel(x_ref, w_ref, scp_ref, scs_ref, shp_ref, shs_ref, o_ref, *, n_prompt_tiles):
    is_prompt = pl.program_id(0) < n_prompt_tiles
    y = _rms(x_ref[...], w_ref[...]).reshape(SEQ_PER_TILE, 8, D)
    sc = _pick_mod(is_prompt, scp_ref, scs_ref)
    sh = _pick_mod(is_prompt, shp_ref, shs_ref)
    o_ref[...] = (y * (1.0 + sc) + sh).reshape(ROW_TILE, D).astype(o_ref.dtype)


def _norm_mod_call(x, w, sc_p, sc_s, sh_p, sh_s, prompt_len):
    m = x.shape[0]
    n_prompt_seq = sc_p.shape[0]
    npt = n_prompt_seq * prompt_len // ROW_TILE
    pspec, sspec = _mod_specs(npt, n_prompt_seq, prompt_len, D)
    return pl.pallas_call(
        functools.partial(_norm_mod_kernel, n_prompt_tiles=npt),
        grid=(m // ROW_TILE,),
        in_specs=[pl.BlockSpec((ROW_TILE, D), lambda i: (i, 0)), pl.BlockSpec((1, D), lambda i: (0, 0)),
                  pspec, sspec, pspec, sspec],
        out_specs=pl.BlockSpec((ROW_TILE, D), lambda i: (i, 0)),
        out_shape=jax.ShapeDtypeStruct((m, D), bf16),
        compiler_params=_cparams(("parallel",)),
        name="norm1",
    )(x, w.reshape(1, D), sc_p, sc_s, sh_p, sh_s)


def _norm_rows_kernel(x_ref, w_ref, sc_ref, sh_ref, o_ref):
    o_ref[...] = _rms(x_ref[...], w_ref[...]) * (1.0 + sc_ref[...]) + sh_ref[...]


def _norm_rows_call(x_rows, w, sc_rows, sh_rows):
    return pl.pallas_call(
        _norm_rows_kernel,
        out_shape=jax.ShapeDtypeStruct(x_rows.shape, f32),
        name="norm_rows",
    )(x_rows, w.reshape(1, D), sc_rows, sh_rows)


def _final_norm_kernel(x_ref, w_ref, o_ref):
    o_ref[...] = _rms(x_ref[...], w_ref[...])


def _final_norm_call(x, w, row0, rows):
    off = row0 // ROW_TILE
    return pl.pallas_call(
        _final_norm_kernel,
        grid=(rows // ROW_TILE,),
        in_specs=[pl.BlockSpec((ROW_TILE, D), lambda i: (i + off, 0)), pl.BlockSpec((1, D), lambda i: (0, 0))],
        out_specs=pl.BlockSpec((ROW_TILE, D), lambda i: (i, 0)),
        out_shape=jax.ShapeDtypeStruct((rows, D), f32),
        compiler_params=_cparams(("parallel",)),
        name="final_norm",
    )(x, w.reshape(1, D))


def _mm_kernel(a_ref, w_ref, o_ref):
    o_ref[...] = jnp.dot(a_ref[...], w_ref[...], preferred_element_type=f32)


def _matmul_call(a, w, tm, tn, n_out=None, col_map=None, name="matmul"):
    m, k = a.shape
    n_out = w.shape[1] if n_out is None else n_out
    wmap = (lambda j, i: (0, j)) if col_map is None else (lambda j, i: (0, col_map(j)))
    return pl.pallas_call(
        _mm_kernel,
        grid=(n_out // tn, m // tm),
        in_specs=[pl.BlockSpec((tm, k), lambda j, i: (i, 0)), pl.BlockSpec((k, tn), wmap)],
        out_specs=pl.BlockSpec((tm, tn), lambda j, i: (i, j)),
        out_shape=jax.ShapeDtypeStruct((m, n_out), f32),
        compiler_params=_cparams(("parallel", "parallel")),
        name=name,
    )(a, w)


def _merge_kernel(yr_ref, ym_ref, wr_ref, wm_ref, gr_ref, gm_ref, o_ref):
    pr = jnp.dot(yr_ref[...], wr_ref[...], preferred_element_type=f32)
    pm = jnp.dot(ym_ref[...], wm_ref[...], preferred_element_type=f32)
    o_ref[...] = (jax.nn.sigmoid(gr_ref[...]) * pr + jax.nn.sigmoid(gm_ref[...]) * pm).astype(o_ref.dtype)


def _merge_call(yr, ym, wr, wm, z):
    m = yr.shape[0]
    tm, tn = 512, 1024
    return pl.pallas_call(
        _merge_kernel,
        grid=(D // tn, m // tm),
        in_specs=[
            pl.BlockSpec((tm, D_RWKV), lambda j, i: (i, 0)),
            pl.BlockSpec((tm, D_MLSTM), lambda j, i: (i, 0)),
            pl.BlockSpec((D_RWKV, tn), lambda j, i: (0, j)),
            pl.BlockSpec((D_MLSTM, tn), lambda j, i: (0, j)),
            pl.BlockSpec((tm, tn), lambda j, i: (i, Z_GR // tn + j)),
            pl.BlockSpec((tm, tn), lambda j, i: (i, Z_GM // tn + j)),
        ],
        out_specs=pl.BlockSpec((tm, tn), lambda j, i: (i, j)),
        out_shape=jax.ShapeDtypeStruct((m, D), bf16),
        compiler_params=_cparams(("parallel", "parallel")),
        name="merge",
    )(yr, ym, wr, wm, z, z)


def _wout_kernel(a_ref, w_ref, x_ref, gp_ref, gs_ref, o_ref, *, n_prompt_tiles):
    is_prompt = pl.program_id(1) < n_prompt_tiles
    tn = o_ref.shape[1]
    acc = jnp.dot(a_ref[...], w_ref[...], preferred_element_type=f32).reshape(SEQ_PER_TILE, 8, tn)
    g = _pick_mod(is_prompt, gp_ref, gs_ref)
    o_ref[...] = (x_ref[...].reshape(SEQ_PER_TILE, 8, tn) + g * acc).reshape(ROW_TILE, tn)


def _wout_call(merged, w, x, g_p, g_s, prompt_len):
    m = x.shape[0]
    tn = 1024
    n_prompt_seq = g_p.shape[0]
    npt = n_prompt_seq * prompt_len // ROW_TILE
    pspec, sspec = _mod_specs(npt, n_prompt_seq, prompt_len, tn, col=True)
    return pl.pallas_call(
        functools.partial(_wout_kernel, n_prompt_tiles=npt),
        grid=(D // tn, m // ROW_TILE),
        in_specs=[
            pl.BlockSpec((ROW_TILE, D), lambda j, i: (i, 0)),
            pl.BlockSpec((D, tn), lambda j, i: (0, j)),
            pl.BlockSpec((ROW_TILE, tn), lambda j, i: (i, j)),
            pspec, sspec,
        ],
        out_specs=pl.BlockSpec((ROW_TILE, tn), lambda j, i: (i, j)),
        out_shape=jax.ShapeDtypeStruct((m, D), f32),
        compiler_params=_cparams(("parallel", "parallel")),
        name="wout",
    )(merged, w, x, g_p, g_s)


def _tri_masks(n):
    t = lax.broadcasted_iota(i32, (n, n), 0)
    s = lax.broadcasted_iota(i32, (n, n), 1)
    return s <= t, s < t, s == t


def _rwkv_local(at, rt, bt, kt, v, bh, kh, lam):
    n = at[0].shape[0]
    incl, strict, eye = _tri_masks(n)
    eye_k = _tri_masks(HEAD_R)[2]
    each = lambda f, *ls: [f(*xs) for xs in zip(*ls)]
    g = each(lambda a_, r_, b_, k_: _bdot(jnp.concatenate([a_, r_], axis=0), jnp.concatenate([b_, k_], axis=0), NT),
             at, rt, bt, kt)
    n_mat = each(lambda x: jnp.where(strict, x[:n, :n], 0.0), g)
    m_mat = each(lambda x: jnp.where(strict, x[:n, n:], 0.0), g)
    pb = each(lambda x: jnp.where(incl, x[n:, :n], 0.0), g)
    pk = each(lambda x: jnp.where(incl, x[n:, n:], 0.0), g)
    t_mat = each(lambda x: jnp.where(eye, 1.0, x), n_mat)
    n_pow = n_mat
    lv = 2
    while lv < n:
        n_pow = each(lambda x: _bdot(x, x), n_pow)
        t_mat = each(lambda t_, p_: t_ + _bdot(t_, p_), t_mat, n_pow)
        lv *= 2
    tam = each(lambda t_, a_, m_: _bdot(t_, jnp.concatenate([a_, m_], axis=1)), t_mat, at, m_mat)
    qw = each(_bdot, pb, tam)
    q = each(lambda r_, x: r_ + x[:, :HEAD_R], rt, qw)
    w_loc = each(lambda x, p_, v_: _bdot(x[:, HEAD_R:] + p_, v_), qw, pk, v)
    ab = each(lambda x, b_: _bdot(x, b_, TN), tam, bh)
    a_tr = each(lambda l_, x: jnp.where(eye_k, l_, 0.0) + x[:HEAD_R], lam, ab)
    s_inc = each(lambda v_, k_, x: _bdot(v_, k_ + x[HEAD_R:], TN), v, kh, ab)
    return q, w_loc, a_tr, s_inc


def _split3(x):
    hi = x.astype(bf16)
    r1 = x - hi.astype(f32)
    mid = r1.astype(bf16)
    lo = (r1 - mid.astype(f32)).astype(bf16)
    return hi, mid, lo


def _dot3_left(w, x):
    return sum(jnp.dot(w, p, preferred_element_type=f32) for p in _split3(x))


def _dot3_right(x, w):
    return sum(jnp.dot(p, w, preferred_element_type=f32) for p in _split3(x))


def _rwkv_kernel(zr_ref, zk_ref, zv_ref, zs_ref, pr_ref, pk_ref, pv_ref, ps_ref,
                 mur_ref, muk_ref, muv_ref, mus_ref, w0_ref, a0_ref, kk_ref, ka_ref, rk_ref, lw_ref, lb_ref,
                 w2_ref, a2_ref, g2_ref, s0_ref,
                 y_ref, so_ref,
                 s_scr, prev_scr, at_scr, rt_scr, bt_scr, kt_scr, v_scr, bh_scr, kh_scr, lam_scr,
                 q_scr, wl_scr, y_scr, atr_scr, sinc_scr,
                 *, nb, tb_rows, chunk, hb, local):
    tb = pl.program_id(2)
    n_tb = pl.num_programs(2)
    rows = nb * tb_rows
    per_seq = tb_rows // chunk
    n_chunks = rows // chunk
    wc = hb * HEAD_R

    @pl.when(tb == 0)
    def _():
        if nb == 1:
            s_scr[...] = s0_ref[0]
            for slot, (p_ref, width) in enumerate(((pr_ref, wc), (pk_ref, wc), (pv_ref, wc), (ps_ref, SM_W))):
                prev_scr[slot, 0:1, 0:width] = p_ref[0:1, :]

    def shifted(z_ref, p_ref, slot, width):
        z = z_ref[...]
        rolled = pltpu.roll(z, 1, 0)
        row = lax.broadcasted_iota(i32, z.shape, 0)
        if nb == 1:
            zs = jnp.where(row == 0, prev_scr[slot, 0:1, 0:width], rolled)
            prev_scr[slot, 0:1, 0:width] = z[rows - 1:rows, :]
        else:
            zs = jnp.where(row % tb_rows == 0, p_ref[...], rolled)
        return z, zs

    def lerp(z_ref, p_ref, mu_ref, slot, width):
        z, zs = shifted(z_ref, p_ref, slot, width)
        return z + (zs - z) * mu_ref[...]

    r = lerp(zr_ref, pr_ref, mur_ref, 0, wc)
    k = lerp(zk_ref, pk_ref, muk_ref, 1, wc)
    v = lerp(zv_ref, pv_ref, muv_ref, 2, wc)
    sm = lerp(zs_ref, ps_ref, mus_ref, 3, SM_W)
    gl = sm[:, 0:G_LORA]
    wl = sm[:, G_LORA:G_LORA + 128]
    al = sm[:, G_LORA + 128:G_LORA + 256]
    w_raw = -_softplus(-(w0_ref[...] + _bdot(jnp.tanh(wl), w2_ref[...]))) - 0.5
    eta = jax.nn.sigmoid(a0_ref[...] + _bdot(al, a2_ref[...]))
    lw = -jnp.exp(w_raw)
    k2 = k * (1.0 + (eta - 1.0) * ka_ref[...])

    shift = chunk.bit_length() - 1
    ti = lax.broadcasted_iota(i32, (rows, rows), 0)
    si = lax.broadcasted_iota(i32, (rows, rows), 1)
    same_chunk = (ti >> shift) == (si >> shift)
    tri_sel = jnp.where(same_chunk & (si <= ti), 1.0, 0.0).astype(bf16)
    all_sel = jnp.where(same_chunk, 1.0, 0.0).astype(bf16)
    hshift = HEAD_R.bit_length() - 1
    head_sel = (lax.broadcasted_iota(i32, (wc, wc), 0) >> hshift) == (lax.broadcasted_iota(i32, (wc, wc), 1) >> hshift)
    head_sum = jnp.where(head_sel, 1.0, 0.0).astype(bf16)
    head_mean = jnp.where(head_sel, 1.0 / HEAD_R, 0.0).astype(bf16)

    c = k * kk_ref[...]
    c = c / jnp.maximum(jnp.sqrt(_dot3_right(c * c, head_sum)), 1e-12)
    b_vec = c * eta
    cum = _dot3_left(tri_sel, lw)
    tot = _dot3_left(all_sel, lw)
    e_out = jnp.exp(-cum)
    tail = jnp.exp(tot - cum)
    at_scr[...] = -c * jnp.exp(cum - lw)
    rt_scr[...] = r * jnp.exp(cum)
    bt_scr[...] = b_vec * e_out
    kt_scr[...] = k2 * e_out
    bh_scr[...] = b_vec * tail
    kh_scr[...] = k2 * tail
    lam_scr[...] = jnp.exp(tot)
    v_scr[...] = v
    bonus = _dot3_right(r * k2 * rk_ref[...], head_sum) * v
    gate = _bdot(jax.nn.sigmoid(gl), g2_ref[...])

    def local_body(it, carry):
        where = []
        for j in range(local):
            ci = it * local + j
            sl = pl.ds(pl.multiple_of(ci * chunk, chunk), chunk)
            where += [(ci, sl, h, slice(h * HEAD_R, (h + 1) * HEAD_R)) for h in range(hb)]
        load = lambda ref: [ref[sl, ls] for _, sl, _, ls in where]
        lam = [lam_scr[pl.ds(ci * chunk, 1), ls] for ci, _, _, ls in where]
        q, w_loc, a_tr, s_inc = _rwkv_local(load(at_scr), load(rt_scr), load(bt_scr), load(kt_scr), load(v_scr),
                                            load(bh_scr), load(kh_scr), lam)
        for i, (ci, sl, h, ls) in enumerate(where):
            q_scr[sl, ls] = q[i]
            wl_scr[sl, ls] = w_loc[i]
            atr_scr[ci, h] = a_tr[i]
            sinc_scr[ci, h] = s_inc[i]
        return carry

    lax.fori_loop(0, n_chunks // local, local_body, 0)

    def advance(ci, state):
        sl = pl.ds(ci * chunk if isinstance(ci, int) else pl.multiple_of(ci * chunk, chunk), chunk)
        ys, new_state = [], []
        for h in range(hb):
            ls = slice(h * HEAD_R, (h + 1) * HEAD_R)
            ys.append(_bdot(q_scr[sl, ls], state[h], NT) + wl_scr[sl, ls])
            new_state.append(_bdot(state[h], atr_scr[ci, h]) + sinc_scr[ci, h])
        for h in range(hb):
            y_scr[sl, h * HEAD_R:(h + 1) * HEAD_R] = ys[h]
        return new_state

    if nb == 1:
        state = [s_scr[0, h] for h in range(hb)]
        for ci in range(n_chunks):
            state = advance(ci, state)
        for h in range(hb):
            s_scr[0, h] = state[h]

        @pl.when(tb == n_tb - 1)
        def _():
            so_ref[...] = s_scr[...]
    else:
        def seq_body(it, carry):
            seqs = [it * local + j for j in range(local)]
            states = [[s0_ref[0, ci, h] for h in range(hb)] for ci in seqs]
            states = [advance(ci, st) for ci, st in zip(seqs, states)]
            for ci, st in zip(seqs, states):
                for h in range(hb):
                    so_ref[ci, h] = st[h]
            return carry

        lax.fori_loop(0, n_chunks // local, seq_body, 0)

    y = y_scr[...]
    dev = y - _dot3_right(y, head_mean)
    var = _dot3_right(dev * dev, head_mean)
    yn = dev * lax.rsqrt(var + LNX_EPS) * lw_ref[...] + lb_ref[...]
    y_ref[...] = ((yn + bonus) * gate).astype(y_ref.dtype)


def _rwkv_call(z, zprev_rows, s0, layer, p, *, nseq, seq_len, row0, nb, tb_rows, chunk, local):
    hb = 4
    wc = hb * HEAD_R
    rows = nb * tb_rows
    n_tb = seq_len // tb_rows
    assert nb == 1 or (tb_rows == chunk and n_tb == 1)
    rb0 = row0 // rows
    zrow = lambda sb, h, tb: rb0 + sb * n_tb + tb

    def zspec(col0, width):
        cb = col0 // width
        if width == wc:
            return pl.BlockSpec((rows, width), lambda sb, h, tb: (zrow(sb, h, tb), cb + h))
        return pl.BlockSpec((rows, width), lambda sb, h, tb: (zrow(sb, h, tb), cb))

    def pspec(col0, width):
        cb = col0 // width
        if width == wc:
            return pl.BlockSpec((nb * 8, width), lambda sb, h, tb: (sb, cb + h))
        return pl.BlockSpec((nb * 8, width), lambda sb, h, tb: (sb, cb))

    vec = lambda: pl.BlockSpec((1, wc), lambda sb, h, tb: (0, h))
    vec_at = lambda col0: pl.BlockSpec((1, wc), lambda sb, h, tb: (0, col0 // wc + h))
    lora = lambda kdim: pl.BlockSpec((kdim, wc), lambda sb, h, tb: (0, h))
    kern = functools.partial(_rwkv_kernel, nb=nb, tb_rows=tb_rows, chunk=chunk, hb=hb, local=local)
    scr = lambda: pltpu.VMEM((rows, wc), f32)
    return pl.pallas_call(
        kern,
        grid=(nseq // nb, NH_R // hb, n_tb),
        in_specs=[
            zspec(Z_RR, wc), zspec(Z_RK, wc), zspec(Z_RV, wc), zspec(Z_SM, SM_W),
            pspec(0, wc), pspec(D_RWKV, wc), pspec(2 * D_RWKV, wc), pspec(3 * D_RWKV, SM_W),
            vec_at(0), vec_at(D_RWKV), vec_at(2 * D_RWKV),
            pl.BlockSpec((1, SM_W), lambda sb, h, tb: (0, 3 * D_RWKV // SM_W)),
            vec(), vec(), vec(), vec(), vec(), vec(), vec(),
            lora(128), lora(128), lora(G_LORA),
            pl.BlockSpec((1, nb, hb, HEAD_R, HEAD_R), lambda sb, h, tb: (layer, sb, h, 0, 0)),
        ],
        out_specs=[
            pl.BlockSpec((rows, wc), lambda sb, h, tb: (sb * n_tb + tb, h)),
            pl.BlockSpec((nb, hb, HEAD_R, HEAD_R), lambda sb, h, tb: (sb, h, 0, 0)),
        ],
        out_shape=[
            jax.ShapeDtypeStruct((nseq * seq_len, D_RWKV), bf16),
            jax.ShapeDtypeStruct((nseq, NH_R, HEAD_R, HEAD_R), f32),
        ],
        scratch_shapes=[
            pltpu.VMEM((nb, hb, HEAD_R, HEAD_R), f32),
            pltpu.VMEM((4, 8, SM_W), f32),
            *[scr() for _ in range(11)],
            pltpu.VMEM((rows // chunk, hb, HEAD_R, HEAD_R), f32),
            pltpu.VMEM((rows // chunk, hb, HEAD_R, HEAD_R), f32),
        ],
        compiler_params=_cparams(("parallel", "parallel", "arbitrary")),
        name="rwkv",
    )(z, z, z, z, zprev_rows, zprev_rows, zprev_rows, zprev_rows,
      p["mu"], p["mu"], p["mu"], p["mu"], p["w0"], p["a0"], p["k_k"], p["k_a"], p["r_k"], p["lnx_w"], p["lnx_b"],
      p["w2"], p["a2"], p["g2"], s0)


def _mlstm_kernel(zq_ref, zk_ref, zv_ref, zo_ref, gate_ref, bias_ref, nw_ref, c0_ref, n0_ref, m0_ref,
                  y_ref, co_ref, no_ref, mo_ref,
                  c_scr, n_scr, m_scr, o_scr, *, nb, hm, tb_rows, chunk):
    tb = pl.program_id(2)
    n_tb = pl.num_programs(2)
    per_seq = tb_rows // chunk

    @pl.when(tb == 0)
    def _():
        c_scr[...] = c0_ref[0]
        n_scr[...] = n0_ref[0]
        m_scr[...] = m0_ref[0]

    incl, _, eye = _tri_masks(chunk)
    upper = lax.broadcasted_iota(i32, (chunk, chunk), 0) <= lax.broadcasted_iota(i32, (chunk, chunk), 1)
    pairs = [(s, h) for s in range(nb) for h in range(hm)]
    each = lambda f, *ls: [f(*xs) for xs in zip(*ls)]
    square = lambda row: jnp.broadcast_to(row, (chunk, chunk))
    row_sum = lambda x: jnp.sum(x, axis=-1, keepdims=True)

    def body(ci, carry):
        def load(ref):
            out = []
            for s, h in pairs:
                start = s * tb_rows + ci * chunk
                start = start if isinstance(start, int) else pl.multiple_of(start, chunk)
                out.append(ref[pl.ds(start, chunk), h * DK:(h + 1) * DK])
            return out

        q, v, o_gate = load(zq_ref), load(zv_ref), load(zo_ref)
        k = [x * (DK ** -0.5) for x in load(zk_ref)]
        gch = [gate_ref[h, s * per_seq + ci] for s, h in pairs]
        ic_row = [g[0:1, :] + bias_ref[h, 0:1, 0:1] for g, (s, h) in zip(gch, pairs)]
        fc_row = [-_softplus(-(g[1:2, :] + bias_ref[h, 1:2, 0:1])) for g, (s, h) in zip(gch, pairs)]
        c_mat = [c_scr[s, h] for s, h in pairs]
        n_row = [n_scr[s, h] for s, h in pairs]
        m_prev = [m_scr[s, h][:, 0:1] for s, h in pairs]
        f_col = each(lambda fr: row_sum(jnp.where(incl, square(fr), 0.0)), fc_row)
        fc_col = each(lambda fr: row_sum(jnp.where(eye, square(fr), 0.0)), fc_row)
        ic_col = each(lambda ir: row_sum(jnp.where(eye, square(ir), 0.0)), ic_row)
        f_row = each(lambda fc: jnp.sum(jnp.where(upper, square(fc), 0.0), axis=0, keepdims=True), fc_col)
        dm = each(lambda fc, fr, ir: jnp.where(incl, fc - fr + ir, -jnp.inf), f_col, f_row, ic_row)
        inter = each(lambda fc, m: fc + m, f_col, m_prev)
        m_t = each(lambda it_, d: jnp.maximum(it_, jnp.max(d, axis=-1, keepdims=True)), inter, dm)
        w_intra = each(lambda d, m: jnp.exp(d - m), dm, m_t)
        w_inter = each(lambda it_, m: jnp.exp(it_ - m), inter, m_t)
        s_mat = each(lambda q_, k_, w: _bdot(q_, k_, NT) * w, q, k, w_intra)
        qc = each(_bdot, q, c_mat)
        num = each(lambda s_, v_, w, x: _bdot(s_, v_) + w * x, s_mat, v, w_inter, qc)
        den = each(lambda s_, w, q_, n_: row_sum(s_) + w * row_sum(q_ * n_), s_mat, w_inter, q, n_row)
        hid = each(lambda a, d, m: a / jnp.maximum(jnp.abs(d), jnp.exp(-m)), num, den, m_t)
        m_new = each(lambda m: m[chunk - 1:chunk, :], m_t)
        w_s = each(lambda fc, ic, m: jnp.exp(fc[chunk - 1:chunk, :] - fc + ic - m), f_col, ic_col, m_new)
        w_c = each(lambda it_, m: jnp.exp(it_[chunk - 1:chunk, :] - m), inter, m_new)
        kw = each(lambda k_, w: k_ * w, k, w_s)
        c_new = each(lambda w, c, kw_, v_: w * c + _bdot(kw_, v_, TN), w_c, c_mat, kw, v)
        n_new = each(lambda w, n_, kw_: w * n_ + jnp.sum(kw_, axis=0, keepdims=True), w_c, n_row, kw)
        mean = each(lambda x: jnp.mean(x, axis=-1, keepdims=True), hid)
        var = each(lambda x, mu: jnp.mean(jnp.square(x - mu), axis=-1, keepdims=True), hid, mean)
        for i, (s, h) in enumerate(pairs):
            c_scr[s, h] = c_new[i]
            n_scr[s, h] = n_new[i]
            m_scr[s, h] = jnp.broadcast_to(m_new[i], (1, 128))
            hn = (hid[i] - mean[i]) * lax.rsqrt(var[i] + MH_EPS) * nw_ref[:, h * DK:(h + 1) * DK]
            start = s * tb_rows + ci * chunk
            start = start if isinstance(start, int) else pl.multiple_of(start, chunk)
            o_scr[pl.ds(start, chunk), h * DK:(h + 1) * DK] = hn * jax.nn.sigmoid(o_gate[i])
        return carry

    if per_seq == 1:
        body(0, 0)
    else:
        lax.fori_loop(0, per_seq, body, 0)
    y_ref[...] = o_scr[...].astype(y_ref.dtype)

    @pl.when(tb == n_tb - 1)
    def _():
        co_ref[...] = c_scr[...]
        no_ref[...] = n_scr[...]
        mo_ref[...] = m_scr[...]


def _mlstm_call(z, gates, bias, norm_w, c0, n0, m0, layer, *, nseq, seq_len, row0, nb, hm, tb_rows, chunk):
    rows = nb * tb_rows
    n_tb = seq_len // tb_rows
    rb0 = row0 // rows
    cpb = rows // chunk
    wm = hm * DK
    zspec = lambda col0: pl.BlockSpec((rows, wm), lambda sb, h, tb: (rb0 + sb * n_tb + tb, col0 // wm + h))
    st4 = lambda a, b: pl.BlockSpec((nb, hm, a, b), lambda sb, h, tb: (sb, h, 0, 0))
    st5 = lambda a, b: pl.BlockSpec((1, nb, hm, a, b), lambda sb, h, tb: (layer, sb, h, 0, 0))
    kern = functools.partial(_mlstm_kernel, nb=nb, hm=hm, tb_rows=tb_rows, chunk=chunk)
    return pl.pallas_call(
        kern,
        grid=(nseq // nb, NH_M // hm, n_tb),
        in_specs=[
            zspec(Z_MQ), zspec(Z_MK), zspec(Z_MV), zspec(Z_MO),
            pl.BlockSpec((hm, cpb, 2, chunk), lambda sb, h, tb: (h, sb * n_tb + tb, 0, 0)),
            pl.BlockSpec((hm, 2, 128), lambda sb, h, tb: (h, 0, 0)),
            pl.BlockSpec((1, wm), lambda sb, h, tb: (0, h)),
            st5(DK, DK), st5(1, DK), st5(1, 128),
        ],
        out_specs=[
            pl.BlockSpec((rows, wm), lambda sb, h, tb: (sb * n_tb + tb, h)),
            st4(DK, DK), st4(1, DK), st4(1, 128),
        ],
        out_shape=[
            jax.ShapeDtypeStruct((nseq * seq_len, D_MLSTM), bf16),
            jax.ShapeDtypeStruct((nseq, NH_M, DK, DK), f32),
            jax.ShapeDtypeStruct((nseq, NH_M, 1, DK), f32),
            jax.ShapeDtypeStruct((nseq, NH_M, 1, 128), f32),
        ],
        scratch_shapes=[
            pltpu.VMEM((nb, hm, DK, DK), f32),
            pltpu.VMEM((nb, hm, 1, DK), f32),
            pltpu.VMEM((nb, hm, 1, 128), f32),
            pltpu.VMEM((rows, wm), f32),
        ],
        compiler_params=_cparams(("parallel", "parallel", "arbitrary")),
        name="mlstm",
    )(z, z, z, z, gates, bias, norm_w, c0, n0, m0)


def _norm_route_kernel(x_ref, w_ref, scp_ref, scs_ref, shp_ref, shs_ref, rw_ref, rb_ref,
                       hn_ref, ei_ref, ew_ref, *, n_prompt_tiles):
    is_prompt = pl.program_id(0) < n_prompt_tiles
    y = _rms(x_ref[...], w_ref[...]).reshape(SEQ_PER_TILE, 8, D)
    sc = _pick_mod(is_prompt, scp_ref, scs_ref)
    sh = _pick_mod(is_prompt, shp_ref, shs_ref)
    hn = (y * (1.0 + sc) + sh).reshape(ROW_TILE, D)
    hn_ref[...] = hn
    logits = _bdot(hn, rw_ref[...]) + rb_ref[...]
    lane = lax.broadcasted_iota(i32, logits.shape, 1)
    lane_f = lane.astype(f32)
    valid = lane < N_EXPERTS
    lg = jnp.where(valid, logits, -1e30)
    ex = jnp.where(valid, jnp.exp(lg - jnp.max(lg, axis=-1, keepdims=True)), 0.0)
    probs = ex / jnp.sum(ex, axis=-1, keepdims=True)
    best = None
    for g in range(N_GROUPS):
        in_g = (lane >= g * EXPERTS_PER_GROUP) & (lane < (g + 1) * EXPERTS_PER_GROUP)
        pg = jnp.where(in_g, probs, -1.0)
        m1 = jnp.max(pg, axis=-1, keepdims=True)
        i1 = jnp.min(jnp.where(pg == m1, lane_f, 1e4), axis=-1, keepdims=True)
        pg2 = jnp.where(lane_f == i1, -1.0, pg)
        m2 = jnp.max(pg2, axis=-1, keepdims=True)
        i2 = jnp.min(jnp.where(pg2 == m2, lane_f, 1e4), axis=-1, keepdims=True)
        cand = (m1 + m2, m1, m2, i1, i2)
        if best is None:
            best = cand
        else:
            upd = cand[0] > best[0]
            best = tuple(jnp.where(upd, c, b) for c, b in zip(cand, best))
    _, m1, m2, i1, i2 = best
    tot = m1 + m2
    ei_ref[...] = jnp.where(lane == 0, i1, jnp.where(lane == 1, i2, 0.0)).astype(i32)
    ew_ref[...] = jnp.where(lane == 0, m1 / tot, jnp.where(lane == 1, m2 / tot, 0.0))


def _norm_route_call(x, w, sc_p, sc_s, sh_p, sh_s, rw, rb, prompt_len):
    m = x.shape[0]
    n_prompt_seq = sc_p.shape[0]
    npt = n_prompt_seq * prompt_len // ROW_TILE
    pspec, sspec = _mod_specs(npt, n_prompt_seq, prompt_len, D)
    row = lambda wd: pl.BlockSpec((ROW_TILE, wd), lambda i: (i, 0))
    return pl.pallas_call(
        functools.partial(_norm_route_kernel, n_prompt_tiles=npt),
        grid=(m // ROW_TILE,),
        in_specs=[row(D), pl.BlockSpec((1, D), lambda i: (0, 0)), pspec, sspec, pspec, sspec,
                  pl.BlockSpec((D, 128), lambda i: (0, 0)), pl.BlockSpec((1, 128), lambda i: (0, 0))],
        out_specs=[row(D), row(128), row(128)],
        out_shape=[jax.ShapeDtypeStruct((m, D), f32), jax.ShapeDtypeStruct((m, 128), i32),
                   jax.ShapeDtypeStruct((m, 128), f32)],
        compiler_params=_cparams(("parallel",)),
        name="norm2_route",
    )(x, w.reshape(1, D), sc_p, sc_s, sh_p, sh_s, rw, rb)


def _row_copy(src_hbm, row, dst, r, sem):
    return pltpu.make_async_copy(src_hbm.at[pl.ds(row, 1), :], dst.at[pl.ds(r, 1), :], sem)


def _gather_kernel(tok_ref, hn_hbm, o_ref, buf, sem):
    base = pl.program_id(0) * MOE_TM

    def issue(r, c):
        _row_copy(hn_hbm, tok_ref[base + r], buf, r, sem).start()
        return c

    def wait(r, c):
        _row_copy(hn_hbm, 0, buf, r, sem).wait()
        return c

    lax.fori_loop(0, MOE_TM, issue, 0)
    lax.fori_loop(0, MOE_TM, wait, 0)
    o_ref[...] = buf[...].astype(o_ref.dtype)


def _gather_call(hn, slot_tok):
    p = slot_tok.shape[0]
    return pl.pallas_call(
        _gather_kernel,
        grid_spec=pltpu.PrefetchScalarGridSpec(
            num_scalar_prefetch=1,
            grid=(p // MOE_TM,),
            in_specs=[pl.BlockSpec(memory_space=pl.ANY)],
            out_specs=pl.BlockSpec((MOE_TM, D), lambda j, tok: (j, 0)),
            scratch_shapes=[pltpu.VMEM((MOE_TM, D), f32), pltpu.SemaphoreType.DMA(())],
        ),
        out_shape=jax.ShapeDtypeStruct((p, D), bf16),
        compiler_params=_cparams(("arbitrary",)),
        name="moe_gather",
    )(slot_tok, hn)


def _moe1_kernel(be_ref, first_ref, valid_ref, x_ref, wg_ref, wu_ref, act_ref, wg_bf, wu_bf):
    j = pl.program_id(1)

    @pl.when(first_ref[j] == 1)
    def _():
        wg_bf[...] = wg_ref[0, 0].astype(bf16)
        wu_bf[...] = wu_ref[0, 0].astype(bf16)

    @pl.when(valid_ref[j] == 1)
    def _():
        x = x_ref[...]
        g = jnp.dot(x, wg_bf[...], preferred_element_type=f32)
        u = jnp.dot(x, wu_bf[...], preferred_element_type=f32)
        act_ref[...] = (g * jax.nn.sigmoid(g) * u).astype(act_ref.dtype)

    @pl.when(valid_ref[j] == 0)
    def _():
        act_ref[...] = jnp.zeros_like(act_ref)


def _moe1_call(xs, w_in, layer, block_e, first, valid):
    p = xs.shape[0]
    tn = 256
    n_t = D_EXPERT // tn
    return pl.pallas_call(
        _moe1_kernel,
        grid_spec=pltpu.PrefetchScalarGridSpec(
            num_scalar_prefetch=3,
            grid=(n_t, p // MOE_TM),
            in_specs=[
                pl.BlockSpec((MOE_TM, D), lambda n, j, be, fi, va: (j, 0)),
                pl.BlockSpec((1, 1, D, tn), lambda n, j, be, fi, va: (layer, be[j], 0, n)),
                pl.BlockSpec((1, 1, D, tn), lambda n, j, be, fi, va: (layer, be[j], 0, n_t + n)),
            ],
            out_specs=pl.BlockSpec((MOE_TM, tn), lambda n, j, be, fi, va: (j, n)),
            scratch_shapes=[pltpu.VMEM((D, tn), bf16), pltpu.VMEM((D, tn), bf16)],
        ),
        out_shape=jax.ShapeDtypeStruct((p, D_EXPERT), bf16),
        compiler_params=_cparams(("arbitrary", "arbitrary")),
        name="moe_up",
    )(block_e, first, valid, xs, w_in, w_in)


def _moe2_kernel(be_ref, first_ref, valid_ref, a_ref, w_ref, o_ref, w_bf):
    j = pl.program_id(1)

    @pl.when(first_ref[j] == 1)
    def _():
        w_bf[...] = w_ref[0, 0].astype(bf16)

    @pl.when(valid_ref[j] == 1)
    def _():
        o_ref[...] = jnp.dot(a_ref[...], w_bf[...], preferred_element_type=f32)

    @pl.when(valid_ref[j] == 0)
    def _():
        o_ref[...] = jnp.zeros_like(o_ref)


def _moe2_call(act, w_out, layer, block_e, first, valid):
    p = act.shape[0]
    tn = 1024
    return pl.pallas_call(
        _moe2_kernel,
        grid_spec=pltpu.PrefetchScalarGridSpec(
            num_scalar_prefetch=3,
            grid=(D // tn, p // MOE_TM),
            in_specs=[
                pl.BlockSpec((MOE_TM, D_EXPERT), lambda n, j, be, fi, va: (j, 0)),
                pl.BlockSpec((1, 1, D_EXPERT, tn), lambda n, j, be, fi, va: (layer, be[j], 0, n)),
            ],
            out_specs=pl.BlockSpec((MOE_TM, tn), lambda n, j, be, fi, va: (j, n)),
            scratch_shapes=[pltpu.VMEM((D_EXPERT, tn), bf16)],
        ),
        out_shape=jax.ShapeDtypeStruct((p, D), f32),
        compiler_params=_cparams(("arbitrary", "arbitrary")),
        name="moe_down",
    )(block_e, first, valid, act, w_out)


def _combine_kernel(pos_ref, x_ref, ew_ref, gp_ref, gs_ref, yb_hbm, o_ref, buf0, buf1, sem, *, n_prompt_tiles):
    i = pl.program_id(0)
    base = i * ROW_TILE

    def issue(r, c):
        _row_copy(yb_hbm, pos_ref[2 * (base + r)], buf0, r, sem).start()
        _row_copy(yb_hbm, pos_ref[2 * (base + r) + 1], buf1, r, sem).start()
        return c

    def wait(r, c):
        _row_copy(yb_hbm, 0, buf0, r, sem).wait()
        _row_copy(yb_hbm, 0, buf1, r, sem).wait()
        return c

    lax.fori_loop(0, ROW_TILE, issue, 0)
    lax.fori_loop(0, ROW_TILE, wait, 0)
    ew = ew_ref[...]
    y = buf0[...] * ew[:, 0:1] + buf1[...] * ew[:, 1:2]
    g = _pick_mod(i < n_prompt_tiles, gp_ref, gs_ref)
    o_ref[...] = (x_ref[...].reshape(SEQ_PER_TILE, 8, D) + g * y.reshape(SEQ_PER_TILE, 8, D)).reshape(ROW_TILE, D)


def _combine_call(x, yb, pos, ew, g_p, g_s, prompt_len):
    m = x.shape[0]
    n_prompt_seq = g_p.shape[0]
    npt = n_prompt_seq * prompt_len // ROW_TILE
    tiles_per_seq = prompt_len // ROW_TILE
    return pl.pallas_call(
        functools.partial(_combine_kernel, n_prompt_tiles=npt),
        grid_spec=pltpu.PrefetchScalarGridSpec(
            num_scalar_prefetch=1,
            grid=(m // ROW_TILE,),
            in_specs=[
                pl.BlockSpec((ROW_TILE, D), lambda i, pos: (i, 0)),
                pl.BlockSpec((ROW_TILE, 128), lambda i, pos: (i, 0)),
                pl.BlockSpec((1, 1, D), lambda i, pos: (jnp.minimum(i // tiles_per_seq, n_prompt_seq - 1), 0, 0)),
                pl.BlockSpec((SEQ_PER_TILE, 1, D), lambda i, pos: (jnp.maximum(i - npt, 0), 0, 0)),
                pl.BlockSpec(memory_space=pl.ANY),
            ],
            out_specs=pl.BlockSpec((ROW_TILE, D), lambda i, pos: (i, 0)),
            scratch_shapes=[pltpu.VMEM((ROW_TILE, D), f32), pltpu.VMEM((ROW_TILE, D), f32),
                            pltpu.SemaphoreType.DMA(())],
        ),
        out_shape=jax.ShapeDtypeStruct((m, D), f32),
        compiler_params=_cparams(("arbitrary",)),
        name="moe_combine",
    )(pos, x, ew, g_p, g_s, yb)


def _moe_plan(eidx, n_blocks):
    m = eidx.shape[0]
    a = 2 * m
    flat_e = eidx.reshape(a)
    order = jnp.argsort(flat_e).astype(i32)
    sorted_e = flat_e[order]
    counts = jnp.zeros((N_EXPERTS,), i32).at[flat_e].add(1)
    padded = (counts + MOE_TM - 1) // MOE_TM * MOE_TM
    start = jnp.cumsum(counts) - counts
    pad_end = jnp.cumsum(padded)
    pad_start = pad_end - padded
    dest = pad_start[sorted_e] + jnp.arange(a, dtype=i32) - start[sorted_e]
    slot_tok = jnp.zeros((n_blocks * MOE_TM,), i32).at[dest].set(order // 2)
    pos = jnp.zeros((a,), i32).at[order].set(dest)
    blk = jnp.arange(n_blocks, dtype=i32) * MOE_TM
    block_e = jnp.minimum(jnp.searchsorted(pad_end, blk, side="right"), N_EXPERTS - 1).astype(i32)
    valid = (blk < pad_end[-1]).astype(i32)
    first = jnp.concatenate([jnp.ones((1,), i32), (block_e[1:] != block_e[:-1]).astype(i32)])
    return slot_tok, pos, block_e, first, valid


def _prep_w_in(w):
    zeros = lambda n: jnp.zeros((w.shape[0], n), w.dtype)
    mb = RWKV_COLS
    gb = RWKV_COLS + MLSTM_COLS
    parts = [
        w[:, 0:2048], w[:, 2144:4192], w[:, 4192:6240],
        w[:, mb:mb + 2048], w[:, mb + 2048:mb + 4096], w[:, mb + 4096:mb + 6144], w[:, mb + 6160:mb + 8208],
        w[:, gb:gb + 8192],
        w[:, 6336:6592], w[:, 2048:2144], zeros(32), w[:, 6240:6336], zeros(32),
        w[:, mb + 6144:mb + 6160], zeros(112), zeros(NZ - Z_IF - 128),
    ]
    return jnp.concatenate(parts, axis=1).astype(bf16)


def _prep_mu(mu):
    zeros = jnp.zeros((32,), mu.dtype)
    return jnp.concatenate([mu[0:2048], mu[2144:4192], mu[4192:6240], mu[6336:6592], mu[2048:2144], zeros,
                            mu[6240:6336], zeros]).reshape(1, ZP_COLS)


def _pad_rows(w, rows):
    return jnp.concatenate([w, jnp.zeros((rows - w.shape[0], w.shape[1]), w.dtype)], axis=0).astype(bf16)


def _gate_rows(z, row0, rows, chunk):
    zif = z[row0:row0 + rows, Z_IF:Z_IF + 2 * NH_M].reshape(rows // chunk, chunk, 2, NH_M)
    return jnp.transpose(zif, (3, 0, 2, 1))


def kernel(x_prompt, x_sample, c_prompt, c_sample, state_wkv, state_shift, state_C, state_n, state_m, ada_w, ada_b, norm1_w, norm2_w, w_in, rwkv_mu, rwkv_w0, rwkv_w2, rwkv_a0, rwkv_a2, rwkv_g2, rwkv_k_k, rwkv_k_a, rwkv_r_k, rwkv_lnx_w, rwkv_lnx_b, mlstm_i_b, mlstm_f_b, mlstm_norm_w, w_branch_r, w_branch_m, w_out, router_w, router_b, moe_w_in, moe_w_out, final_norm_w):
    bp, tp, _ = x_prompt.shape
    bs, ts, _ = x_sample.shape
    mp, ms = bp * tp, bs * ts
    m = mp + ms
    assert ts == 8 and tp % ROW_TILE == 0 and ms % ROW_TILE == 0 and bs % SEQ_PER_TILE == 0 and m % 512 == 0
    n_seq = bp + bs
    n_seq_pad = -(-n_seq // 8) * 8

    x = jnp.concatenate([x_prompt.reshape(mp, D), x_sample.reshape(ms, D)], axis=0)
    c_all = jnp.concatenate([c_prompt, c_sample, jnp.zeros((n_seq_pad - n_seq, D), f32)], axis=0)
    ada = _ada_call(c_all, ada_w, ada_b).reshape(DEPTH, n_seq_pad, N_ADA, D)
    last_rows = jnp.concatenate([jnp.arange(bp, dtype=i32) * tp + tp - 1,
                                 mp + jnp.arange(bs, dtype=i32) * ts + ts - 1,
                                 jnp.zeros((n_seq_pad - n_seq,), i32)])
    router_w_p = jnp.concatenate([router_w, jnp.zeros((D, 128 - N_EXPERTS), f32)], axis=1).astype(bf16)
    router_b_p = jnp.concatenate([router_b, jnp.zeros((128 - N_EXPERTS,), f32)]).reshape(1, 128)
    n_blocks = -(-(2 * m + N_EXPERTS * (MOE_TM - 1)) // MOE_TM)

    prompt_geo = dict(nseq=bp, seq_len=tp, row0=0, nb=1, tb_rows=256, chunk=64)
    sample_geo = dict(nseq=bs, seq_len=ts, row0=mp, tb_rows=8, chunk=8)
    rwkv_p, rwkv_s = dict(prompt_geo, local=2), dict(sample_geo, nb=16, local=4)
    mlstm_p, mlstm_s = dict(prompt_geo, hm=4), dict(sample_geo, nb=4, hm=4)

    outs_p = [[] for _ in range(5)]
    outs_s = [[] for _ in range(5)]
    for l in range(DEPTH):
        mod = lambda idx: (ada[l, :bp, idx][:, None, :], ada[l, bp:n_seq, idx][:, None, :])
        sh1, sc1, g1, sh2, sc2, g2 = (mod(i) for i in range(N_ADA))
        wz = _prep_w_in(w_in[l])

        xn = _norm_mod_call(x, norm1_w[l], sc1[0], sc1[1], sh1[0], sh1[1], tp)
        shift_rows = _norm_rows_call(x[last_rows], norm1_w[l], ada[l, :, 1], ada[l, :, 0])
        z = _matmul_call(xn, wz, 512, 1024, name="w_in")
        zprev_s = _matmul_call(state_shift[l].astype(bf16), wz, bs, 512, n_out=ZP_COLS,
                               col_map=lambda j: jnp.where(j < 12, j, Z_SM // 512), name="w_in_prev")
        rp = dict(mu=_prep_mu(rwkv_mu[l]), w0=rwkv_w0[l].reshape(1, -1), a0=rwkv_a0[l].reshape(1, -1),
                  k_k=rwkv_k_k[l].reshape(1, -1), k_a=rwkv_k_a[l].reshape(1, -1), r_k=rwkv_r_k[l].reshape(1, -1),
                  lnx_w=rwkv_lnx_w[l].reshape(1, -1), lnx_b=rwkv_lnx_b[l].reshape(1, -1),
                  w2=_pad_rows(rwkv_w2[l], 128), a2=_pad_rows(rwkv_a2[l], 128), g2=rwkv_g2[l].astype(bf16))
        yr_p, wkv_p = _rwkv_call(z, jnp.zeros((bp * 8, ZP_COLS), f32),
                                 jnp.zeros((1, bp, NH_R, HEAD_R, HEAD_R), f32), 0, rp, **rwkv_p)
        yr_s, wkv_s = _rwkv_call(z, jnp.repeat(zprev_s, 8, axis=0), state_wkv, l, rp, **rwkv_s)
        bias = jnp.broadcast_to(jnp.stack([mlstm_i_b[l], mlstm_f_b[l]], axis=1)[:, :, None], (NH_M, 2, 128))
        nw = mlstm_norm_w[l].reshape(1, -1)
        ym_p, c_p, n_p, m_p = _mlstm_call(
            z, _gate_rows(z, 0, mp, 64), bias, nw, jnp.zeros((1, bp, NH_M, DK, DK), f32),
            jnp.zeros((1, bp, NH_M, 1, DK), f32), jnp.zeros((1, bp, NH_M, 1, 128), f32), 0, **mlstm_p)
        ym_s, c_s, n_s, m_s = _mlstm_call(
            z, _gate_rows(z, mp, ms, 8), bias, nw, state_C, state_n[:, :, :, None, :],
            jnp.broadcast_to(state_m[:, :, :, None, None], (DEPTH, bs, NH_M, 1, 128)), l, **mlstm_s)
        merged = _merge_call(jnp.concatenate([yr_p, yr_s], axis=0), jnp.concatenate([ym_p, ym_s], axis=0),
                             w_branch_r[l].astype(bf16), w_branch_m[l].astype(bf16), z)
        x = _wout_call(merged, w_out[l].astype(bf16), x, g1[0], g1[1], tp)

        hn, eidx, ew = _norm_route_call(x, norm2_w[l], sc2[0], sc2[1], sh2[0], sh2[1], router_w_p, router_b_p, tp)
        slot_tok, pos, block_e, first, valid = _moe_plan(eidx[:, :2], n_blocks)
        xs = _gather_call(hn, slot_tok)
        act = _moe1_call(xs, moe_w_in, l, block_e, first, valid)
        yb = _moe2_call(act, moe_w_out, l, block_e, first, valid)
        x = _combine_call(x, yb, pos, ew, g2[0], g2[1], tp)

        for acc, vals in ((outs_p, (wkv_p, shift_rows[:bp], c_p, n_p[:, :, 0], m_p[:, :, 0, 0])),
                          (outs_s, (wkv_s, shift_rows[bp:n_seq], c_s, n_s[:, :, 0], m_s[:, :, 0, 0]))):
            for lst, v in zip(acc, vals):
                lst.append(v)

    y_prompt = _final_norm_call(x, final_norm_w, 0, mp).reshape(bp, tp, D)
    y_sample = _final_norm_call(x, final_norm_w, mp, ms).reshape(bs, ts, D)
    return (y_prompt, y_sample, *(jnp.stack(v) for v in outs_p), *(jnp.stack(v) for v in outs_s))
```

```python
import functools
import math

import jax
import jax.numpy as jnp
from jax import lax
from jax.experimental import pallas as pl
from jax.experimental.pallas import tpu as pltpu

f32 = jnp.float32
bf16 = jnp.bfloat16
i32 = jnp.int32

D = 4096
DEPTH = 2
D_RWKV = 2048
HEAD_R = 64
NH_R = D_RWKV // HEAD_R
W_LORA = 96
A_LORA = 96
G_LORA = 256
LNX_EPS = 64e-5
D_MLSTM = 2048
NH_M = 8
DK = D_MLSTM // NH_M
MH_EPS = 1e-6
RWKV_COLS = 3 * D_RWKV + W_LORA + A_LORA + G_LORA
MLSTM_COLS = 4 * D_MLSTM + 2 * NH_M
N_EXPERTS = 32
EXPERTS_PER_GROUP = 8
N_GROUPS = 4
D_EXPERT = D // 4
N_ADA = 6
RMS_EPS = 1e-6

Z_RR, Z_RK, Z_RV = 0, 2048, 4096
Z_MQ, Z_MK, Z_MV, Z_MO = 6144, 8192, 10240, 12288
Z_GR, Z_GM = 14336, 18432
Z_SM = 22528
Z_IF = 23040
NZ = 23552
SM_W = 512
ZP_COLS = 3 * D_RWKV + SM_W

ROW_TILE = 256
SEQ_PER_TILE = 32
MOE_TM = 256
VMEM_LIMIT = 48 * 1024 * 1024

NN = (((1,), (0,)), ((), ()))
NT = (((1,), (1,)), ((), ()))
TN = (((0,), (0,)), ((), ()))


def _cparams(sem, vmem=VMEM_LIMIT):
    return pltpu.CompilerParams(dimension_semantics=sem, vmem_limit_bytes=vmem)


def _bdot(a, b, dims=NN):
    return lax.dot_general(a.astype(bf16), b.astype(bf16), dims, preferred_element_type=f32)


def _softplus(x):
    return jnp.maximum(x, 0.0) + jnp.log1p(jnp.exp(-jnp.abs(x)))


def _ada_kernel(c_ref, w_ref, b_ref, o_ref):
    c = c_ref[...]
    a = c * jax.nn.sigmoid(c)
    o_ref[0] = _bdot(a, w_ref[0]) + b_ref[0]


def _ada_call(c_all, ada_w, ada_b):
    nb = c_all.shape[0]
    n = ada_w.shape[2]
    tn = 512
    return pl.pallas_call(
        _ada_kernel,
        grid=(DEPTH, n // tn),
        in_specs=[
            pl.BlockSpec((nb, D), lambda l, j: (0, 0)),
            pl.BlockSpec((1, D, tn), lambda l, j: (l, 0, j)),
            pl.BlockSpec((1, 1, tn), lambda l, j: (l, 0, j)),
        ],
        out_specs=pl.BlockSpec((1, nb, tn), lambda l, j: (l, 0, j)),
        out_shape=jax.ShapeDtypeStruct((DEPTH, nb, n), f32),
        compiler_params=_cparams(("parallel", "parallel")),
        name="ada",
    )(c_all, ada_w, ada_b.reshape(DEPTH, 1, n))


def _mod_specs(n_prompt_tiles, n_prompt_seq, prompt_len, width, col=None):
    tiles_per_seq = prompt_len // ROW_TILE
    if col is None:
        pmap = lambda i: (jnp.minimum(i // tiles_per_seq, n_prompt_seq - 1), 0, 0)
        smap = lambda i: (jnp.maximum(i - n_prompt_tiles, 0), 0, 0)
    else:
        pmap = lambda j, i: (jnp.minimum(i // tiles_per_seq, n_prompt_seq - 1), 0, j)
        smap = lambda j, i: (jnp.maximum(i - n_prompt_tiles, 0), 0, j)
    return pl.BlockSpec((1, 1, width), pmap), pl.BlockSpec((SEQ_PER_TILE, 1, width), smap)


def _pick_mod(is_prompt, p_ref, s_ref):
    return jnp.where(is_prompt, jnp.broadcast_to(p_ref[...], s_ref.shape), s_ref[...])


def _rms(x, w):
    return x * lax.rsqrt(jnp.mean(x * x, axis=-1, keepdims=True) + RMS_EPS) * w


def _norm_mod_kernel(x_ref, w_ref, scp_ref, scs_ref, shp_ref, shs_ref, o_ref, *, n_prompt_tiles):
    is_prompt = pl.program_id(0) < n_prompt_tiles
    y = _rms(x_ref[...], w_ref[...]).reshape(SEQ_PER_TILE, 8, D)
    sc = _pick_mod(is_prompt, scp_ref, scs_ref)
    sh = _pick_mod(is_prompt, shp_ref, shs_ref)
    o_ref[...] = (y * (1.0 + sc) + sh).reshape(ROW_TILE, D).astype(o_ref.dtype)


def _norm_mod_call(x, w, sc_p, sc_s, sh_p, sh_s, prompt_len):
    m = x.shape[0]
    n_prompt_seq = sc_p.shape[0]
    npt = n_prompt_seq * prompt_len // ROW_TILE
    pspec, sspec = _mod_specs(npt, n_prompt_seq, prompt_len, D)
    return pl.pallas_call(
        functools.partial(_norm_mod_kernel, n_prompt_tiles=npt),
        grid=(m // ROW_TILE,),
        in_specs=[pl.BlockSpec((ROW_TILE, D), lambda i: (i, 0)), pl.BlockSpec((1, D), lambda i: (0, 0)),
                  pspec, sspec, pspec, sspec],
        out_specs=pl.BlockSpec((ROW_TILE, D), lambda i: (i, 0)),
        out_shape=jax.ShapeDtypeStruct((m, D), bf16),
        compiler_params=_cparams(("parallel",)),
        name="norm1",
    )(x, w.reshape(1, D), sc_p, sc_s, sh_p, sh_s)


def _norm_rows_kernel(x_ref, w_ref, sc_ref, sh_ref, o_ref):
    o_ref[...] = _rms(x_ref[...], w_ref[...]) * (1.0 + sc_ref[...]) + sh_ref[...]


def _norm_rows_call(x_rows, w, sc_rows, sh_rows):
    return pl.pallas_call(
        _norm_rows_kernel,
        out_shape=jax.ShapeDtypeStruct(x_rows.shape, f32),
        name="norm_rows",
    )(x_rows, w.reshape(1, D), sc_rows, sh_rows)


def _final_norm_kernel(x_ref, w_ref, o_ref):
    o_ref[...] = _rms(x_ref[...], w_ref[...])


def _final_norm_call(x, w, row0, rows):
    off = row0 // ROW_TILE
    return pl.pallas_call(
        _final_norm_kernel,
        grid=(rows // ROW_TILE,),
        in_specs=[pl.BlockSpec((ROW_TILE, D), lambda i: (i + off, 0)), pl.BlockSpec((1, D), lambda i: (0, 0))],
        out_specs=pl.BlockSpec((ROW_TILE, D), lambda i: (i, 0)),
        out_shape=jax.ShapeDtypeStruct((rows, D), f32),
        compiler_params=_cparams(("parallel",)),
        name="final_norm",
    )(x, w.reshape(1, D))


def _mm_kernel(a_ref, w_ref, o_ref):
    o_ref[...] = jnp.dot(a_ref[...], w_ref[...], preferred_element_type=f32)


def _matmul_call(a, w, tm, tn, n_out=None, col_map=None, name="matmul"):
    m, k = a.shape
    n_out = w.shape[1] if n_out is None else n_out
    wmap = (lambda j, i: (0, j)) if col_map is None else (lambda j, i: (0, col_map(j)))
    return pl.pallas_call(
        _mm_kernel,
        grid=(n_out // tn, m // tm),
        in_specs=[pl.BlockSpec((tm, k), lambda j, i: (i, 0)), pl.BlockSpec((k, tn), wmap)],
        out_specs=pl.BlockSpec((tm, tn), lambda j, i: (i, j)),
        out_shape=jax.ShapeDtypeStruct((m, n_out), f32),
        compiler_params=_cparams(("parallel", "parallel")),
        name=name,
    )(a, w)


def _merge_kernel(yr_ref, ym_ref, wr_ref, wm_ref, gr_ref, gm_ref, o_ref):
    pr = jnp.dot(yr_ref[...], wr_ref[...], preferred_element_type=f32)
    pm = jnp.dot(ym_ref[...], wm_ref[...], preferred_element_type=f32)
    o_ref[...] = (jax.nn.sigmoid(gr_ref[...]) * pr + jax.nn.sigmoid(gm_ref[...]) * pm).astype(o_ref.dtype)


def _merge_call(yr, ym, wr, wm, z):
    m = yr.shape[0]
    tm, tn = 512, 1024
    return pl.pallas_call(
        _merge_kernel,
        grid=(D // tn, m // tm),
        in_specs=[
            pl.BlockSpec((tm, D_RWKV), lambda j, i: (i, 0)),
            pl.BlockSpec((tm, D_MLSTM), lambda j, i: (i, 0)),
            pl.BlockSpec((D_RWKV, tn), lambda j, i: (0, j)),
            pl.BlockSpec((D_MLSTM, tn), lambda j, i: (0, j)),
            pl.BlockSpec((tm, tn), lambda j, i: (i, Z_GR // tn + j)),
            pl.BlockSpec((tm, tn), lambda j, i: (i, Z_GM // tn + j)),
        ],
        out_specs=pl.BlockSpec((tm, tn), lambda j, i: (i, j)),
        out_shape=jax.ShapeDtypeStruct((m, D), bf16),
        compiler_params=_cparams(("parallel", "parallel")),
        name="merge",
    )(yr, ym, wr, wm, z, z)


def _wout_kernel(a_ref, w_ref, x_ref, gp_ref, gs_ref, o_ref, *, n_prompt_tiles):
    is_prompt = pl.program_id(1) < n_prompt_tiles
    tn = o_ref.shape[1]
    acc = jnp.dot(a_ref[...], w_ref[...], preferred_element_type=f32).reshape(SEQ_PER_TILE, 8, tn)
    g = _pick_mod(is_prompt, gp_ref, gs_ref)
    o_ref[...] = (x_ref[...].reshape(SEQ_PER_TILE, 8, tn) + g * acc).reshape(ROW_TILE, tn)


def _wout_call(merged, w, x, g_p, g_s, prompt_len):
    m = x.shape[0]
    tn = 1024
    n_prompt_seq = g_p.shape[0]
    npt = n_prompt_seq * prompt_len // ROW_TILE
    pspec, sspec = _mod_specs(npt, n_prompt_seq, prompt_len, tn, col=True)
    return pl.pallas_call(
        functools.partial(_wout_kernel, n_prompt_tiles=npt),
        grid=(D // tn, m // ROW_TILE),
        in_specs=[
            pl.BlockSpec((ROW_TILE, D), lambda j, i: (i, 0)),
            pl.BlockSpec((D, tn), lambda j, i: (0, j)),
            pl.BlockSpec((ROW_TILE, tn), lambda j, i: (i, j)),
            pspec, sspec,
        ],
        out_specs=pl.BlockSpec((ROW_TILE, tn), lambda j, i: (i, j)),
        out_shape=jax.ShapeDtypeStruct((m, D), f32),
        compiler_params=_cparams(("parallel", "parallel")),
        name="wout",
    )(merged, w, x, g_p, g_s)


def _tri_masks(n):
    t = lax.broadcasted_iota(i32, (n, n), 0)
    s = lax.broadcasted_iota(i32, (n, n), 1)
    return s <= t, s < t, s == t


def _rwkv_local(at, rt, bt, kt, v, bh, kh, lam):
    n = at[0].shape[0]
    incl, strict, eye = _tri_masks(n)
    eye_k = _tri_masks(HEAD_R)[2]
    each = lambda f, *ls: [f(*xs) for xs in zip(*ls)]
    g = each(lambda a_, r_, b_, k_: _bdot(jnp.concatenate([a_, r_], axis=0), jnp.concatenate([b_, k_], axis=0), NT),
             at, rt, bt, kt)
    n_mat = each(lambda x: jnp.where(strict, x[:n, :n], 0.0), g)
    m_mat = each(lambda x: jnp.where(strict, x[:n, n:], 0.0), g)
    pb = each(lambda x: jnp.where(incl, x[n:, :n], 0.0), g)
    pk = each(lambda x: jnp.where(incl, x[n:, n:], 0.0), g)
    t_mat = each(lambda x: jnp.where(eye, 1.0, x), n_mat)
    n_pow = n_mat
    lv = 2
    while lv < n:
        n_pow = each(lambda x: _bdot(x, x), n_pow)
        t_mat = each(lambda t_, p_: t_ + _bdot(t_, p_), t_mat, n_pow)
        lv *= 2
    tam = each(lambda t_, a_, m_: _bdot(t_, jnp.concatenate([a_, m_], axis=1)), t_mat, at, m_mat)
    qw = each(_bdot, pb, tam)
    q = each(lambda r_, x: r_ + x[:, :HEAD_R], rt, qw)
    w_loc = each(lambda x, p_, v_: _bdot(x[:, HEAD_R:] + p_, v_), qw, pk, v)
    ab = each(lambda x, b_: _bdot(x, b_, TN), tam, bh)
    a_tr = each(lambda l_, x: jnp.where(eye_k, l_, 0.0) + x[:HEAD_R], lam, ab)
    s_inc = each(lambda v_, k_, x: _bdot(v_, k_ + x[HEAD_R:], TN), v, kh, ab)
    return q, w_loc, a_tr, s_inc


def _split3(x):
    hi = x.astype(bf16)
    r1 = x - hi.astype(f32)
    mid = r1.astype(bf16)
    lo = (r1 - mid.astype(f32)).astype(bf16)
    return hi, mid, lo


def _dot3_left(w, x):
    return sum(jnp.dot(w, p, preferred_element_type=f32) for p in _split3(x))


def _dot3_right(x, w):
    return sum(jnp.dot(p, w, preferred_element_type=f32) for p in _split3(x))


def _rwkv_kernel(zr_ref, zk_ref, zv_ref, zs_ref, pr_ref, pk_ref, pv_ref, ps_ref,
                 mur_ref, muk_ref, muv_ref, mus_ref, w0_ref, a0_ref, kk_ref, ka_ref, rk_ref, lw_ref, lb_ref,
                 w2_ref, a2_ref, g2_ref, s0_ref,
                 y_ref, so_ref,
                 s_scr, prev_scr, at_scr, rt_scr, bt_scr, kt_scr, v_scr, bh_scr, kh_scr, lam_scr,
                 q_scr, wl_scr, y_scr, atr_scr, sinc_scr,
                 *, nb, tb_rows, chunk, hb, local):
    tb = pl.program_id(2)
    n_tb = pl.num_programs(2)
    rows = nb * tb_rows
    per_seq = tb_rows // chunk
    n_chunks = rows // chunk
    wc = hb * HEAD_R

    @pl.when(tb == 0)
    def _():
        if nb == 1:
            s_scr[...] = s0_ref[0]
            for slot, (p_ref, width) in enumerate(((pr_ref, wc), (pk_ref, wc), (pv_ref, wc), (ps_ref, SM_W))):
                prev_scr[slot, 0:1, 0:width] = p_ref[0:1, :]

    def shifted(z_ref, p_ref, slot, width):
        z = z_ref[...]
        rolled = pltpu.roll(z, 1, 0)
        row = lax.broadcasted_iota(i32, z.shape, 0)
        if nb == 1:
            zs = jnp.where(row == 0, prev_scr[slot, 0:1, 0:width], rolled)
            prev_scr[slot, 0:1, 0:width] = z[rows - 1:rows, :]
        else:
            zs = jnp.where(row % tb_rows == 0, p_ref[...], rolled)
        return z, zs

    def lerp(z_ref, p_ref, mu_ref, slot, width):
        z, zs = shifted(z_ref, p_ref, slot, width)
        return z + (zs - z) * mu_ref[...]

    r = lerp(zr_ref, pr_ref, mur_ref, 0, wc)
    k = lerp(zk_ref, pk_ref, muk_ref, 1, wc)
    v = lerp(zv_ref, pv_ref, muv_ref, 2, wc)
    sm = lerp(zs_ref, ps_ref, mus_ref, 3, SM_W)
    gl = sm[:, 0:G_LORA]
    wl = sm[:, G_LORA:G_LORA + 128]
    al = sm[:, G_LORA + 128:G_LORA + 256]
    w_raw = -_softplus(-(w0_ref[...] + _bdot(jnp.tanh(wl), w2_ref[...]))) - 0.5
    eta = jax.nn.sigmoid(a0_ref[...] + _bdot(al, a2_ref[...]))
    lw = -jnp.exp(w_raw)
    k2 = k * (1.0 + (eta - 1.0) * ka_ref[...])

    shift = chunk.bit_length() - 1
    ti = lax.broadcasted_iota(i32, (rows, rows), 0)
    si = lax.broadcasted_iota(i32, (rows, rows), 1)
    same_chunk = (ti >> shift) == (si >> shift)
    tri_sel = jnp.where(same_chunk & (si <= ti), 1.0, 0.0).astype(bf16)
    all_sel = jnp.where(same_chunk, 1.0, 0.0).astype(bf16)
    hshift = HEAD_R.bit_length() - 1
    head_sel = (lax.broadcasted_iota(i32, (wc, wc), 0) >> hshift) == (lax.broadcasted_iota(i32, (wc, wc), 1) >> hshift)
    head_sum = jnp.where(head_sel, 1.0, 0.0).astype(bf16)
    head_mean = jnp.where(head_sel, 1.0 / HEAD_R, 0.0).astype(bf16)

    c = k * kk_ref[...]
    c = c / jnp.maximum(jnp.sqrt(_dot3_right(c * c, head_sum)), 1e-12)
    b_vec = c * eta
    cum = _dot3_left(tri_sel, lw)
    tot = _dot3_left(all_sel, lw)
    e_out = jnp.exp(-cum)
    tail = jnp.exp(tot - cum)
    at_scr[...] = -c * jnp.exp(cum - lw)
    rt_scr[...] = r * jnp.exp(cum)
    bt_scr[...] = b_vec * e_out
    kt_scr[...] = k2 * e_out
    bh_scr[...] = b_vec * tail
    kh_scr[...] = k2 * tail
    lam_scr[...] = jnp.exp(tot)
    v_scr[...] = v
    bonus = _dot3_right(r * k2 * rk_ref[...], head_sum) * v
    gate = _bdot(jax.nn.sigmoid(gl), g2_ref[...])

    def local_body(it, carry):
        where = []
        for j in range(local):
            ci = it * local + j
            sl = pl.ds(pl.multiple_of(ci * chunk, chunk), chunk)
            where += [(ci, sl, h, slice(h * HEAD_R, (h + 1) * HEAD_R)) for h in range(hb)]
        load = lambda ref: [ref[sl, ls] for _, sl, _, ls in where]
        lam = [lam_scr[pl.ds(ci * chunk, 1), ls] for ci, _, _, ls in where]
        q, w_loc, a_tr, s_inc = _rwkv_local(load(at_scr), load(rt_scr), load(bt_scr), load(kt_scr), load(v_scr),
                                            load(bh_scr), load(kh_scr), lam)
        for i, (ci, sl, h, ls) in enumerate(where):
            q_scr[sl, ls] = q[i]
            wl_scr[sl, ls] = w_loc[i]
            atr_scr[ci, h] = a_tr[i]
            sinc_scr[ci, h] = s_inc[i]
        return carry

    lax.fori_loop(0, n_chunks // local, local_body, 0)

    def advance(ci, state):
        sl = pl.ds(ci * chunk if isinstance(ci, int) else pl.multiple_of(ci * chunk, chunk), chunk)
        ys, new_state = [], []
        for h in range(hb):
            ls = slice(h * HEAD_R, (h + 1) * HEAD_R)
            ys.append(_bdot(q_scr[sl, ls], state[h], NT) + wl_scr[sl, ls])
            new_state.append(_bdot(state[h], atr_scr[ci, h]) + sinc_scr[ci, h])
        for h in range(hb):
            y_scr[sl, h * HEAD_R:(h + 1) * HEAD_R] = ys[h]
        return new_state

    if nb == 1:
        state = [s_scr[0, h] for h in range(hb)]
        for ci in range(n_chunks):
            state = advance(ci, state)
        for h in range(hb):
            s_scr[0, h] = state[h]

        @pl.when(tb == n_tb - 1)
        def _():
            so_ref[...] = s_scr[...]
    else:
        def seq_body(it, carry):
            seqs = [it * local + j for j in range(local)]
            states = [[s0_ref[0, ci, h] for h in range(hb)] for ci in seqs]
            states = [advance(ci, st) for ci, st in zip(seqs, states)]
            for ci, st in zip(seqs, states):
                for h in range(hb):
                    so_ref[ci, h] = st[h]
            return carry

        lax.fori_loop(0, n_chunks // local, seq_body, 0)

    y = y_scr[...]
    dev = y - _dot3_right(y, head_mean)
    var = _dot3_right(dev * dev, head_mean)
    yn = dev * lax.rsqrt(var + LNX_EPS) * lw_ref[...] + lb_ref[...]
    y_ref[...] = ((yn + bonus) * gate).astype(y_ref.dtype)


def _rwkv_call(z, zprev_rows, s0, layer, p, *, nseq, seq_len, row0, nb, tb_rows, chunk, local):
    hb = 4
    wc = hb * HEAD_R
    rows = nb * tb_rows
    n_tb = seq_len // tb_rows
    assert nb == 1 or (tb_rows == chunk and n_tb == 1)
    rb0 = row0 // rows
    zrow = lambda sb, h, tb: rb0 + sb * n_tb + tb

    def zspec(col0, width):
        cb = col0 // width
        if width == wc:
            return pl.BlockSpec((rows, width), lambda sb, h, tb: (zrow(sb, h, tb), cb + h))
        return pl.BlockSpec((rows, width), lambda sb, h, tb: (zrow(sb, h, tb), cb))

    def pspec(col0, width):
        cb = col0 // width
        if width == wc:
            return pl.BlockSpec((nb * 8, width), lambda sb, h, tb: (sb, cb + h))
        return pl.BlockSpec((nb * 8, width), lambda sb, h, tb: (sb, cb))

    vec = lambda: pl.BlockSpec((1, wc), lambda sb, h, tb: (0, h))
    vec_at = lambda col0: pl.BlockSpec((1, wc), lambda sb, h, tb: (0, col0 // wc + h))
    lora = lambda kdim: pl.BlockSpec((kdim, wc), lambda sb, h, tb: (0, h))
    kern = functools.partial(_rwkv_kernel, nb=nb, tb_rows=tb_rows, chunk=chunk, hb=hb, local=local)
    scr = lambda: pltpu.VMEM((rows, wc), f32)
    return pl.pallas_call(
        kern,
        grid=(nseq // nb, NH_R // hb, n_tb),
        in_specs=[
            zspec(Z_RR, wc), zspec(Z_RK, wc), zspec(Z_RV, wc), zspec(Z_SM, SM_W),
            pspec(0, wc), pspec(D_RWKV, wc), pspec(2 * D_RWKV, wc), pspec(3 * D_RWKV, SM_W),
            vec_at(0), vec_at(D_RWKV), vec_at(2 * D_RWKV),
            pl.BlockSpec((1, SM_W), lambda sb, h, tb: (0, 3 * D_RWKV // SM_W)),
            vec(), vec(), vec(), vec(), vec(), vec(), vec(),
            lora(128), lora(128), lora(G_LORA),
            pl.BlockSpec((1, nb, hb, HEAD_R, HEAD_R), lambda sb, h, tb: (layer, sb, h, 0, 0)),
        ],
        out_specs=[
            pl.BlockSpec((rows, wc), lambda sb, h, tb: (sb * n_tb + tb, h)),
            pl.BlockSpec((nb, hb, HEAD_R, HEAD_R), lambda sb, h, tb: (sb, h, 0, 0)),
        ],
        out_shape=[
            jax.ShapeDtypeStruct((nseq * seq_len, D_RWKV), bf16),
            jax.ShapeDtypeStruct((nseq, NH_R, HEAD_R, HEAD_R), f32),
        ],
        scratch_shapes=[
            pltpu.VMEM((nb, hb, HEAD_R, HEAD_R), f32),
            pltpu.VMEM((4, 8, SM_W), f32),
            *[scr() for _ in range(11)],
            pltpu.VMEM((rows // chunk, hb, HEAD_R, HEAD_R), f32),
            pltpu.VMEM((rows // chunk, hb, HEAD_R, HEAD_R), f32),
        ],
        compiler_params=_cparams(("parallel", "parallel", "arbitrary")),
        name="rwkv",
    )(z, z, z, z, zprev_rows, zprev_rows, zprev_rows, zprev_rows,
      p["mu"], p["mu"], p["mu"], p["mu"], p["w0"], p["a0"], p["k_k"], p["k_a"], p["r_k"], p["lnx_w"], p["lnx_b"],
      p["w2"], p["a2"], p["g2"], s0)


def _mlstm_kernel(zq_ref, zk_ref, zv_ref, zo_ref, gate_ref, bias_ref, nw_ref, c0_ref, n0_ref, m0_ref,
                  y_ref, co_ref, no_ref, mo_ref,
                  c_scr, n_scr, m_scr, o_scr, *, nb, hm, tb_rows, chunk):
    tb = pl.program_id(2)
    n_tb = pl.num_programs(2)
    per_seq = tb_rows // chunk

    @pl.when(tb == 0)
    def _():
        c_scr[...] = c0_ref[0]
        n_scr[...] = n0_ref[0]
        m_scr[...] = m0_ref[0]

    incl, _, eye = _tri_masks(chunk)
    upper = lax.broadcasted_iota(i32, (chunk, chunk), 0) <= lax.broadcasted_iota(i32, (chunk, chunk), 1)
    pairs = [(s, h) for s in range(nb) for h in range(hm)]
    each = lambda f, *ls: [f(*xs) for xs in zip(*ls)]
    square = lambda row: jnp.broadcast_to(row, (chunk, chunk))
    row_sum = lambda x: jnp.sum(x, axis=-1, keepdims=True)

    def body(ci, carry):
        def load(ref):
            out = []
            for s, h in pairs:
                start = s * tb_rows + ci * chunk
                start = start if isinstance(start, int) else pl.multiple_of(start, chunk)
                out.append(ref[pl.ds(start, chunk), h * DK:(h + 1) * DK])
            return out

        q, v, o_gate = load(zq_ref), load(zv_ref), load(zo_ref)
        k = [x * (DK ** -0.5) for x in load(zk_ref)]
        gch = [gate_ref[h, s * per_seq + ci] for s, h in pairs]
        ic_row = [g[0:1, :] + bias_ref[h, 0:1, 0:1] for g, (s, h) in zip(gch, pairs)]
        fc_row = [-_softplus(-(g[1:2, :] + bias_ref[h, 1:2, 0:1])) for g, (s, h) in zip(gch, pairs)]
        c_mat = [c_scr[s, h] for s, h in pairs]
        n_row = [n_scr[s, h] for s, h in pairs]
        m_prev = [m_scr[s, h][:, 0:1] for s, h in pairs]
        f_col = each(lambda fr: row_sum(jnp.where(incl, square(fr), 0.0)), fc_row)
        fc_col = each(lambda fr: row_sum(jnp.where(eye, square(fr), 0.0)), fc_row)
        ic_col = each(lambda ir: row_sum(jnp.where(eye, square(ir), 0.0)), ic_row)
        f_row = each(lambda fc: jnp.sum(jnp.where(upper, square(fc), 0.0), axis=0, keepdims=True), fc_col)
        dm = each(lambda fc, fr, ir: jnp.where(incl, fc - fr + ir, -jnp.inf), f_col, f_row, ic_row)
        inter = each(lambda fc, m: fc + m, f_col, m_prev)
        m_t = each(lambda it_, d: jnp.maximum(it_, jnp.max(d, axis=-1, keepdims=True)), inter, dm)
        w_intra = each(lambda d, m: jnp.exp(d - m), dm, m_t)
        w_inter = each(lambda it_, m: jnp.exp(it_ - m), inter, m_t)
        s_mat = each(lambda q_, k_, w: _bdot(q_, k_, NT) * w, q, k, w_intra)
        qc = each(_bdot, q, c_mat)
        num = each(lambda s_, v_, w, x: _bdot(s_, v_) + w * x, s_mat, v, w_inter, qc)
        den = each(lambda s_, w, q_, n_: row_sum(s_) + w * row_sum(q_ * n_), s_mat, w_inter, q, n_row)
        hid = each(lambda a, d, m: a / jnp.maximum(jnp.abs(d), jnp.exp(-m)), num, den, m_t)
        m_new = each(lambda m: m[chunk - 1:chunk, :], m_t)
        w_s = each(lambda fc, ic, m: jnp.exp(fc[chunk - 1:chunk, :] - fc + ic - m), f_col, ic_col, m_new)
        w_c = each(lambda it_, m: jnp.exp(it_[chunk - 1:chunk, :] - m), inter, m_new)
        kw = each(lambda k_, w: k_ * w, k, w_s)
        c_new = each(lambda w, c, kw_, v_: w * c + _bdot(kw_, v_, TN), w_c, c_mat, kw, v)
        n_new = each(lambda w, n_, kw_: w * n_ + jnp.sum(kw_, axis=0, keepdims=True), w_c, n_row, kw)
        mean = each(lambda x: jnp.mean(x, axis=-1, keepdims=True), hid)
        var = each(lambda x, mu: jnp.mean(jnp.square(x - mu), axis=-1, keepdims=True), hid, mean)
        for i, (s, h) in enumerate(pairs):
            c_scr[s, h] = c_new[i]
            n_scr[s, h] = n_new[i]
            m_scr[s, h] = jnp.broadcast_to(m_new[i], (1, 128))
            hn = (hid[i] - mean[i]) * lax.rsqrt(var[i] + MH_EPS) * nw_ref[:, h * DK:(h + 1) * DK]
            start = s * tb_rows + ci * chunk
            start = start if isinstance(start, int) else pl.multiple_of(start, chunk)
            o_scr[pl.ds(start, chunk), h * DK:(h + 1) * DK] = hn * jax.nn.sigmoid(o_gate[i])
        return carry

    if per_seq == 1:
        body(0, 0)
    else:
        lax.fori_loop(0, per_seq, body, 0)
    y_ref[...] = o_scr[...].astype(y_ref.dtype)

    @pl.when(tb == n_tb - 1)
    def _():
        co_ref[...] = c_scr[...]
        no_ref[...] = n_scr[...]
        mo_ref[...] = m_scr[...]


def _mlstm_call(z, gates, bias, norm_w, c0, n0, m0, layer, *, nseq, seq_len, row0, nb, hm, tb_rows, chunk):
    rows = nb * tb_rows
    n_tb = seq_len // tb_rows
    rb0 = row0 // rows
    cpb = rows // chunk
    wm = hm * DK
    zspec = lambda col0: pl.BlockSpec((rows, wm), lambda sb, h, tb: (rb0 + sb * n_tb + tb, col0 // wm + h))
    st4 = lambda a, b: pl.BlockSpec((nb, hm, a, b), lambda sb, h, tb: (sb, h, 0, 0))
    st5 = lambda a, b: pl.BlockSpec((1, nb, hm, a, b), lambda sb, h, tb: (layer, sb, h, 0, 0))
    kern = functools.partial(_mlstm_kernel, nb=nb, hm=hm, tb_rows=tb_rows, chunk=chunk)
    return pl.pallas_call(
        kern,
        grid=(nseq // nb, NH_M // hm, n_tb),
        in_specs=[
            zspec(Z_MQ), zspec(Z_MK), zspec(Z_MV), zspec(Z_MO),
            pl.BlockSpec((hm, cpb, 2, chunk), lambda sb, h, tb: (h, sb * n_tb + tb, 0, 0)),
            pl.BlockSpec((hm, 2, 128), lambda sb, h, tb: (h, 0, 0)),
            pl.BlockSpec((1, wm), lambda sb, h, tb: (0, h)),
            st5(DK, DK), st5(1, DK), st5(1, 128),
        ],
        out_specs=[
            pl.BlockSpec((rows, wm), lambda sb, h, tb: (sb * n_tb + tb, h)),
            st4(DK, DK), st4(1, DK), st4(1, 128),
        ],
        out_shape=[
            jax.ShapeDtypeStruct((nseq * seq_len, D_MLSTM), bf16),
            jax.ShapeDtypeStruct((nseq, NH_M, DK, DK), f32),
            jax.ShapeDtypeStruct((nseq, NH_M, 1, DK), f32),
            jax.ShapeDtypeStruct((nseq, NH_M, 1, 128), f32),
        ],
        scratch_shapes=[
            pltpu.VMEM((nb, hm, DK, DK), f32),
            pltpu.VMEM((nb, hm, 1, DK), f32),
            pltpu.VMEM((nb, hm, 1, 128), f32),
            pltpu.VMEM((rows, wm), f32),
        ],
        compiler_params=_cparams(("parallel", "parallel", "arbitrary")),
        name="mlstm",
    )(z, z, z, z, gates, bias, norm_w, c0, n0, m0)


def _norm_route_kernel(x_ref, w_ref, scp_ref, scs_ref, shp_ref, shs_ref, rw_ref, rb_ref,
                       hn_ref, ei_ref, ew_ref, *, n_prompt_tiles):
    is_prompt = pl.program_id(0) < n_prompt_tiles
    y = _rms(x_ref[...], w_ref[...]).reshape(SEQ_PER_TILE, 8, D)
    sc = _pick_mod(is_prompt, scp_ref, scs_ref)
    sh = _pick_mod(is_prompt, shp_ref, shs_ref)
    hn = (y * (1.0 + sc) + sh).reshape(ROW_TILE, D)
    hn_ref[...] = hn
    logits = _bdot(hn, rw_ref[...]) + rb_ref[...]
    lane = lax.broadcasted_iota(i32, logits.shape, 1)
    lane_f = lane.astype(f32)
    valid = lane < N_EXPERTS
    lg = jnp.where(valid, logits, -1e30)
    ex = jnp.where(valid, jnp.exp(lg - jnp.max(lg, axis=-1, keepdims=True)), 0.0)
    probs = ex / jnp.sum(ex, axis=-1, keepdims=True)
    best = None
    for g in range(N_GROUPS):
        in_g = (lane >= g * EXPERTS_PER_GROUP) & (lane < (g + 1) * EXPERTS_PER_GROUP)
        pg = jnp.where(in_g, probs, -1.0)
        m1 = jnp.max(pg, axis=-1, keepdims=True)
        i1 = jnp.min(jnp.where(pg == m1, lane_f, 1e4), axis=-1, keepdims=True)
        pg2 = jnp.where(lane_f == i1, -1.0, pg)
        m2 = jnp.max(pg2, axis=-1, keepdims=True)
        i2 = jnp.min(jnp.where(pg2 == m2, lane_f, 1e4), axis=-1, keepdims=True)
        cand = (m1 + m2, m1, m2, i1, i2)
        if best is None:
            best = cand
        else:
            upd = cand[0] > best[0]
            best = tuple(jnp.where(upd, c, b) for c, b in zip(cand, best))
    _, m1, m2, i1, i2 = best
    tot = m1 + m2
    ei_ref[...] = jnp.where(lane == 0, i1, jnp.where(lane == 1, i2, 0.0)).astype(i32)
    ew_ref[...] = jnp.where(lane == 0, m1 / tot, jnp.where(lane == 1, m2 / tot, 0.0))


def _norm_route_call(x, w, sc_p, sc_s, sh_p, sh_s, rw, rb, prompt_len):
    m = x.shape[0]
    n_prompt_seq = sc_p.shape[0]
    npt = n_prompt_seq * prompt_len // ROW_TILE
    pspec, sspec = _mod_specs(npt, n_prompt_seq, prompt_len, D)
    row = lambda wd: pl.BlockSpec((ROW_TILE, wd), lambda i: (i, 0))
    return pl.pallas_call(
        functools.partial(_norm_route_kernel, n_prompt_tiles=npt),
        grid=(m // ROW_TILE,),
        in_specs=[row(D), pl.BlockSpec((1, D), lambda i: (0, 0)), pspec, sspec, pspec, sspec,
                  pl.BlockSpec((D, 128), lambda i: (0, 0)), pl.BlockSpec((1, 128), lambda i: (0, 0))],
        out_specs=[row(D), row(128), row(128)],
        out_shape=[jax.ShapeDtypeStruct((m, D), f32), jax.ShapeDtypeStruct((m, 128), i32),
                   jax.ShapeDtypeStruct((m, 128), f32)],
        compiler_params=_cparams(("parallel",)),
        name="norm2_route",
    )(x, w.reshape(1, D), sc_p, sc_s, sh_p, sh_s, rw, rb)


def _row_copy(src_hbm, row, dst, r, sem):
    return pltpu.make_async_copy(src_hbm.at[pl.ds(row, 1), :], dst.at[pl.ds(r, 1), :], sem)


def _gather_kernel(tok_ref, valid_ref, hn_hbm, o_ref, buf, sem):
    j = pl.program_id(0)
    n_j = pl.num_programs(0)

    def issue(block, slot):
        def one(r, c):
            _row_copy(hn_hbm, tok_ref[block * MOE_TM + r], buf.at[slot], r, sem.at[slot]).start()
            return c
        lax.fori_loop(0, MOE_TM, one, 0, unroll=8)

    def wait(slot):
        def one(r, c):
            _row_copy(hn_hbm, 0, buf.at[slot], r, sem.at[slot]).wait()
            return c
        lax.fori_loop(0, MOE_TM, one, 0, unroll=8)

    @pl.when((j == 0) & (valid_ref[0] == 1))
    def _():
        issue(0, 0)

    nxt = jnp.minimum(j + 1, n_j - 1)

    @pl.when((j + 1 < n_j) & (valid_ref[nxt] == 1))
    def _():
        issue(j + 1, (j + 1) % 2)

    @pl.when(valid_ref[j] == 1)
    def _():
        wait(j % 2)
        o_ref[...] = buf[j % 2].astype(o_ref.dtype)

    @pl.when(valid_ref[j] == 0)
    def _():
        o_ref[...] = jnp.zeros_like(o_ref)


def _gather_call(hn, slot_tok, valid):
    p = slot_tok.shape[0]
    return pl.pallas_call(
        _gather_kernel,
        grid_spec=pltpu.PrefetchScalarGridSpec(
            num_scalar_prefetch=2,
            grid=(p // MOE_TM,),
            in_specs=[pl.BlockSpec(memory_space=pl.ANY)],
            out_specs=pl.BlockSpec((MOE_TM, D), lambda j, tok, va: (j, 0)),
            scratch_shapes=[pltpu.VMEM((2, MOE_TM, D), f32), pltpu.SemaphoreType.DMA((2,))],
        ),
        out_shape=jax.ShapeDtypeStruct((p, D), bf16),
        compiler_params=_cparams(("arbitrary",)),
        name="moe_gather",
    )(slot_tok, valid, hn)


def _moe1_kernel(be_ref, first_ref, valid_ref, x_ref, wg_ref, wu_ref, act_ref, wg_bf, wu_bf):
    j = pl.program_id(1)

    @pl.when(first_ref[j] == 1)
    def _():
        wg_bf[...] = wg_ref[0, 0].astype(bf16)
        wu_bf[...] = wu_ref[0, 0].astype(bf16)

    @pl.when(valid_ref[j] == 1)
    def _():
        x = x_ref[...]
        g = jnp.dot(x, wg_bf[...], preferred_element_type=f32)
        u = jnp.dot(x, wu_bf[...], preferred_element_type=f32)
        act_ref[...] = (g * jax.nn.sigmoid(g) * u).astype(act_ref.dtype)

    @pl.when(valid_ref[j] == 0)
    def _():
        act_ref[...] = jnp.zeros_like(act_ref)


def _moe1_call(xs, w_in, layer, block_e, first, valid):
    p = xs.shape[0]
    tn = 256
    n_t = D_EXPERT // tn
    return pl.pallas_call(
        _moe1_kernel,
        grid_spec=pltpu.PrefetchScalarGridSpec(
            num_scalar_prefetch=3,
            grid=(n_t, p // MOE_TM),
            in_specs=[
                pl.BlockSpec((MOE_TM, D), lambda n, j, be, fi, va: (j, 0)),
                pl.BlockSpec((1, 1, D, tn), lambda n, j, be, fi, va: (layer, be[j], 0, n)),
                pl.BlockSpec((1, 1, D, tn), lambda n, j, be, fi, va: (layer, be[j], 0, n_t + n)),
            ],
            out_specs=pl.BlockSpec((MOE_TM, tn), lambda n, j, be, fi, va: (j, n)),
            scratch_shapes=[pltpu.VMEM((D, tn), bf16), pltpu.VMEM((D, tn), bf16)],
        ),
        out_shape=jax.ShapeDtypeStruct((p, D_EXPERT), bf16),
        compiler_params=_cparams(("arbitrary", "arbitrary")),
        name="moe_up",
    )(block_e, first, valid, xs, w_in, w_in)


def _moe2_kernel(be_ref, first_ref, valid_ref, a_ref, w_ref, o_ref, w_bf):
    j = pl.program_id(1)

    @pl.when(first_ref[j] == 1)
    def _():
        w_bf[...] = w_ref[0, 0].astype(bf16)

    @pl.when(valid_ref[j] == 1)
    def _():
        o_ref[...] = jnp.dot(a_ref[...], w_bf[...], preferred_element_type=f32)

    @pl.when(valid_ref[j] == 0)
    def _():
        o_ref[...] = jnp.zeros_like(o_ref)


def _moe2_call(act, w_out, layer, block_e, first, valid):
    p = act.shape[0]
    tn = 1024
    return pl.pallas_call(
        _moe2_kernel,
        grid_spec=pltpu.PrefetchScalarGridSpec(
            num_scalar_prefetch=3,
            grid=(D // tn, p // MOE_TM),
            in_specs=[
                pl.BlockSpec((MOE_TM, D_EXPERT), lambda n, j, be, fi, va: (j, 0)),
                pl.BlockSpec((1, 1, D_EXPERT, tn), lambda n, j, be, fi, va: (layer, be[j], 0, n)),
            ],
            out_specs=pl.BlockSpec((MOE_TM, tn), lambda n, j, be, fi, va: (j, n)),
            scratch_shapes=[pltpu.VMEM((D_EXPERT, tn), bf16)],
        ),
        out_shape=jax.ShapeDtypeStruct((p, D), f32),
        compiler_params=_cparams(("arbitrary", "arbitrary")),
        name="moe_down",
    )(block_e, first, valid, act, w_out)


def _combine_kernel(pos_ref, x_ref, ew_ref, gp_ref, gs_ref, yb_hbm, o_ref, buf, sem, *, n_prompt_tiles):
    i = pl.program_id(0)
    n_i = pl.num_programs(0)

    def issue(tile, slot):
        def one(r, c):
            for k in range(2):
                _row_copy(yb_hbm, pos_ref[2 * (tile * ROW_TILE + r) + k], buf.at[slot, k], r, sem.at[slot]).start()
            return c
        lax.fori_loop(0, ROW_TILE, one, 0, unroll=4)

    def wait(slot):
        def one(r, c):
            for k in range(2):
                _row_copy(yb_hbm, 0, buf.at[slot, k], r, sem.at[slot]).wait()
            return c
        lax.fori_loop(0, ROW_TILE, one, 0, unroll=4)

    @pl.when(i == 0)
    def _():
        issue(0, 0)

    @pl.when(i + 1 < n_i)
    def _():
        issue(i + 1, (i + 1) % 2)

    wait(i % 2)
    ew = ew_ref[...]
    y = buf[i % 2, 0] * ew[:, 0:1] + buf[i % 2, 1] * ew[:, 1:2]
    g = _pick_mod(i < n_prompt_tiles, gp_ref, gs_ref)
    o_ref[...] = (x_ref[...].reshape(SEQ_PER_TILE, 8, D) + g * y.reshape(SEQ_PER_TILE, 8, D)).reshape(ROW_TILE, D)


def _combine_call(x, yb, pos, ew, g_p, g_s, prompt_len):
    m = x.shape[0]
    n_prompt_seq = g_p.shape[0]
    npt = n_prompt_seq * prompt_len // ROW_TILE
    tiles_per_seq = prompt_len // ROW_TILE
    return pl.pallas_call(
        functools.partial(_combine_kernel, n_prompt_tiles=npt),
        grid_spec=pltpu.PrefetchScalarGridSpec(
            num_scalar_prefetch=1,
            grid=(m // ROW_TILE,),
            in_specs=[
                pl.BlockSpec((ROW_TILE, D), lambda i, pos: (i, 0)),
                pl.BlockSpec((ROW_TILE, 128), lambda i, pos: (i, 0)),
                pl.BlockSpec((1, 1, D), lambda i, pos: (jnp.minimum(i // tiles_per_seq, n_prompt_seq - 1), 0, 0)),
                pl.BlockSpec((SEQ_PER_TILE, 1, D), lambda i, pos: (jnp.maximum(i - npt, 0), 0, 0)),
                pl.BlockSpec(memory_space=pl.ANY),
            ],
            out_specs=pl.BlockSpec((ROW_TILE, D), lambda i, pos: (i, 0)),
            scratch_shapes=[pltpu.VMEM((2, 2, ROW_TILE, D), f32), pltpu.SemaphoreType.DMA((2,))],
        ),
        out_shape=jax.ShapeDtypeStruct((m, D), f32),
        compiler_params=_cparams(("arbitrary",)),
        name="moe_combine",
    )(pos, x, ew, g_p, g_s, yb)


def _moe_plan(eidx, n_blocks):
    m = eidx.shape[0]
    a = 2 * m
    flat_e = eidx.reshape(a)
    order = jnp.argsort(flat_e).astype(i32)
    sorted_e = flat_e[order]
    counts = jnp.zeros((N_EXPERTS,), i32).at[flat_e].add(1)
    padded = (counts + MOE_TM - 1) // MOE_TM * MOE_TM
    start = jnp.cumsum(counts) - counts
    pad_end = jnp.cumsum(padded)
    pad_start = pad_end - padded
    dest = pad_start[sorted_e] + jnp.arange(a, dtype=i32) - start[sorted_e]
    slot_tok = jnp.zeros((n_blocks * MOE_TM,), i32).at[dest].set(order // 2)
    pos = jnp.zeros((a,), i32).at[order].set(dest)
    blk = jnp.arange(n_blocks, dtype=i32) * MOE_TM
    block_e = jnp.minimum(jnp.searchsorted(pad_end, blk, side="right"), N_EXPERTS - 1).astype(i32)
    valid = (blk < pad_end[-1]).astype(i32)
    first = jnp.concatenate([jnp.ones((1,), i32), (block_e[1:] != block_e[:-1]).astype(i32)])
    return slot_tok, pos, block_e, first, valid


def _align_cols(w):
    zeros = lambda n: jnp.zeros((w.shape[0], n), w.dtype)
    mb = RWKV_COLS
    gb = RWKV_COLS + MLSTM_COLS
    parts = [
        w[:, 0:2048], w[:, 2144:4192], w[:, 4192:6240],
        w[:, mb:mb + 2048], w[:, mb + 2048:mb + 4096], w[:, mb + 4096:mb + 6144], w[:, mb + 6160:mb + 8208],
        w[:, gb:gb + 8192],
        w[:, 6336:6592], w[:, 2048:2144], zeros(32), w[:, 6240:6336], zeros(32),
        w[:, mb + 6144:mb + 6160], zeros(112), zeros(NZ - Z_IF - 128),
    ]
    return jnp.concatenate(parts, axis=1)


def _prep_w_kernel(w_ref, o_ref):
    o_ref[...] = _align_cols(w_ref[0]).astype(o_ref.dtype)


def _prep_w_in(w_in, layer):
    tr = 128
    n_src = w_in.shape[2]
    return pl.pallas_call(
        _prep_w_kernel,
        grid=(D // tr,),
        in_specs=[pl.BlockSpec((1, tr, n_src), lambda i: (layer, i, 0))],
        out_specs=pl.BlockSpec((tr, NZ), lambda i: (i, 0)),
        out_shape=jax.ShapeDtypeStruct((D, NZ), bf16),
        compiler_params=_cparams(("parallel",), 56 * 1024 * 1024),
        name="prep_w_in",
    )(w_in)


def _prep_mu(mu):
    zeros = jnp.zeros((32,), mu.dtype)
    return jnp.concatenate([mu[0:2048], mu[2144:4192], mu[4192:6240], mu[6336:6592], mu[2048:2144], zeros,
                            mu[6240:6336], zeros]).reshape(1, ZP_COLS)


def _pad_rows(w, rows):
    return jnp.concatenate([w, jnp.zeros((rows - w.shape[0], w.shape[1]), w.dtype)], axis=0).astype(bf16)


def _gate_rows(z, row0, rows, chunk):
    zif = z[row0:row0 + rows, Z_IF:Z_IF + 2 * NH_M].reshape(rows // chunk, chunk, 2, NH_M)
    return jnp.transpose(zif, (3, 0, 2, 1))


def kernel(x_prompt, x_sample, c_prompt, c_sample, state_wkv, state_shift, state_C, state_n, state_m, ada_w, ada_b, norm1_w, norm2_w, w_in, rwkv_mu, rwkv_w0, rwkv_w2, rwkv_a0, rwkv_a2, rwkv_g2, rwkv_k_k, rwkv_k_a, rwkv_r_k, rwkv_lnx_w, rwkv_lnx_b, mlstm_i_b, mlstm_f_b, mlstm_norm_w, w_branch_r, w_branch_m, w_out, router_w, router_b, moe_w_in, moe_w_out, final_norm_w):
    bp, tp, _ = x_prompt.shape
    bs, ts, _ = x_sample.shape
    mp, ms = bp * tp, bs * ts
    m = mp + ms
    assert ts == 8 and tp % ROW_TILE == 0 and ms % ROW_TILE == 0 and bs % SEQ_PER_TILE == 0 and m % 512 == 0
    n_seq = bp + bs
    n_seq_pad = -(-n_seq // 8) * 8

    x = jnp.concatenate([x_prompt.reshape(mp, D), x_sample.reshape(ms, D)], axis=0)
    c_all = jnp.concatenate([c_prompt, c_sample, jnp.zeros((n_seq_pad - n_seq, D), f32)], axis=0)
    ada = _ada_call(c_all, ada_w, ada_b).reshape(DEPTH, n_seq_pad, N_ADA, D)
    last_rows = jnp.concatenate([jnp.arange(bp, dtype=i32) * tp + tp - 1,
                                 mp + jnp.arange(bs, dtype=i32) * ts + ts - 1,
                                 jnp.zeros((n_seq_pad - n_seq,), i32)])
    router_w_p = jnp.concatenate([router_w, jnp.zeros((D, 128 - N_EXPERTS), f32)], axis=1).astype(bf16)
    router_b_p = jnp.concatenate([router_b, jnp.zeros((128 - N_EXPERTS,), f32)]).reshape(1, 128)
    n_blocks = -(-(2 * m + N_EXPERTS * (MOE_TM - 1)) // MOE_TM)

    prompt_geo = dict(nseq=bp, seq_len=tp, row0=0, nb=1, tb_rows=256, chunk=64)
    sample_geo = dict(nseq=bs, seq_len=ts, row0=mp, tb_rows=8, chunk=8)
    rwkv_p, rwkv_s = dict(prompt_geo, local=4), dict(sample_geo, nb=16, local=4)
    mlstm_p, mlstm_s = dict(prompt_geo, hm=4), dict(sample_geo, nb=4, hm=4)

    outs_p = [[] for _ in range(5)]
    outs_s = [[] for _ in range(5)]
    for l in range(DEPTH):
        mod = lambda idx: (ada[l, :bp, idx][:, None, :], ada[l, bp:n_seq, idx][:, None, :])
        sh1, sc1, g1, sh2, sc2, g2 = (mod(i) for i in range(N_ADA))
        wz = _prep_w_in(w_in, l)

        xn = _norm_mod_call(x, norm1_w[l], sc1[0], sc1[1], sh1[0], sh1[1], tp)
        shift_rows = _norm_rows_call(x[last_rows], norm1_w[l], ada[l, :, 1], ada[l, :, 0])
        z = _matmul_call(xn, wz, 512, 1024, name="w_in")
        zprev_s = _matmul_call(state_shift[l].astype(bf16), wz, bs, 512, n_out=ZP_COLS,
                               col_map=lambda j: jnp.where(j < 12, j, Z_SM // 512), name="w_in_prev")
        rp = dict(mu=_prep_mu(rwkv_mu[l]), w0=rwkv_w0[l].reshape(1, -1), a0=rwkv_a0[l].reshape(1, -1),
                  k_k=rwkv_k_k[l].reshape(1, -1), k_a=rwkv_k_a[l].reshape(1, -1), r_k=rwkv_r_k[l].reshape(1, -1),
                  lnx_w=rwkv_lnx_w[l].reshape(1, -1), lnx_b=rwkv_lnx_b[l].reshape(1, -1),
                  w2=_pad_rows(rwkv_w2[l], 128), a2=_pad_rows(rwkv_a2[l], 128), g2=rwkv_g2[l].astype(bf16))
        yr_p, wkv_p = _rwkv_call(z, jnp.zeros((bp * 8, ZP_COLS), f32),
                                 jnp.zeros((1, bp, NH_R, HEAD_R, HEAD_R), f32), 0, rp, **rwkv_p)
        yr_s, wkv_s = _rwkv_call(z, jnp.repeat(zprev_s, 8, axis=0), state_wkv, l, rp, **rwkv_s)
        bias = jnp.broadcast_to(jnp.stack([mlstm_i_b[l], mlstm_f_b[l]], axis=1)[:, :, None], (NH_M, 2, 128))
        nw = mlstm_norm_w[l].reshape(1, -1)
        ym_p, c_p, n_p, m_p = _mlstm_call(
            z, _gate_rows(z, 0, mp, 64), bias, nw, jnp.zeros((1, bp, NH_M, DK, DK), f32),
            jnp.zeros((1, bp, NH_M, 1, DK), f32), jnp.zeros((1, bp, NH_M, 1, 128), f32), 0, **mlstm_p)
        ym_s, c_s, n_s, m_s = _mlstm_call(
            z, _gate_rows(z, mp, ms, 8), bias, nw, state_C, state_n[:, :, :, None, :],
            jnp.broadcast_to(state_m[:, :, :, None, None], (DEPTH, bs, NH_M, 1, 128)), l, **mlstm_s)
        merged = _merge_call(jnp.concatenate([yr_p, yr_s], axis=0), jnp.concatenate([ym_p, ym_s], axis=0),
                             w_branch_r[l].astype(bf16), w_branch_m[l].astype(bf16), z)
        x = _wout_call(merged, w_out[l].astype(bf16), x, g1[0], g1[1], tp)

        hn, eidx, ew = _norm_route_call(x, norm2_w[l], sc2[0], sc2[1], sh2[0], sh2[1], router_w_p, router_b_p, tp)
        slot_tok, pos, block_e, first, valid = _moe_plan(eidx[:, :2], n_blocks)
        xs = _gather_call(hn, slot_tok, valid)
        act = _moe1_call(xs, moe_w_in, l, block_e, first, valid)
        yb = _moe2_call(act, moe_w_out, l, block_e, first, valid)
        x = _combine_call(x, yb, pos, ew, g2[0], g2[1], tp)

        for acc, vals in ((outs_p, (wkv_p, shift_rows[:bp], c_p, n_p[:, :, 0], m_p[:, :, 0, 0])),
                          (outs_s, (wkv_s, shift_rows[bp:n_seq], c_s, n_s[:, :, 0], m_s[:, :, 0, 0]))):
            for lst, v in zip(acc, vals):
                lst.append(v)

    y_prompt = _final_norm_call(x, final_norm_w, 0, mp).reshape(bp, tp, D)
    y_sample = _final_norm_call(x, final_norm_w, mp, ms).reshape(bs, ts, D)
    return (y_prompt, y_sample, *(jnp.stack(v) for v in outs_p), *(jnp.stack(v) for v in outs_s))
```

```python
import functools
import math

import jax
import jax.numpy as jnp
from jax import lax
from jax.experimental import pallas as pl
from jax.experimental.pallas import tpu as pltpu

f32 = jnp.float32
bf16 = jnp.bfloat16
i32 = jnp.int32

D = 4096
DEPTH = 2
D_RWKV = 2048
HEAD_R = 64
NH_R = D_RWKV // HEAD_R
W_LORA = 96
A_LORA = 96
G_LORA = 256
LNX_EPS = 64e-5
D_MLSTM = 2048
NH_M = 8
DK = D_MLSTM // NH_M
MH_EPS = 1e-6
RWKV_COLS = 3 * D_RWKV + W_LORA + A_LORA + G_LORA
MLSTM_COLS = 4 * D_MLSTM + 2 * NH_M
N_EXPERTS = 32
EXPERTS_PER_GROUP = 8
N_GROUPS = 4
D_EXPERT = D // 4
N_ADA = 6
RMS_EPS = 1e-6

Z_RR, Z_RK, Z_RV = 0, 2048, 4096
Z_MQ, Z_MK, Z_MV, Z_MO = 6144, 8192, 10240, 12288
Z_GR, Z_GM = 14336, 18432
Z_SM = 22528
Z_IF = 23040
NZ = 23552
SM_W = 512
ZP_COLS = 3 * D_RWKV + SM_W

ROW_TILE = 256
SEQ_PER_TILE = 32
MOE_TM = 256
VMEM_LIMIT = 48 * 1024 * 1024

NN = (((1,), (0,)), ((), ()))
NT = (((1,), (1,)), ((), ()))
TN = (((0,), (0,)), ((), ()))


def _cparams(sem, vmem=VMEM_LIMIT):
    return pltpu.CompilerParams(dimension_semantics=sem, vmem_limit_bytes=vmem)


def _bdot(a, b, dims=NN):
    return lax.dot_general(a.astype(bf16), b.astype(bf16), dims, preferred_element_type=f32)


def _softplus(x):
    return jnp.maximum(x, 0.0) + jnp.log1p(jnp.exp(-jnp.abs(x)))


def _ada_kernel(c_ref, w_ref, b_ref, o_ref):
    c = c_ref[...]
    a = c * jax.nn.sigmoid(c)
    o_ref[0] = _bdot(a, w_ref[0]) + b_ref[0]


def _ada_call(c_all, ada_w, ada_b):
    nb = c_all.shape[0]
    n = ada_w.shape[2]
    tn = 512
    return pl.pallas_call(
        _ada_kernel,
        grid=(DEPTH, n // tn),
        in_specs=[
            pl.BlockSpec((nb, D), lambda l, j: (0, 0)),
            pl.BlockSpec((1, D, tn), lambda l, j: (l, 0, j)),
            pl.BlockSpec((1, 1, tn), lambda l, j: (l, 0, j)),
        ],
        out_specs=pl.BlockSpec((1, nb, tn), lambda l, j: (l, 0, j)),
        out_shape=jax.ShapeDtypeStruct((DEPTH, nb, n), f32),
        compiler_params=_cparams(("parallel", "parallel")),
        name="ada",
    )(c_all, ada_w, ada_b.reshape(DEPTH, 1, n))


def _mod_specs(n_prompt_tiles, n_prompt_seq, prompt_len, width, col=None):
    tiles_per_seq = prompt_len // ROW_TILE
    if col is None:
        pmap = lambda i: (jnp.minimum(i // tiles_per_seq, n_prompt_seq - 1), 0, 0)
        smap = lambda i: (jnp.maximum(i - n_prompt_tiles, 0), 0, 0)
    else:
        pmap = lambda j, i: (jnp.minimum(i // tiles_per_seq, n_prompt_seq - 1), 0, j)
        smap = lambda j, i: (jnp.maximum(i - n_prompt_tiles, 0), 0, j)
    return pl.BlockSpec((1, 1, width), pmap), pl.BlockSpec((SEQ_PER_TILE, 1, width), smap)


def _pick_mod(is_prompt, p_ref, s_ref):
    return jnp.where(is_prompt, jnp.broadcast_to(p_ref[...], s_ref.shape), s_ref[...])


def _rms(x, w):
    return x * lax.rsqrt(jnp.mean(x * x, axis=-1, keepdims=True) + RMS_EPS) * w


def _norm_mod_kernel(x_ref, w_ref, scp_ref, scs_ref, shp_ref, shs_ref, o_ref, *, n_prompt_tiles):
    is_prompt = pl.program_id(0) < n_prompt_tiles
    y = _rms(x_ref[...], w_ref[...]).reshape(SEQ_PER_TILE, 8, D)
    sc = _pick_mod(is_prompt, scp_ref, scs_ref)
    sh = _pick_mod(is_prompt, shp_ref, shs_ref)
    o_ref[...] = (y * (1.0 + sc) + sh).reshape(ROW_TILE, D).astype(o_ref.dtype)


def _norm_mod_call(x, w, sc_p, sc_s, sh_p, sh_s, prompt_len):
    m = x.shape[0]
    n_prompt_seq = sc_p.shape[0]
    npt = n_prompt_seq * prompt_len // ROW_TILE
    pspec, sspec = _mod_specs(npt, n_prompt_seq, prompt_len, D)
    return pl.pallas_call(
        functools.partial(_norm_mod_kernel, n_prompt_tiles=npt),
        grid=(m // ROW_TILE,),
        in_specs=[pl.BlockSpec((ROW_TILE, D), lambda i: (i, 0)), pl.BlockSpec((1, D), lambda i: (0, 0)),
                  pspec, sspec, pspec, sspec],
        out_specs=pl.BlockSpec((ROW_TILE, D), lambda i: (i, 0)),
        out_shape=jax.ShapeDtypeStruct((m, D), bf16),
        compiler_params=_cparams(("parallel",)),
        name="norm1",
    )(x, w.reshape(1, D), sc_p, sc_s, sh_p, sh_s)


def _norm_rows_kernel(x_ref, w_ref, sc_ref, sh_ref, o_ref):
    o_ref[...] = _rms(x_ref[...], w_ref[...]) * (1.0 + sc_ref[...]) + sh_ref[...]


def _norm_rows_call(x_rows, w, sc_rows, sh_rows):
    return pl.pallas_call(
        _norm_rows_kernel,
        out_shape=jax.ShapeDtypeStruct(x_rows.shape, f32),
        name="norm_rows",
    )(x_rows, w.reshape(1, D), sc_rows, sh_rows)


def _final_norm_kernel(x_ref, w_ref, o_ref):
    o_ref[...] = _rms(x_ref[...], w_ref[...])


def _final_norm_call(x, w, row0, rows):
    off = row0 // ROW_TILE
    return pl.pallas_call(
        _final_norm_kernel,
        grid=(rows // ROW_TILE,),
        in_specs=[pl.BlockSpec((ROW_TILE, D), lambda i: (i + off, 0)), pl.BlockSpec((1, D), lambda i: (0, 0))],
        out_specs=pl.BlockSpec((ROW_TILE, D), lambda i: (i, 0)),
        out_shape=jax.ShapeDtypeStruct((rows, D), f32),
        compiler_params=_cparams(("parallel",)),
        name="final_norm",
    )(x, w.reshape(1, D))


def _mm_kernel(a_ref, wt_ref, o_ref):
    o_ref[...] = lax.dot_general(a_ref[...], wt_ref[...], NT, preferred_element_type=f32)


def _matmul_call(a, wt, tm, tn, n_out=None, col_map=None, name="matmul"):
    m, k = a.shape
    n_out = wt.shape[0] if n_out is None else n_out
    wmap = (lambda j, i: (j, 0)) if col_map is None else (lambda j, i: (col_map(j), 0))
    return pl.pallas_call(
        _mm_kernel,
        grid=(n_out // tn, m // tm),
        in_specs=[pl.BlockSpec((tm, k), lambda j, i: (i, 0)), pl.BlockSpec((tn, k), wmap)],
        out_specs=pl.BlockSpec((tm, tn), lambda j, i: (i, j)),
        out_shape=jax.ShapeDtypeStruct((m, n_out), f32),
        compiler_params=_cparams(("parallel", "parallel")),
        name=name,
    )(a, wt)


def _merge_kernel(yr_ref, ym_ref, wr_ref, wm_ref, gr_ref, gm_ref, o_ref):
    pr = jnp.dot(yr_ref[...], wr_ref[...], preferred_element_type=f32)
    pm = jnp.dot(ym_ref[...], wm_ref[...], preferred_element_type=f32)
    o_ref[...] = (jax.nn.sigmoid(gr_ref[...]) * pr + jax.nn.sigmoid(gm_ref[...]) * pm).astype(o_ref.dtype)


def _merge_call(yr, ym, wr, wm, z):
    m = yr.shape[0]
    tm, tn = 512, 1024
    return pl.pallas_call(
        _merge_kernel,
        grid=(D // tn, m // tm),
        in_specs=[
            pl.BlockSpec((tm, D_RWKV), lambda j, i: (i, 0)),
            pl.BlockSpec((tm, D_MLSTM), lambda j, i: (i, 0)),
            pl.BlockSpec((D_RWKV, tn), lambda j, i: (0, j)),
            pl.BlockSpec((D_MLSTM, tn), lambda j, i: (0, j)),
            pl.BlockSpec((tm, tn), lambda j, i: (i, Z_GR // tn + j)),
            pl.BlockSpec((tm, tn), lambda j, i: (i, Z_GM // tn + j)),
        ],
        out_specs=pl.BlockSpec((tm, tn), lambda j, i: (i, j)),
        out_shape=jax.ShapeDtypeStruct((m, D), bf16),
        compiler_params=_cparams(("parallel", "parallel")),
        name="merge",
    )(yr, ym, wr, wm, z, z)


def _wout_kernel(a_ref, w_ref, x_ref, gp_ref, gs_ref, o_ref, *, n_prompt_tiles):
    is_prompt = pl.program_id(1) < n_prompt_tiles
    tn = o_ref.shape[1]
    acc = jnp.dot(a_ref[...], w_ref[...], preferred_element_type=f32).reshape(SEQ_PER_TILE, 8, tn)
    g = _pick_mod(is_prompt, gp_ref, gs_ref)
    o_ref[...] = (x_ref[...].reshape(SEQ_PER_TILE, 8, tn) + g * acc).reshape(ROW_TILE, tn)


def _wout_call(merged, w, x, g_p, g_s, prompt_len):
    m = x.shape[0]
    tn = 1024
    n_prompt_seq = g_p.shape[0]
    npt = n_prompt_seq * prompt_len // ROW_TILE
    pspec, sspec = _mod_specs(npt, n_prompt_seq, prompt_len, tn, col=True)
    return pl.pallas_call(
        functools.partial(_wout_kernel, n_prompt_tiles=npt),
        grid=(D // tn, m // ROW_TILE),
        in_specs=[
            pl.BlockSpec((ROW_TILE, D), lambda j, i: (i, 0)),
            pl.BlockSpec((D, tn), lambda j, i: (0, j)),
            pl.BlockSpec((ROW_TILE, tn), lambda j, i: (i, j)),
            pspec, sspec,
        ],
        out_specs=pl.BlockSpec((ROW_TILE, tn), lambda j, i: (i, j)),
        out_shape=jax.ShapeDtypeStruct((m, D), f32),
        compiler_params=_cparams(("parallel", "parallel")),
        name="wout",
    )(merged, w, x, g_p, g_s)


def _tri_masks(n):
    t = lax.broadcasted_iota(i32, (n, n), 0)
    s = lax.broadcasted_iota(i32, (n, n), 1)
    return s <= t, s < t, s == t


def _rwkv_local(at, rt, bt, kt, v, bh, kh, lam):
    n = at[0].shape[0]
    incl, strict, eye = _tri_masks(n)
    eye_k = _tri_masks(HEAD_R)[2]
    each = lambda f, *ls: [f(*xs) for xs in zip(*ls)]
    g = each(lambda a_, r_, b_, k_: _bdot(jnp.concatenate([a_, r_], axis=0), jnp.concatenate([b_, k_], axis=0), NT),
             at, rt, bt, kt)
    n_mat = each(lambda x: jnp.where(strict, x[:n, :n], 0.0), g)
    m_mat = each(lambda x: jnp.where(strict, x[:n, n:], 0.0), g)
    pb = each(lambda x: jnp.where(incl, x[n:, :n], 0.0), g)
    pk = each(lambda x: jnp.where(incl, x[n:, n:], 0.0), g)
    t_mat = each(lambda x: jnp.where(eye, 1.0, x), n_mat)
    n_pow = n_mat
    lv = 2
    while lv < n:
        n_pow = each(lambda x: _bdot(x, x), n_pow)
        t_mat = each(lambda t_, p_: t_ + _bdot(t_, p_), t_mat, n_pow)
        lv *= 2
    tam = each(lambda t_, a_, m_: _bdot(t_, jnp.concatenate([a_, m_], axis=1)), t_mat, at, m_mat)
    qw = each(_bdot, pb, tam)
    q = each(lambda r_, x: r_ + x[:, :HEAD_R], rt, qw)
    w_loc = each(lambda x, p_, v_: _bdot(x[:, HEAD_R:] + p_, v_), qw, pk, v)
    ab = each(lambda x, b_: _bdot(x, b_, TN), tam, bh)
    a_tr = each(lambda l_, x: jnp.where(eye_k, l_, 0.0) + x[:HEAD_R], lam, ab)
    s_inc = each(lambda v_, k_, x: _bdot(v_, k_ + x[HEAD_R:], TN), v, kh, ab)
    return q, w_loc, a_tr, s_inc


def _split3(x):
    hi = x.astype(bf16)
    r1 = x - hi.astype(f32)
    mid = r1.astype(bf16)
    lo = (r1 - mid.astype(f32)).astype(bf16)
    return hi, mid, lo


def _dot3_left(w, x):
    return sum(jnp.dot(w, p, preferred_element_type=f32) for p in _split3(x))


def _dot3_right(x, w):
    return sum(jnp.dot(p, w, preferred_element_type=f32) for p in _split3(x))


def _rwkv_kernel(zr_ref, zk_ref, zv_ref, zs_ref, pr_ref, pk_ref, pv_ref, ps_ref,
                 mur_ref, muk_ref, muv_ref, mus_ref, w0_ref, a0_ref, kk_ref, ka_ref, rk_ref, lw_ref, lb_ref,
                 w2_ref, a2_ref, g2_ref, s0_ref,
                 y_ref, so_ref,
                 s_scr, prev_scr, at_scr, rt_scr, bt_scr, kt_scr, v_scr, bh_scr, kh_scr, lam_scr,
                 q_scr, wl_scr, y_scr, atr_scr, sinc_scr,
                 *, nb, tb_rows, chunk, hb, local):
    tb = pl.program_id(2)
    n_tb = pl.num_programs(2)
    rows = nb * tb_rows
    per_seq = tb_rows // chunk
    n_chunks = rows // chunk
    wc = hb * HEAD_R

    @pl.when(tb == 0)
    def _():
        if nb == 1:
            s_scr[...] = s0_ref[0]
            for slot, (p_ref, width) in enumerate(((pr_ref, wc), (pk_ref, wc), (pv_ref, wc), (ps_ref, SM_W))):
                prev_scr[slot, 0:1, 0:width] = p_ref[0:1, :]

    def shifted(z_ref, p_ref, slot, width):
        z = z_ref[...]
        rolled = pltpu.roll(z, 1, 0)
        row = lax.broadcasted_iota(i32, z.shape, 0)
        if nb == 1:
            zs = jnp.where(row == 0, prev_scr[slot, 0:1, 0:width], rolled)
            prev_scr[slot, 0:1, 0:width] = z[rows - 1:rows, :]
        else:
            zs = jnp.where(row % tb_rows == 0, p_ref[...], rolled)
        return z, zs

    def lerp(z_ref, p_ref, mu_ref, slot, width):
        z, zs = shifted(z_ref, p_ref, slot, width)
        return z + (zs - z) * mu_ref[...]

    r = lerp(zr_ref, pr_ref, mur_ref, 0, wc)
    k = lerp(zk_ref, pk_ref, muk_ref, 1, wc)
    v = lerp(zv_ref, pv_ref, muv_ref, 2, wc)
    sm = lerp(zs_ref, ps_ref, mus_ref, 3, SM_W)
    gl = sm[:, 0:G_LORA]
    wl = sm[:, G_LORA:G_LORA + 128]
    al = sm[:, G_LORA + 128:G_LORA + 256]
    w_raw = -_softplus(-(w0_ref[...] + _bdot(jnp.tanh(wl), w2_ref[...]))) - 0.5
    eta = jax.nn.sigmoid(a0_ref[...] + _bdot(al, a2_ref[...]))
    lw = -jnp.exp(w_raw)
    k2 = k * (1.0 + (eta - 1.0) * ka_ref[...])

    shift = chunk.bit_length() - 1
    ti = lax.broadcasted_iota(i32, (rows, rows), 0)
    si = lax.broadcasted_iota(i32, (rows, rows), 1)
    same_chunk = (ti >> shift) == (si >> shift)
    tri_sel = jnp.where(same_chunk & (si <= ti), 1.0, 0.0).astype(bf16)
    all_sel = jnp.where(same_chunk, 1.0, 0.0).astype(bf16)
    hshift = HEAD_R.bit_length() - 1
    head_sel = (lax.broadcasted_iota(i32, (wc, wc), 0) >> hshift) == (lax.broadcasted_iota(i32, (wc, wc), 1) >> hshift)
    head_sum = jnp.where(head_sel, 1.0, 0.0).astype(bf16)
    head_mean = jnp.where(head_sel, 1.0 / HEAD_R, 0.0).astype(bf16)

    c = k * kk_ref[...]
    c = c / jnp.maximum(jnp.sqrt(_dot3_right(c * c, head_sum)), 1e-12)
    b_vec = c * eta
    cum = _dot3_left(tri_sel, lw)
    tot = _dot3_left(all_sel, lw)
    e_out = jnp.exp(-cum)
    tail = jnp.exp(tot - cum)
    at_scr[...] = -c * jnp.exp(cum - lw)
    rt_scr[...] = r * jnp.exp(cum)
    bt_scr[...] = b_vec * e_out
    kt_scr[...] = k2 * e_out
    bh_scr[...] = b_vec * tail
    kh_scr[...] = k2 * tail
    lam_scr[...] = jnp.exp(tot)
    v_scr[...] = v
    bonus = _dot3_right(r * k2 * rk_ref[...], head_sum) * v
    gate = _bdot(jax.nn.sigmoid(gl), g2_ref[...])

    def local_body(it, carry):
        where = []
        for j in range(local):
            ci = it * local + j
            sl = pl.ds(pl.multiple_of(ci * chunk, chunk), chunk)
            where += [(ci, sl, h, slice(h * HEAD_R, (h + 1) * HEAD_R)) for h in range(hb)]
        load = lambda ref: [ref[sl, ls] for _, sl, _, ls in where]
        lam = [lam_scr[pl.ds(ci * chunk, 1), ls] for ci, _, _, ls in where]
        q, w_loc, a_tr, s_inc = _rwkv_local(load(at_scr), load(rt_scr), load(bt_scr), load(kt_scr), load(v_scr),
                                            load(bh_scr), load(kh_scr), lam)
        for i, (ci, sl, h, ls) in enumerate(where):
            q_scr[sl, ls] = q[i]
            wl_scr[sl, ls] = w_loc[i]
            atr_scr[ci, h] = a_tr[i]
            sinc_scr[ci, h] = s_inc[i]
        return carry

    lax.fori_loop(0, n_chunks // local, local_body, 0)

    def advance(ci, state):
        sl = pl.ds(ci * chunk if isinstance(ci, int) else pl.multiple_of(ci * chunk, chunk), chunk)
        ys, new_state = [], []
        for h in range(hb):
            ls = slice(h * HEAD_R, (h + 1) * HEAD_R)
            ys.append(_bdot(q_scr[sl, ls], state[h], NT) + wl_scr[sl, ls])
            new_state.append(_bdot(state[h], atr_scr[ci, h]) + sinc_scr[ci, h])
        for h in range(hb):
            y_scr[sl, h * HEAD_R:(h + 1) * HEAD_R] = ys[h]
        return new_state

    if nb == 1:
        state = [s_scr[0, h] for h in range(hb)]
        for ci in range(n_chunks):
            state = advance(ci, state)
        for h in range(hb):
            s_scr[0, h] = state[h]

        @pl.when(tb == n_tb - 1)
        def _():
            so_ref[...] = s_scr[...]
    else:
        def seq_body(it, carry):
            seqs = [it * local + j for j in range(local)]
            states = [[s0_ref[0, ci, h] for h in range(hb)] for ci in seqs]
            states = [advance(ci, st) for ci, st in zip(seqs, states)]
            for ci, st in zip(seqs, states):
                for h in range(hb):
                    so_ref[ci, h] = st[h]
            return carry

        lax.fori_loop(0, n_chunks // local, seq_body, 0)

    y = y_scr[...]
    dev = y - _dot3_right(y, head_mean)
    var = _dot3_right(dev * dev, head_mean)
    yn = dev * lax.rsqrt(var + LNX_EPS) * lw_ref[...] + lb_ref[...]
    y_ref[...] = ((yn + bonus) * gate).astype(y_ref.dtype)


def _rwkv_call(z, zprev_rows, s0, layer, p, *, nseq, seq_len, row0, nb, tb_rows, chunk, local):
    hb = 4
    wc = hb * HEAD_R
    rows = nb * tb_rows
    n_tb = seq_len // tb_rows
    assert nb == 1 or (tb_rows == chunk and n_tb == 1)
    rb0 = row0 // rows
    zrow = lambda sb, h, tb: rb0 + sb * n_tb + tb

    def zspec(col0, width):
        cb = col0 // width
        if width == wc:
            return pl.BlockSpec((rows, width), lambda sb, h, tb: (zrow(sb, h, tb), cb + h))
        return pl.BlockSpec((rows, width), lambda sb, h, tb: (zrow(sb, h, tb), cb))

    def pspec(col0, width):
        cb = col0 // width
        if width == wc:
            return pl.BlockSpec((nb * 8, width), lambda sb, h, tb: (sb, cb + h))
        return pl.BlockSpec((nb * 8, width), lambda sb, h, tb: (sb, cb))

    vec = lambda: pl.BlockSpec((1, wc), lambda sb, h, tb: (0, h))
    vec_at = lambda col0: pl.BlockSpec((1, wc), lambda sb, h, tb: (0, col0 // wc + h))
    lora = lambda kdim: pl.BlockSpec((kdim, wc), lambda sb, h, tb: (0, h))
    kern = functools.partial(_rwkv_kernel, nb=nb, tb_rows=tb_rows, chunk=chunk, hb=hb, local=local)
    scr = lambda: pltpu.VMEM((rows, wc), f32)
    return pl.pallas_call(
        kern,
        grid=(nseq // nb, NH_R // hb, n_tb),
        in_specs=[
            zspec(Z_RR, wc), zspec(Z_RK, wc), zspec(Z_RV, wc), zspec(Z_SM, SM_W),
            pspec(0, wc), pspec(D_RWKV, wc), pspec(2 * D_RWKV, wc), pspec(3 * D_RWKV, SM_W),
            vec_at(0), vec_at(D_RWKV), vec_at(2 * D_RWKV),
            pl.BlockSpec((1, SM_W), lambda sb, h, tb: (0, 3 * D_RWKV // SM_W)),
            vec(), vec(), vec(), vec(), vec(), vec(), vec(),
            lora(128), lora(128), lora(G_LORA),
            pl.BlockSpec((1, nb, hb, HEAD_R, HEAD_R), lambda sb, h, tb: (layer, sb, h, 0, 0)),
        ],
        out_specs=[
            pl.BlockSpec((rows, wc), lambda sb, h, tb: (sb * n_tb + tb, h)),
            pl.BlockSpec((nb, hb, HEAD_R, HEAD_R), lambda sb, h, tb: (sb, h, 0, 0)),
        ],
        out_shape=[
            jax.ShapeDtypeStruct((nseq * seq_len, D_RWKV), bf16),
            jax.ShapeDtypeStruct((nseq, NH_R, HEAD_R, HEAD_R), f32),
        ],
        scratch_shapes=[
            pltpu.VMEM((nb, hb, HEAD_R, HEAD_R), f32),
            pltpu.VMEM((4, 8, SM_W), f32),
            *[scr() for _ in range(11)],
            pltpu.VMEM((rows // chunk, hb, HEAD_R, HEAD_R), f32),
            pltpu.VMEM((rows // chunk, hb, HEAD_R, HEAD_R), f32),
        ],
        compiler_params=_cparams(("parallel", "parallel", "arbitrary")),
        name="rwkv",
    )(z, z, z, z, zprev_rows, zprev_rows, zprev_rows, zprev_rows,
      p["mu"], p["mu"], p["mu"], p["mu"], p["w0"], p["a0"], p["k_k"], p["k_a"], p["r_k"], p["lnx_w"], p["lnx_b"],
      p["w2"], p["a2"], p["g2"], s0)


def _mlstm_kernel(zq_ref, zk_ref, zv_ref, zo_ref, gate_ref, bias_ref, nw_ref, c0_ref, n0_ref, m0_ref,
                  y_ref, co_ref, no_ref, mo_ref,
                  c_scr, n_scr, m_scr, o_scr, *, nb, hm, tb_rows, chunk):
    tb = pl.program_id(2)
    n_tb = pl.num_programs(2)
    per_seq = tb_rows // chunk

    @pl.when(tb == 0)
    def _():
        c_scr[...] = c0_ref[0]
        n_scr[...] = n0_ref[0]
        m_scr[...] = m0_ref[0]

    incl, _, eye = _tri_masks(chunk)
    upper = lax.broadcasted_iota(i32, (chunk, chunk), 0) <= lax.broadcasted_iota(i32, (chunk, chunk), 1)
    pairs = [(s, h) for s in range(nb) for h in range(hm)]
    each = lambda f, *ls: [f(*xs) for xs in zip(*ls)]
    square = lambda row: jnp.broadcast_to(row, (chunk, chunk))
    row_sum = lambda x: jnp.sum(x, axis=-1, keepdims=True)

    def body(ci, carry):
        def load(ref):
            out = []
            for s, h in pairs:
                start = s * tb_rows + ci * chunk
                start = start if isinstance(start, int) else pl.multiple_of(start, chunk)
                out.append(ref[pl.ds(start, chunk), h * DK:(h + 1) * DK])
            return out

        q, v, o_gate = load(zq_ref), load(zv_ref), load(zo_ref)
        k = [x * (DK ** -0.5) for x in load(zk_ref)]
        gch = [gate_ref[h, s * per_seq + ci] for s, h in pairs]
        ic_row = [g[0:1, :] + bias_ref[h, 0:1, 0:1] for g, (s, h) in zip(gch, pairs)]
        fc_row = [-_softplus(-(g[1:2, :] + bias_ref[h, 1:2, 0:1])) for g, (s, h) in zip(gch, pairs)]
        c_mat = [c_scr[s, h] for s, h in pairs]
        n_row = [n_scr[s, h] for s, h in pairs]
        m_prev = [m_scr[s, h][:, 0:1] for s, h in pairs]
        f_col = each(lambda fr: row_sum(jnp.where(incl, square(fr), 0.0)), fc_row)
        fc_col = each(lambda fr: row_sum(jnp.where(eye, square(fr), 0.0)), fc_row)
        ic_col = each(lambda ir: row_sum(jnp.where(eye, square(ir), 0.0)), ic_row)
        f_row = each(lambda fc: jnp.sum(jnp.where(upper, square(fc), 0.0), axis=0, keepdims=True), fc_col)
        dm = each(lambda fc, fr, ir: jnp.where(incl, fc - fr + ir, -jnp.inf), f_col, f_row, ic_row)
        inter = each(lambda fc, m: fc + m, f_col, m_prev)
        m_t = each(lambda it_, d: jnp.maximum(it_, jnp.max(d, axis=-1, keepdims=True)), inter, dm)
        w_intra = each(lambda d, m: jnp.exp(d - m), dm, m_t)
        w_inter = each(lambda it_, m: jnp.exp(it_ - m), inter, m_t)
        s_mat = each(lambda q_, k_, w: _bdot(q_, k_, NT) * w, q, k, w_intra)
        qc = each(_bdot, q, c_mat)
        num = each(lambda s_, v_, w, x: _bdot(s_, v_) + w * x, s_mat, v, w_inter, qc)
        den = each(lambda s_, w, q_, n_: row_sum(s_) + w * row_sum(q_ * n_), s_mat, w_inter, q, n_row)
        hid = each(lambda a, d, m: a / jnp.maximum(jnp.abs(d), jnp.exp(-m)), num, den, m_t)
        m_new = each(lambda m: m[chunk - 1:chunk, :], m_t)
        w_s = each(lambda fc, ic, m: jnp.exp(fc[chunk - 1:chunk, :] - fc + ic - m), f_col, ic_col, m_new)
        w_c = each(lambda it_, m: jnp.exp(it_[chunk - 1:chunk, :] - m), inter, m_new)
        kw = each(lambda k_, w: k_ * w, k, w_s)
        c_new = each(lambda w, c, kw_, v_: w * c + _bdot(kw_, v_, TN), w_c, c_mat, kw, v)
        n_new = each(lambda w, n_, kw_: w * n_ + jnp.sum(kw_, axis=0, keepdims=True), w_c, n_row, kw)
        mean = each(lambda x: jnp.mean(x, axis=-1, keepdims=True), hid)
        var = each(lambda x, mu: jnp.mean(jnp.square(x - mu), axis=-1, keepdims=True), hid, mean)
        for i, (s, h) in enumerate(pairs):
            c_scr[s, h] = c_new[i]
            n_scr[s, h] = n_new[i]
            m_scr[s, h] = jnp.broadcast_to(m_new[i], (1, 128))
            hn = (hid[i] - mean[i]) * lax.rsqrt(var[i] + MH_EPS) * nw_ref[:, h * DK:(h + 1) * DK]
            start = s * tb_rows + ci * chunk
            start = start if isinstance(start, int) else pl.multiple_of(start, chunk)
            o_scr[pl.ds(start, chunk), h * DK:(h + 1) * DK] = hn * jax.nn.sigmoid(o_gate[i])
        return carry

    if per_seq == 1:
        body(0, 0)
    else:
        lax.fori_loop(0, per_seq, body, 0)
    y_ref[...] = o_scr[...].astype(y_ref.dtype)

    @pl.when(tb == n_tb - 1)
    def _():
        co_ref[...] = c_scr[...]
        no_ref[...] = n_scr[...]
        mo_ref[...] = m_scr[...]


def _mlstm_call(z, gates, bias, norm_w, c0, n0, m0, layer, *, nseq, seq_len, row0, nb, hm, tb_rows, chunk):
    rows = nb * tb_rows
    n_tb = seq_len // tb_rows
    rb0 = row0 // rows
    cpb = rows // chunk
    wm = hm * DK
    zspec = lambda col0: pl.BlockSpec((rows, wm), lambda sb, h, tb: (rb0 + sb * n_tb + tb, col0 // wm + h))
    st4 = lambda a, b: pl.BlockSpec((nb, hm, a, b), lambda sb, h, tb: (sb, h, 0, 0))
    st5 = lambda a, b: pl.BlockSpec((1, nb, hm, a, b), lambda sb, h, tb: (layer, sb, h, 0, 0))
    kern = functools.partial(_mlstm_kernel, nb=nb, hm=hm, tb_rows=tb_rows, chunk=chunk)
    return pl.pallas_call(
        kern,
        grid=(nseq // nb, NH_M // hm, n_tb),
        in_specs=[
            zspec(Z_MQ), zspec(Z_MK), zspec(Z_MV), zspec(Z_MO),
            pl.BlockSpec((hm, cpb, 2, chunk), lambda sb, h, tb: (h, sb * n_tb + tb, 0, 0)),
            pl.BlockSpec((hm, 2, 128), lambda sb, h, tb: (h, 0, 0)),
            pl.BlockSpec((1, wm), lambda sb, h, tb: (0, h)),
            st5(DK, DK), st5(1, DK), st5(1, 128),
        ],
        out_specs=[
            pl.BlockSpec((rows, wm), lambda sb, h, tb: (sb * n_tb + tb, h)),
            st4(DK, DK), st4(1, DK), st4(1, 128),
        ],
        out_shape=[
            jax.ShapeDtypeStruct((nseq * seq_len, D_MLSTM), bf16),
            jax.ShapeDtypeStruct((nseq, NH_M, DK, DK), f32),
            jax.ShapeDtypeStruct((nseq, NH_M, 1, DK), f32),
            jax.ShapeDtypeStruct((nseq, NH_M, 1, 128), f32),
        ],
        scratch_shapes=[
            pltpu.VMEM((nb, hm, DK, DK), f32),
            pltpu.VMEM((nb, hm, 1, DK), f32),
            pltpu.VMEM((nb, hm, 1, 128), f32),
            pltpu.VMEM((rows, wm), f32),
        ],
        compiler_params=_cparams(("parallel", "parallel", "arbitrary")),
        name="mlstm",
    )(z, z, z, z, gates, bias, norm_w, c0, n0, m0)


def _norm_route_kernel(x_ref, w_ref, scp_ref, scs_ref, shp_ref, shs_ref, rw_ref, rb_ref,
                       hn_ref, ei_ref, ew_ref, *, n_prompt_tiles):
    is_prompt = pl.program_id(0) < n_prompt_tiles
    y = _rms(x_ref[...], w_ref[...]).reshape(SEQ_PER_TILE, 8, D)
    sc = _pick_mod(is_prompt, scp_ref, scs_ref)
    sh = _pick_mod(is_prompt, shp_ref, shs_ref)
    hn = (y * (1.0 + sc) + sh).reshape(ROW_TILE, D)
    hn_ref[...] = hn
    logits = _bdot(hn, rw_ref[...]) + rb_ref[...]
    lane = lax.broadcasted_iota(i32, logits.shape, 1)
    lane_f = lane.astype(f32)
    valid = lane < N_EXPERTS
    lg = jnp.where(valid, logits, -1e30)
    ex = jnp.where(valid, jnp.exp(lg - jnp.max(lg, axis=-1, keepdims=True)), 0.0)
    probs = ex / jnp.sum(ex, axis=-1, keepdims=True)
    best = None
    for g in range(N_GROUPS):
        in_g = (lane >= g * EXPERTS_PER_GROUP) & (lane < (g + 1) * EXPERTS_PER_GROUP)
        pg = jnp.where(in_g, probs, -1.0)
        m1 = jnp.max(pg, axis=-1, keepdims=True)
        i1 = jnp.min(jnp.where(pg == m1, lane_f, 1e4), axis=-1, keepdims=True)
        pg2 = jnp.where(lane_f == i1, -1.0, pg)
        m2 = jnp.max(pg2, axis=-1, keepdims=True)
        i2 = jnp.min(jnp.where(pg2 == m2, lane_f, 1e4), axis=-1, keepdims=True)
        cand = (m1 + m2, m1, m2, i1, i2)
        if best is None:
            best = cand
        else:
            upd = cand[0] > best[0]
            best = tuple(jnp.where(upd, c, b) for c, b in zip(cand, best))
    _, m1, m2, i1, i2 = best
    tot = m1 + m2
    ei_ref[...] = jnp.where(lane == 0, i1, jnp.where(lane == 1, i2, 0.0)).astype(i32)
    ew_ref[...] = jnp.where(lane == 0, m1 / tot, jnp.where(lane == 1, m2 / tot, 0.0))


def _norm_route_call(x, w, sc_p, sc_s, sh_p, sh_s, rw, rb, prompt_len):
    m = x.shape[0]
    n_prompt_seq = sc_p.shape[0]
    npt = n_prompt_seq * prompt_len // ROW_TILE
    pspec, sspec = _mod_specs(npt, n_prompt_seq, prompt_len, D)
    row = lambda wd: pl.BlockSpec((ROW_TILE, wd), lambda i: (i, 0))
    return pl.pallas_call(
        functools.partial(_norm_route_kernel, n_prompt_tiles=npt),
        grid=(m // ROW_TILE,),
        in_specs=[row(D), pl.BlockSpec((1, D), lambda i: (0, 0)), pspec, sspec, pspec, sspec,
                  pl.BlockSpec((D, 128), lambda i: (0, 0)), pl.BlockSpec((1, 128), lambda i: (0, 0))],
        out_specs=[row(D), row(128), row(128)],
        out_shape=[jax.ShapeDtypeStruct((m, D), f32), jax.ShapeDtypeStruct((m, 128), i32),
                   jax.ShapeDtypeStruct((m, 128), f32)],
        compiler_params=_cparams(("parallel",)),
        name="norm2_route",
    )(x, w.reshape(1, D), sc_p, sc_s, sh_p, sh_s, rw, rb)


def _row_copy(src_hbm, row, dst, r, sem):
    return pltpu.make_async_copy(src_hbm.at[pl.ds(row, 1), :], dst.at[pl.ds(r, 1), :], sem)


def _gather_kernel(tok_ref, valid_ref, hn_hbm, o_ref, buf, sem):
    j = pl.program_id(0)
    n_j = pl.num_programs(0)

    def issue(block, slot):
        def one(r, c):
            _row_copy(hn_hbm, tok_ref[block * MOE_TM + r], buf.at[slot], r, sem.at[slot]).start()
            return c
        lax.fori_loop(0, MOE_TM, one, 0, unroll=8)

    def wait(slot):
        def one(r, c):
            _row_copy(hn_hbm, 0, buf.at[slot], r, sem.at[slot]).wait()
            return c
        lax.fori_loop(0, MOE_TM, one, 0, unroll=8)

    @pl.when((j == 0) & (valid_ref[0] == 1))
    def _():
        issue(0, 0)

    nxt = jnp.minimum(j + 1, n_j - 1)

    @pl.when((j + 1 < n_j) & (valid_ref[nxt] == 1))
    def _():
        issue(j + 1, (j + 1) % 2)

    @pl.when(valid_ref[j] == 1)
    def _():
        wait(j % 2)
        o_ref[...] = buf[j % 2].astype(o_ref.dtype)

    @pl.when(valid_ref[j] == 0)
    def _():
        o_ref[...] = jnp.zeros_like(o_ref)


def _gather_call(hn, slot_tok, valid):
    p = slot_tok.shape[0]
    return pl.pallas_call(
        _gather_kernel,
        grid_spec=pltpu.PrefetchScalarGridSpec(
            num_scalar_prefetch=2,
            grid=(p // MOE_TM,),
            in_specs=[pl.BlockSpec(memory_space=pl.ANY)],
            out_specs=pl.BlockSpec((MOE_TM, D), lambda j, tok, va: (j, 0)),
            scratch_shapes=[pltpu.VMEM((2, MOE_TM, D), f32), pltpu.SemaphoreType.DMA((2,))],
        ),
        out_shape=jax.ShapeDtypeStruct((p, D), bf16),
        compiler_params=_cparams(("arbitrary",)),
        name="moe_gather",
    )(slot_tok, valid, hn)


def _stream_expert_tiles(be_ref, first_ref, next_ref, w_hbm, wbuf, w_bf, sem, run_ref, *, layer, tn, col_tiles):
    n = pl.program_id(0)
    j = pl.program_id(1)
    n_pass = pl.num_programs(0)

    def copies(e, nn, slot):
        return [pltpu.make_async_copy(w_hbm.at[layer, e, :, pl.ds(pl.multiple_of(ct(nn) * tn, tn), tn)],
                                      wbuf.at[slot, c], sem.at[slot]) for c, ct in enumerate(col_tiles)]

    @pl.when((n == 0) & (j == 0))
    def _():
        run_ref[0] = 0
        for cp in copies(be_ref[0], 0, 0):
            cp.start()

    @pl.when(first_ref[j] == 1)
    def _():
        slot = run_ref[0] % 2
        for cp in copies(be_ref[j], n, slot):
            cp.wait()
        nxt = next_ref[j]

        @pl.when(nxt >= 0)
        def _():
            for cp in copies(nxt, n, 1 - slot):
                cp.start()

        @pl.when((nxt < 0) & (n + 1 < n_pass))
        def _():
            for cp in copies(be_ref[0], n + 1, 1 - slot):
                cp.start()

        for c in range(len(col_tiles)):
            w_bf[c] = wbuf[slot, c].astype(bf16)
        run_ref[0] = run_ref[0] + 1


def _moe1_kernel(be_ref, first_ref, next_ref, valid_ref, x_ref, w_hbm, act_ref, wbuf, w_bf, sem, run_ref,
                 *, layer, tn, n_t):
    j = pl.program_id(1)
    _stream_expert_tiles(be_ref, first_ref, next_ref, w_hbm, wbuf, w_bf, sem, run_ref, layer=layer, tn=tn,
                         col_tiles=(lambda n: n, lambda n: n_t + n))

    @pl.when(valid_ref[j] == 1)
    def _():
        x = x_ref[...]
        g = jnp.dot(x, w_bf[0], preferred_element_type=f32)
        u = jnp.dot(x, w_bf[1], preferred_element_type=f32)
        act_ref[...] = (g * jax.nn.sigmoid(g) * u).astype(act_ref.dtype)

    @pl.when(valid_ref[j] == 0)
    def _():
        act_ref[...] = jnp.zeros_like(act_ref)


def _moe1_call(xs, w_in, layer, block_e, first, nxt, valid):
    p = xs.shape[0]
    tn = 256
    n_t = D_EXPERT // tn
    return pl.pallas_call(
        functools.partial(_moe1_kernel, layer=layer, tn=tn, n_t=n_t),
        grid_spec=pltpu.PrefetchScalarGridSpec(
            num_scalar_prefetch=4,
            grid=(n_t, p // MOE_TM),
            in_specs=[
                pl.BlockSpec((MOE_TM, D), lambda n, j, *_: (j, 0)),
                pl.BlockSpec(memory_space=pl.ANY),
            ],
            out_specs=pl.BlockSpec((MOE_TM, tn), lambda n, j, *_: (j, n)),
            scratch_shapes=[pltpu.VMEM((2, 2, D, tn), f32), pltpu.VMEM((2, D, tn), bf16),
                            pltpu.SemaphoreType.DMA((2,)), pltpu.SMEM((1,), i32)],
        ),
        out_shape=jax.ShapeDtypeStruct((p, D_EXPERT), bf16),
        compiler_params=_cparams(("arbitrary", "arbitrary")),
        name="moe_up",
    )(block_e, first, nxt, valid, xs, w_in)


def _moe2_kernel(be_ref, first_ref, next_ref, valid_ref, a_ref, w_hbm, o_ref, wbuf, w_bf, sem, run_ref,
                 *, layer, tn):
    j = pl.program_id(1)
    _stream_expert_tiles(be_ref, first_ref, next_ref, w_hbm, wbuf, w_bf, sem, run_ref, layer=layer, tn=tn,
                         col_tiles=(lambda n: n,))

    @pl.when(valid_ref[j] == 1)
    def _():
        o_ref[...] = jnp.dot(a_ref[...], w_bf[0], preferred_element_type=f32)

    @pl.when(valid_ref[j] == 0)
    def _():
        o_ref[...] = jnp.zeros_like(o_ref)


def _moe2_call(act, w_out, layer, block_e, first, nxt, valid):
    p = act.shape[0]
    tn = 1024
    return pl.pallas_call(
        functools.partial(_moe2_kernel, layer=layer, tn=tn),
        grid_spec=pltpu.PrefetchScalarGridSpec(
            num_scalar_prefetch=4,
            grid=(D // tn, p // MOE_TM),
            in_specs=[
                pl.BlockSpec((MOE_TM, D_EXPERT), lambda n, j, *_: (j, 0)),
                pl.BlockSpec(memory_space=pl.ANY),
            ],
            out_specs=pl.BlockSpec((MOE_TM, tn), lambda n, j, *_: (j, n)),
            scratch_shapes=[pltpu.VMEM((2, 1, D_EXPERT, tn), f32), pltpu.VMEM((1, D_EXPERT, tn), bf16),
                            pltpu.SemaphoreType.DMA((2,)), pltpu.SMEM((1,), i32)],
        ),
        out_shape=jax.ShapeDtypeStruct((p, D), f32),
        compiler_params=_cparams(("arbitrary", "arbitrary")),
        name="moe_down",
    )(block_e, first, nxt, valid, act, w_out)


def _combine_kernel(pos_ref, x_ref, ew_ref, gp_ref, gs_ref, yb_hbm, o_ref, buf, sem, *, n_prompt_tiles):
    i = pl.program_id(0)
    n_i = pl.num_programs(0)

    def issue(tile, slot):
        def one(r, c):
            for k in range(2):
                _row_copy(yb_hbm, pos_ref[2 * (tile * ROW_TILE + r) + k], buf.at[slot, k], r, sem.at[slot]).start()
            return c
        lax.fori_loop(0, ROW_TILE, one, 0, unroll=4)

    def wait(slot):
        def one(r, c):
            for k in range(2):
                _row_copy(yb_hbm, 0, buf.at[slot, k], r, sem.at[slot]).wait()
            return c
        lax.fori_loop(0, ROW_TILE, one, 0, unroll=4)

    @pl.when(i == 0)
    def _():
        issue(0, 0)

    @pl.when(i + 1 < n_i)
    def _():
        issue(i + 1, (i + 1) % 2)

    wait(i % 2)
    ew = ew_ref[...]
    y = buf[i % 2, 0] * ew[:, 0:1] + buf[i % 2, 1] * ew[:, 1:2]
    g = _pick_mod(i < n_prompt_tiles, gp_ref, gs_ref)
    o_ref[...] = (x_ref[...].reshape(SEQ_PER_TILE, 8, D) + g * y.reshape(SEQ_PER_TILE, 8, D)).reshape(ROW_TILE, D)


def _combine_call(x, yb, pos, ew, g_p, g_s, prompt_len):
    m = x.shape[0]
    n_prompt_seq = g_p.shape[0]
    npt = n_prompt_seq * prompt_len // ROW_TILE
    tiles_per_seq = prompt_len // ROW_TILE
    return pl.pallas_call(
        functools.partial(_combine_kernel, n_prompt_tiles=npt),
        grid_spec=pltpu.PrefetchScalarGridSpec(
            num_scalar_prefetch=1,
            grid=(m // ROW_TILE,),
            in_specs=[
                pl.BlockSpec((ROW_TILE, D), lambda i, pos: (i, 0)),
                pl.BlockSpec((ROW_TILE, 128), lambda i, pos: (i, 0)),
                pl.BlockSpec((1, 1, D), lambda i, pos: (jnp.minimum(i // tiles_per_seq, n_prompt_seq - 1), 0, 0)),
                pl.BlockSpec((SEQ_PER_TILE, 1, D), lambda i, pos: (jnp.maximum(i - npt, 0), 0, 0)),
                pl.BlockSpec(memory_space=pl.ANY),
            ],
            out_specs=pl.BlockSpec((ROW_TILE, D), lambda i, pos: (i, 0)),
            scratch_shapes=[pltpu.VMEM((2, 2, ROW_TILE, D), f32), pltpu.SemaphoreType.DMA((2,))],
        ),
        out_shape=jax.ShapeDtypeStruct((m, D), f32),
        compiler_params=_cparams(("arbitrary",)),
        name="moe_combine",
    )(pos, x, ew, g_p, g_s, yb)


def _moe_plan(eidx, n_blocks):
    m = eidx.shape[0]
    a = 2 * m
    flat_e = eidx.reshape(a)
    order = jnp.argsort(flat_e).astype(i32)
    sorted_e = flat_e[order]
    counts = jnp.zeros((N_EXPERTS,), i32).at[flat_e].add(1)
    padded = (counts + MOE_TM - 1) // MOE_TM * MOE_TM
    start = jnp.cumsum(counts) - counts
    pad_end = jnp.cumsum(padded)
    pad_start = pad_end - padded
    dest = pad_start[sorted_e] + jnp.arange(a, dtype=i32) - start[sorted_e]
    slot_tok = jnp.zeros((n_blocks * MOE_TM,), i32).at[dest].set(order // 2)
    pos = jnp.zeros((a,), i32).at[order].set(dest)
    blk = jnp.arange(n_blocks, dtype=i32) * MOE_TM
    block_e = jnp.minimum(jnp.searchsorted(pad_end, blk, side="right"), N_EXPERTS - 1).astype(i32)
    valid = (blk < pad_end[-1]).astype(i32)
    first = jnp.concatenate([jnp.ones((1,), i32), (block_e[1:] != block_e[:-1]).astype(i32)]) * valid
    first_at = jnp.where(first == 1, jnp.arange(n_blocks, dtype=i32), n_blocks)
    later = jnp.concatenate([lax.cummin(first_at[::-1])[::-1][1:], jnp.full((1,), n_blocks, i32)])
    nxt = jnp.where(later < n_blocks, block_e[jnp.minimum(later, n_blocks - 1)], -1).astype(i32)
    return slot_tok, pos, block_e, first, nxt, valid


def _prep_w_in(w):
    wt = jnp.swapaxes(w, 0, 1)
    zeros = lambda n: jnp.zeros((n, wt.shape[1]), wt.dtype)
    mb = RWKV_COLS
    gb = RWKV_COLS + MLSTM_COLS
    parts = [
        wt[0:2048], wt[2144:4192], wt[4192:6240],
        wt[mb:mb + 2048], wt[mb + 2048:mb + 4096], wt[mb + 4096:mb + 6144], wt[mb + 6160:mb + 8208],
        wt[gb:gb + 8192],
        wt[6336:6592], wt[2048:2144], zeros(32), wt[6240:6336], zeros(32),
        wt[mb + 6144:mb + 6160], zeros(112), zeros(NZ - Z_IF - 128),
    ]
    return jnp.concatenate(parts, axis=0).astype(bf16)


def _prep_mu(mu):
    zeros = jnp.zeros((32,), mu.dtype)
    return jnp.concatenate([mu[0:2048], mu[2144:4192], mu[4192:6240], mu[6336:6592], mu[2048:2144], zeros,
                            mu[6240:6336], zeros]).reshape(1, ZP_COLS)


def _pad_rows(w, rows):
    return jnp.concatenate([w, jnp.zeros((rows - w.shape[0], w.shape[1]), w.dtype)], axis=0).astype(bf16)


def _gate_rows(z, row0, rows, chunk):
    zif = z[row0:row0 + rows, Z_IF:Z_IF + 2 * NH_M].reshape(rows // chunk, chunk, 2, NH_M)
    return jnp.transpose(zif, (3, 0, 2, 1))


def kernel(x_prompt, x_sample, c_prompt, c_sample, state_wkv, state_shift, state_C, state_n, state_m, ada_w, ada_b, norm1_w, norm2_w, w_in, rwkv_mu, rwkv_w0, rwkv_w2, rwkv_a0, rwkv_a2, rwkv_g2, rwkv_k_k, rwkv_k_a, rwkv_r_k, rwkv_lnx_w, rwkv_lnx_b, mlstm_i_b, mlstm_f_b, mlstm_norm_w, w_branch_r, w_branch_m, w_out, router_w, router_b, moe_w_in, moe_w_out, final_norm_w):
    bp, tp, _ = x_prompt.shape
    bs, ts, _ = x_sample.shape
    mp, ms = bp * tp, bs * ts
    m = mp + ms
    assert ts == 8 and tp % ROW_TILE == 0 and ms % ROW_TILE == 0 and bs % SEQ_PER_TILE == 0 and m % 512 == 0
    n_seq = bp + bs
    n_seq_pad = -(-n_seq // 8) * 8

    x = jnp.concatenate([x_prompt.reshape(mp, D), x_sample.reshape(ms, D)], axis=0)
    c_all = jnp.concatenate([c_prompt, c_sample, jnp.zeros((n_seq_pad - n_seq, D), f32)], axis=0)
    ada = _ada_call(c_all, ada_w, ada_b).reshape(DEPTH, n_seq_pad, N_ADA, D)
    last_rows = jnp.concatenate([jnp.arange(bp, dtype=i32) * tp + tp - 1,
                                 mp + jnp.arange(bs, dtype=i32) * ts + ts - 1,
                                 jnp.zeros((n_seq_pad - n_seq,), i32)])
    router_w_p = jnp.concatenate([router_w, jnp.zeros((D, 128 - N_EXPERTS), f32)], axis=1).astype(bf16)
    router_b_p = jnp.concatenate([router_b, jnp.zeros((128 - N_EXPERTS,), f32)]).reshape(1, 128)
    n_blocks = -(-(2 * m + N_EXPERTS * (MOE_TM - 1)) // MOE_TM)

    prompt_geo = dict(nseq=bp, seq_len=tp, row0=0, nb=1, tb_rows=256, chunk=64)
    sample_geo = dict(nseq=bs, seq_len=ts, row0=mp, tb_rows=8, chunk=8)
    rwkv_p, rwkv_s = dict(prompt_geo, local=4), dict(sample_geo, nb=16, local=4)
    mlstm_p, mlstm_s = dict(prompt_geo, hm=4), dict(sample_geo, nb=4, hm=4)

    outs_p = [[] for _ in range(5)]
    outs_s = [[] for _ in range(5)]
    for l in range(DEPTH):
        mod = lambda idx: (ada[l, :bp, idx][:, None, :], ada[l, bp:n_seq, idx][:, None, :])
        sh1, sc1, g1, sh2, sc2, g2 = (mod(i) for i in range(N_ADA))
        wz = _prep_w_in(w_in[l])

        xn = _norm_mod_call(x, norm1_w[l], sc1[0], sc1[1], sh1[0], sh1[1], tp)
        shift_rows = _norm_rows_call(x[last_rows], norm1_w[l], ada[l, :, 1], ada[l, :, 0])
        z = _matmul_call(xn, wz, 512, 1024, name="w_in")
        zprev_s = _matmul_call(state_shift[l].astype(bf16), wz, bs, 512, n_out=ZP_COLS,
                               col_map=lambda j: jnp.where(j < 12, j, Z_SM // 512), name="w_in_prev")
        rp = dict(mu=_prep_mu(rwkv_mu[l]), w0=rwkv_w0[l].reshape(1, -1), a0=rwkv_a0[l].reshape(1, -1),
                  k_k=rwkv_k_k[l].reshape(1, -1), k_a=rwkv_k_a[l].reshape(1, -1), r_k=rwkv_r_k[l].reshape(1, -1),
                  lnx_w=rwkv_lnx_w[l].reshape(1, -1), lnx_b=rwkv_lnx_b[l].reshape(1, -1),
                  w2=_pad_rows(rwkv_w2[l], 128), a2=_pad_rows(rwkv_a2[l], 128), g2=rwkv_g2[l].astype(bf16))
        yr_p, wkv_p = _rwkv_call(z, jnp.zeros((bp * 8, ZP_COLS), f32),
                                 jnp.zeros((1, bp, NH_R, HEAD_R, HEAD_R), f32), 0, rp, **rwkv_p)
        yr_s, wkv_s = _rwkv_call(z, jnp.repeat(zprev_s, 8, axis=0), state_wkv, l, rp, **rwkv_s)
        bias = jnp.broadcast_to(jnp.stack([mlstm_i_b[l], mlstm_f_b[l]], axis=1)[:, :, None], (NH_M, 2, 128))
        nw = mlstm_norm_w[l].reshape(1, -1)
        ym_p, c_p, n_p, m_p = _mlstm_call(
            z, _gate_rows(z, 0, mp, 64), bias, nw, jnp.zeros((1, bp, NH_M, DK, DK), f32),
            jnp.zeros((1, bp, NH_M, 1, DK), f32), jnp.zeros((1, bp, NH_M, 1, 128), f32), 0, **mlstm_p)
        ym_s, c_s, n_s, m_s = _mlstm_call(
            z, _gate_rows(z, mp, ms, 8), bias, nw, state_C, state_n[:, :, :, None, :],
            jnp.broadcast_to(state_m[:, :, :, None, None], (DEPTH, bs, NH_M, 1, 128)), l, **mlstm_s)
        merged = _merge_call(jnp.concatenate([yr_p, yr_s], axis=0), jnp.concatenate([ym_p, ym_s], axis=0),
                             w_branch_r[l].astype(bf16), w_branch_m[l].astype(bf16), z)
        x = _wout_call(merged, w_out[l].astype(bf16), x, g1[0], g1[1], tp)

        hn, eidx, ew = _norm_route_call(x, norm2_w[l], sc2[0], sc2[1], sh2[0], sh2[1], router_w_p, router_b_p, tp)
        slot_tok, pos, block_e, first, nxt, valid = _moe_plan(eidx[:, :2], n_blocks)
        xs = _gather_call(hn, slot_tok, valid)
        act = _moe1_call(xs, moe_w_in, l, block_e, first, nxt, valid)
        yb = _moe2_call(act, moe_w_out, l, block_e, first, nxt, valid)
        x = _combine_call(x, yb, pos, ew, g2[0], g2[1], tp)

        for acc, vals in ((outs_p, (wkv_p, shift_rows[:bp], c_p, n_p[:, :, 0], m_p[:, :, 0, 0])),
                          (outs_s, (wkv_s, shift_rows[bp:n_seq], c_s, n_s[:, :, 0], m_s[:, :, 0, 0]))):
            for lst, v in zip(acc, vals):
                lst.append(v)

    y_prompt = _final_norm_call(x, final_norm_w, 0, mp).reshape(bp, tp, D)
    y_sample = _final_norm_call(x, final_norm_w, mp, ms).reshape(bs, ts, D)
    return (y_prompt, y_sample, *(jnp.stack(v) for v in outs_p), *(jnp.stack(v) for v in outs_s))
```

```python
import functools
import math

import jax
import jax.numpy as jnp
from jax import lax
from jax.experimental import pallas as pl
from jax.experimental.pallas import tpu as pltpu

f32 = jnp.float32
bf16 = jnp.bfloat16
i32 = jnp.int32

D = 4096
DEPTH = 2
D_RWKV = 2048
HEAD_R = 64
NH_R = D_RWKV // HEAD_R
W_LORA = 96
A_LORA = 96
G_LORA = 256
LNX_EPS = 64e-5
D_MLSTM = 2048
NH_M = 8
DK = D_MLSTM // NH_M
MH_EPS = 1e-6
RWKV_COLS = 3 * D_RWKV + W_LORA + A_LORA + G_LORA
MLSTM_COLS = 4 * D_MLSTM + 2 * NH_M
N_EXPERTS = 32
EXPERTS_PER_GROUP = 8
N_GROUPS = 4
D_EXPERT = D // 4
N_ADA = 6
RMS_EPS = 1e-6

Z_RR, Z_RK, Z_RV = 0, 2048, 4096
Z_MQ, Z_MK, Z_MV, Z_MO = 6144, 8192, 10240, 12288
Z_GR, Z_GM = 14336, 18432
Z_SM = 22528
Z_IF = 23040
NZ = 23552
SM_W = 512
ZP_COLS = 3 * D_RWKV + SM_W

ROW_TILE = 256
SEQ_PER_TILE = 32
MOE_TM = 256
VMEM_LIMIT = 48 * 1024 * 1024
MOE_UP_VMEM = 58 * 1024 * 1024

NN = (((1,), (0,)), ((), ()))
NT = (((1,), (1,)), ((), ()))
TN = (((0,), (0,)), ((), ()))


def _cparams(sem, vmem=VMEM_LIMIT):
    return pltpu.CompilerParams(dimension_semantics=sem, vmem_limit_bytes=vmem)


def _bdot(a, b, dims=NN):
    return lax.dot_general(a.astype(bf16), b.astype(bf16), dims, preferred_element_type=f32)


def _softplus(x):
    return jnp.maximum(x, 0.0) + jnp.log1p(jnp.exp(-jnp.abs(x)))


def _ada_kernel(c_ref, w_ref, b_ref, o_ref):
    c = c_ref[...]
    a = c * jax.nn.sigmoid(c)
    o_ref[0] = _bdot(a, w_ref[0]) + b_ref[0]


def _ada_call(c_all, ada_w, ada_b):
    nb = c_all.shape[0]
    n = ada_w.shape[2]
    tn = 512
    return pl.pallas_call(
        _ada_kernel,
        grid=(DEPTH, n // tn),
        in_specs=[
            pl.BlockSpec((nb, D), lambda l, j: (0, 0)),
            pl.BlockSpec((1, D, tn), lambda l, j: (l, 0, j)),
            pl.BlockSpec((1, 1, tn), lambda l, j: (l, 0, j)),
        ],
        out_specs=pl.BlockSpec((1, nb, tn), lambda l, j: (l, 0, j)),
        out_shape=jax.ShapeDtypeStruct((DEPTH, nb, n), f32),
        compiler_params=_cparams(("parallel", "parallel")),
        name="ada",
    )(c_all, ada_w, ada_b.reshape(DEPTH, 1, n))


def _mod_specs(n_prompt_tiles, n_prompt_seq, prompt_len, width, col=None):
    tiles_per_seq = prompt_len // ROW_TILE
    if col is None:
        pmap = lambda i: (jnp.minimum(i // tiles_per_seq, n_prompt_seq - 1), 0, 0)
        smap = lambda i: (jnp.maximum(i - n_prompt_tiles, 0), 0, 0)
    else:
        pmap = lambda j, i: (jnp.minimum(i // tiles_per_seq, n_prompt_seq - 1), 0, j)
        smap = lambda j, i: (jnp.maximum(i - n_prompt_tiles, 0), 0, j)
    return pl.BlockSpec((1, 1, width), pmap), pl.BlockSpec((SEQ_PER_TILE, 1, width), smap)


def _pick_mod(is_prompt, p_ref, s_ref):
    return jnp.where(is_prompt, jnp.broadcast_to(p_ref[...], s_ref.shape), s_ref[...])


def _rms(x, w):
    return x * lax.rsqrt(jnp.mean(x * x, axis=-1, keepdims=True) + RMS_EPS) * w


def _norm_mod_kernel(x_ref, w_ref, scp_ref, scs_ref, shp_ref, shs_ref, o_ref, *, n_prompt_tiles):
    is_prompt = pl.program_id(0) < n_prompt_tiles
    y = _rms(x_ref[...], w_ref[...]).reshape(SEQ_PER_TILE, 8, D)
    sc = _pick_mod(is_prompt, scp_ref, scs_ref)
    sh = _pick_mod(is_prompt, shp_ref, shs_ref)
    o_ref[...] = (y * (1.0 + sc) + sh).reshape(ROW_TILE, D).astype(o_ref.dtype)


def _norm_mod_call(x, w, sc_p, sc_s, sh_p, sh_s, prompt_len):
    m = x.shape[0]
    n_prompt_seq = sc_p.shape[0]
    npt = n_prompt_seq * prompt_len // ROW_TILE
    pspec, sspec = _mod_specs(npt, n_prompt_seq, prompt_len, D)
    return pl.pallas_call(
        functools.partial(_norm_mod_kernel, n_prompt_tiles=npt),
        grid=(m // ROW_TILE,),
        in_specs=[pl.BlockSpec((ROW_TILE, D), lambda i: (i, 0)), pl.BlockSpec((1, D), lambda i: (0, 0)),
                  pspec, sspec, pspec, sspec],
        out_specs=pl.BlockSpec((ROW_TILE, D), lambda i: (i, 0)),
        out_shape=jax.ShapeDtypeStruct((m, D), bf16),
        compiler_params=_cparams(("parallel",)),
        name="norm1",
    )(x, w.reshape(1, D), sc_p, sc_s, sh_p, sh_s)


def _norm_rows_kernel(x_ref, w_ref, sc_ref, sh_ref, o_ref):
    o_ref[...] = _rms(x_ref[...], w_ref[...]) * (1.0 + sc_ref[...]) + sh_ref[...]


def _norm_rows_call(x_rows, w, sc_rows, sh_rows):
    return pl.pallas_call(
        _norm_rows_kernel,
        out_shape=jax.ShapeDtypeStruct(x_rows.shape, f32),
        name="norm_rows",
    )(x_rows, w.reshape(1, D), sc_rows, sh_rows)


def _final_norm_kernel(x_ref, w_ref, o_ref):
    o_ref[...] = _rms(x_ref[...], w_ref[...])


def _final_norm_call(x, w, row0, rows):
    off = row0 // ROW_TILE
    return pl.pallas_call(
        _final_norm_kernel,
        grid=(rows // ROW_TILE,),
        in_specs=[pl.BlockSpec((ROW_TILE, D), lambda i: (i + off, 0)), pl.BlockSpec((1, D), lambda i: (0, 0))],
        out_specs=pl.BlockSpec((ROW_TILE, D), lambda i: (i, 0)),
        out_shape=jax.ShapeDtypeStruct((rows, D), f32),
        compiler_params=_cparams(("parallel",)),
        name="final_norm",
    )(x, w.reshape(1, D))


def _mm_kernel(a_ref, wt_ref, o_ref):
    o_ref[...] = lax.dot_general(a_ref[...], wt_ref[...], NT, preferred_element_type=f32)


def _matmul_call(a, wt, tm, tn, n_out=None, col_map=None, name="matmul"):
    m, k = a.shape
    n_out = wt.shape[0] if n_out is None else n_out
    wmap = (lambda j, i: (j, 0)) if col_map is None else (lambda j, i: (col_map(j), 0))
    return pl.pallas_call(
        _mm_kernel,
        grid=(n_out // tn, m // tm),
        in_specs=[pl.BlockSpec((tm, k), lambda j, i: (i, 0)), pl.BlockSpec((tn, k), wmap)],
        out_specs=pl.BlockSpec((tm, tn), lambda j, i: (i, j)),
        out_shape=jax.ShapeDtypeStruct((m, n_out), f32),
        compiler_params=_cparams(("parallel", "parallel")),
        name=name,
    )(a, wt)


def _merge_kernel(yr_ref, ym_ref, wr_ref, wm_ref, gr_ref, gm_ref, o_ref):
    pr = jnp.dot(yr_ref[...], wr_ref[...], preferred_element_type=f32)
    pm = jnp.dot(ym_ref[...], wm_ref[...], preferred_element_type=f32)
    o_ref[...] = (jax.nn.sigmoid(gr_ref[...]) * pr + jax.nn.sigmoid(gm_ref[...]) * pm).astype(o_ref.dtype)


def _merge_call(yr, ym, wr, wm, z):
    m = yr.shape[0]
    tm, tn = 512, 1024
    return pl.pallas_call(
        _merge_kernel,
        grid=(D // tn, m // tm),
        in_specs=[
            pl.BlockSpec((tm, D_RWKV), lambda j, i: (i, 0)),
            pl.BlockSpec((tm, D_MLSTM), lambda j, i: (i, 0)),
            pl.BlockSpec((D_RWKV, tn), lambda j, i: (0, j)),
            pl.BlockSpec((D_MLSTM, tn), lambda j, i: (0, j)),
            pl.BlockSpec((tm, tn), lambda j, i: (i, Z_GR // tn + j)),
            pl.BlockSpec((tm, tn), lambda j, i: (i, Z_GM // tn + j)),
        ],
        out_specs=pl.BlockSpec((tm, tn), lambda j, i: (i, j)),
        out_shape=jax.ShapeDtypeStruct((m, D), bf16),
        compiler_params=_cparams(("parallel", "parallel")),
        name="merge",
    )(yr, ym, wr, wm, z, z)


def _wout_kernel(a_ref, w_ref, x_ref, gp_ref, gs_ref, o_ref, *, n_prompt_tiles):
    is_prompt = pl.program_id(1) < n_prompt_tiles
    tn = o_ref.shape[1]
    acc = jnp.dot(a_ref[...], w_ref[...], preferred_element_type=f32).reshape(SEQ_PER_TILE, 8, tn)
    g = _pick_mod(is_prompt, gp_ref, gs_ref)
    o_ref[...] = (x_ref[...].reshape(SEQ_PER_TILE, 8, tn) + g * acc).reshape(ROW_TILE, tn)


def _wout_call(merged, w, x, g_p, g_s, prompt_len):
    m = x.shape[0]
    tn = 1024
    n_prompt_seq = g_p.shape[0]
    npt = n_prompt_seq * prompt_len // ROW_TILE
    pspec, sspec = _mod_specs(npt, n_prompt_seq, prompt_len, tn, col=True)
    return pl.pallas_call(
        functools.partial(_wout_kernel, n_prompt_tiles=npt),
        grid=(D // tn, m // ROW_TILE),
        in_specs=[
            pl.BlockSpec((ROW_TILE, D), lambda j, i: (i, 0)),
            pl.BlockSpec((D, tn), lambda j, i: (0, j)),
            pl.BlockSpec((ROW_TILE, tn), lambda j, i: (i, j)),
            pspec, sspec,
        ],
        out_specs=pl.BlockSpec((ROW_TILE, tn), lambda j, i: (i, j)),
        out_shape=jax.ShapeDtypeStruct((m, D), f32),
        compiler_params=_cparams(("parallel", "parallel")),
        name="wout",
    )(merged, w, x, g_p, g_s)


def _tri_masks(n):
    t = lax.broadcasted_iota(i32, (n, n), 0)
    s = lax.broadcasted_iota(i32, (n, n), 1)
    return s <= t, s < t, s == t


def _rwkv_local(at, rt, bt, kt, v, bh, kh, lam):
    n = at[0].shape[0]
    incl, strict, eye = _tri_masks(n)
    eye_k = _tri_masks(HEAD_R)[2]
    each = lambda f, *ls: [f(*xs) for xs in zip(*ls)]
    g = each(lambda a_, r_, b_, k_: _bdot(jnp.concatenate([a_, r_], axis=0), jnp.concatenate([b_, k_], axis=0), NT),
             at, rt, bt, kt)
    n_mat = each(lambda x: jnp.where(strict, x[:n, :n], 0.0), g)
    m_mat = each(lambda x: jnp.where(strict, x[:n, n:], 0.0), g)
    pb = each(lambda x: jnp.where(incl, x[n:, :n], 0.0), g)
    pk = each(lambda x: jnp.where(incl, x[n:, n:], 0.0), g)
    t_mat = each(lambda x: jnp.where(eye, 1.0, x), n_mat)
    n_pow = n_mat
    lv = 2
    while lv < n:
        n_pow = each(lambda x: _bdot(x, x), n_pow)
        t_mat = each(lambda t_, p_: t_ + _bdot(t_, p_), t_mat, n_pow)
        lv *= 2
    tam = each(lambda t_, a_, m_: _bdot(t_, jnp.concatenate([a_, m_], axis=1)), t_mat, at, m_mat)
    qw = each(_bdot, pb, tam)
    q = each(lambda r_, x: r_ + x[:, :HEAD_R], rt, qw)
    w_loc = each(lambda x, p_, v_: _bdot(x[:, HEAD_R:] + p_, v_), qw, pk, v)
    ab = each(lambda x, b_: _bdot(x, b_, TN), tam, bh)
    a_tr = each(lambda l_, x: jnp.where(eye_k, l_, 0.0) + x[:HEAD_R], lam, ab)
    s_inc = each(lambda v_, k_, x: _bdot(v_, k_ + x[HEAD_R:], TN), v, kh, ab)
    return q, w_loc, a_tr, s_inc


def _split3(x):
    hi = x.astype(bf16)
    r1 = x - hi.astype(f32)
    mid = r1.astype(bf16)
    lo = (r1 - mid.astype(f32)).astype(bf16)
    return hi, mid, lo


def _dot3_left(w, x):
    return sum(jnp.dot(w, p, preferred_element_type=f32) for p in _split3(x))


def _dot3_right(x, w):
    return sum(jnp.dot(p, w, preferred_element_type=f32) for p in _split3(x))


def _rwkv_kernel(zr_ref, zk_ref, zv_ref, zs_ref, pr_ref, pk_ref, pv_ref, ps_ref,
                 mur_ref, muk_ref, muv_ref, mus_ref, w0_ref, a0_ref, kk_ref, ka_ref, rk_ref, lw_ref, lb_ref,
                 w2_ref, a2_ref, g2_ref, s0_ref,
                 y_ref, so_ref,
                 s_scr, prev_scr, at_scr, rt_scr, bt_scr, kt_scr, v_scr, bh_scr, kh_scr, lam_scr,
                 q_scr, wl_scr, y_scr, atr_scr, sinc_scr,
                 *, nb, tb_rows, chunk, hb, local):
    tb = pl.program_id(2)
    n_tb = pl.num_programs(2)
    rows = nb * tb_rows
    per_seq = tb_rows // chunk
    n_chunks = rows // chunk
    wc = hb * HEAD_R

    @pl.when(tb == 0)
    def _():
        if nb == 1:
            s_scr[...] = s0_ref[0]
            for slot, (p_ref, width) in enumerate(((pr_ref, wc), (pk_ref, wc), (pv_ref, wc), (ps_ref, SM_W))):
                prev_scr[slot, 0:1, 0:width] = p_ref[0:1, :]

    def shifted(z_ref, p_ref, slot, width):
        z = z_ref[...]
        rolled = pltpu.roll(z, 1, 0)
        row = lax.broadcasted_iota(i32, z.shape, 0)
        if nb == 1:
            zs = jnp.where(row == 0, prev_scr[slot, 0:1, 0:width], rolled)
            prev_scr[slot, 0:1, 0:width] = z[rows - 1:rows, :]
        else:
            zs = jnp.where(row % tb_rows == 0, p_ref[...], rolled)
        return z, zs

    def lerp(z_ref, p_ref, mu_ref, slot, width):
        z, zs = shifted(z_ref, p_ref, slot, width)
        return z + (zs - z) * mu_ref[...]

    r = lerp(zr_ref, pr_ref, mur_ref, 0, wc)
    k = lerp(zk_ref, pk_ref, muk_ref, 1, wc)
    v = lerp(zv_ref, pv_ref, muv_ref, 2, wc)
    sm = lerp(zs_ref, ps_ref, mus_ref, 3, SM_W)
    gl = sm[:, 0:G_LORA]
    wl = sm[:, G_LORA:G_LORA + 128]
    al = sm[:, G_LORA + 128:G_LORA + 256]
    w_raw = -_softplus(-(w0_ref[...] + _bdot(jnp.tanh(wl), w2_ref[...]))) - 0.5
    eta = jax.nn.sigmoid(a0_ref[...] + _bdot(al, a2_ref[...]))
    lw = -jnp.exp(w_raw)
    k2 = k * (1.0 + (eta - 1.0) * ka_ref[...])

    shift = chunk.bit_length() - 1
    ti = lax.broadcasted_iota(i32, (rows, rows), 0)
    si = lax.broadcasted_iota(i32, (rows, rows), 1)
    same_chunk = (ti >> shift) == (si >> shift)
    tri_sel = jnp.where(same_chunk & (si <= ti), 1.0, 0.0).astype(bf16)
    hshift = HEAD_R.bit_length() - 1
    head_sel = (lax.broadcasted_iota(i32, (wc, wc), 0) >> hshift) == (lax.broadcasted_iota(i32, (wc, wc), 1) >> hshift)
    head_sum = jnp.where(head_sel, 1.0, 0.0).astype(bf16)
    head_mean = jnp.where(head_sel, 1.0 / HEAD_R, 0.0).astype(bf16)

    c = k * kk_ref[...]
    c = c / jnp.maximum(jnp.sqrt(_dot3_right(c * c, head_sum)), 1e-12)
    b_vec = c * eta
    cum = _dot3_left(tri_sel, lw)
    tot = jnp.concatenate([jnp.broadcast_to(cum[(ci + 1) * chunk - 1:(ci + 1) * chunk, :], (chunk, wc))
                           for ci in range(n_chunks)], axis=0)
    e_out = jnp.exp(-cum)
    tail = jnp.exp(tot - cum)
    at_scr[...] = -c * jnp.exp(cum - lw)
    rt_scr[...] = r * jnp.exp(cum)
    bt_scr[...] = b_vec * e_out
    kt_scr[...] = k2 * e_out
    bh_scr[...] = b_vec * tail
    kh_scr[...] = k2 * tail
    lam_scr[...] = jnp.exp(tot)
    v_scr[...] = v
    bonus = _dot3_right(r * k2 * rk_ref[...], head_sum) * v
    gate = _bdot(jax.nn.sigmoid(gl), g2_ref[...])

    def local_body(it, carry):
        where = []
        for j in range(local):
            ci = it * local + j
            sl = pl.ds(pl.multiple_of(ci * chunk, chunk), chunk)
            where += [(ci, sl, h, slice(h * HEAD_R, (h + 1) * HEAD_R)) for h in range(hb)]
        load = lambda ref: [ref[sl, ls] for _, sl, _, ls in where]
        lam = [lam_scr[pl.ds(ci * chunk, 1), ls] for ci, _, _, ls in where]
        q, w_loc, a_tr, s_inc = _rwkv_local(load(at_scr), load(rt_scr), load(bt_scr), load(kt_scr), load(v_scr),
                                            load(bh_scr), load(kh_scr), lam)
        for i, (ci, sl, h, ls) in enumerate(where):
            q_scr[sl, ls] = q[i]
            wl_scr[sl, ls] = w_loc[i]
            atr_scr[ci, h] = a_tr[i]
            sinc_scr[ci, h] = s_inc[i]
        return carry

    lax.fori_loop(0, n_chunks // local, local_body, 0)

    def advance(ci, state):
        sl = pl.ds(ci * chunk if isinstance(ci, int) else pl.multiple_of(ci * chunk, chunk), chunk)
        ys, new_state = [], []
        for h in range(hb):
            ls = slice(h * HEAD_R, (h + 1) * HEAD_R)
            ys.append(_bdot(q_scr[sl, ls], state[h], NT) + wl_scr[sl, ls])
            new_state.append(_bdot(state[h], atr_scr[ci, h]) + sinc_scr[ci, h])
        for h in range(hb):
            y_scr[sl, h * HEAD_R:(h + 1) * HEAD_R] = ys[h]
        return new_state

    if nb == 1:
        state = [s_scr[0, h] for h in range(hb)]
        for ci in range(n_chunks):
            state = advance(ci, state)
        for h in range(hb):
            s_scr[0, h] = state[h]

        @pl.when(tb == n_tb - 1)
        def _():
            so_ref[...] = s_scr[...]
    else:
        def seq_body(it, carry):
            seqs = [it * local + j for j in range(local)]
            states = [[s0_ref[0, ci, h] for h in range(hb)] for ci in seqs]
            states = [advance(ci, st) for ci, st in zip(seqs, states)]
            for ci, st in zip(seqs, states):
                for h in range(hb):
                    so_ref[ci, h] = st[h]
            return carry

        lax.fori_loop(0, n_chunks // local, seq_body, 0)

    y = y_scr[...]
    dev = y - _dot3_right(y, head_mean)
    var = _dot3_right(dev * dev, head_mean)
    yn = dev * lax.rsqrt(var + LNX_EPS) * lw_ref[...] + lb_ref[...]
    y_ref[...] = ((yn + bonus) * gate).astype(y_ref.dtype)


def _rwkv_call(z, zprev_rows, s0, layer, p, *, nseq, seq_len, row0, nb, tb_rows, chunk, local):
    hb = 4
    wc = hb * HEAD_R
    rows = nb * tb_rows
    n_tb = seq_len // tb_rows
    assert nb == 1 or (tb_rows == chunk and n_tb == 1)
    rb0 = row0 // rows
    zrow = lambda sb, h, tb: rb0 + sb * n_tb + tb

    def zspec(col0, width):
        cb = col0 // width
        if width == wc:
            return pl.BlockSpec((rows, width), lambda sb, h, tb: (zrow(sb, h, tb), cb + h))
        return pl.BlockSpec((rows, width), lambda sb, h, tb: (zrow(sb, h, tb), cb))

    def pspec(col0, width):
        cb = col0 // width
        if width == wc:
            return pl.BlockSpec((nb * 8, width), lambda sb, h, tb: (sb, cb + h))
        return pl.BlockSpec((nb * 8, width), lambda sb, h, tb: (sb, cb))

    vec = lambda: pl.BlockSpec((1, wc), lambda sb, h, tb: (0, h))
    vec_at = lambda col0: pl.BlockSpec((1, wc), lambda sb, h, tb: (0, col0 // wc + h))
    lora = lambda kdim: pl.BlockSpec((kdim, wc), lambda sb, h, tb: (0, h))
    kern = functools.partial(_rwkv_kernel, nb=nb, tb_rows=tb_rows, chunk=chunk, hb=hb, local=local)
    scr = lambda: pltpu.VMEM((rows, wc), f32)
    return pl.pallas_call(
        kern,
        grid=(nseq // nb, NH_R // hb, n_tb),
        in_specs=[
            zspec(Z_RR, wc), zspec(Z_RK, wc), zspec(Z_RV, wc), zspec(Z_SM, SM_W),
            pspec(0, wc), pspec(D_RWKV, wc), pspec(2 * D_RWKV, wc), pspec(3 * D_RWKV, SM_W),
            vec_at(0), vec_at(D_RWKV), vec_at(2 * D_RWKV),
            pl.BlockSpec((1, SM_W), lambda sb, h, tb: (0, 3 * D_RWKV // SM_W)),
            vec(), vec(), vec(), vec(), vec(), vec(), vec(),
            lora(128), lora(128), lora(G_LORA),
            pl.BlockSpec((1, nb, hb, HEAD_R, HEAD_R), lambda sb, h, tb: (layer, sb, h, 0, 0)),
        ],
        out_specs=[
            pl.BlockSpec((rows, wc), lambda sb, h, tb: (sb * n_tb + tb, h)),
            pl.BlockSpec((nb, hb, HEAD_R, HEAD_R), lambda sb, h, tb: (sb, h, 0, 0)),
        ],
        out_shape=[
            jax.ShapeDtypeStruct((nseq * seq_len, D_RWKV), bf16),
            jax.ShapeDtypeStruct((nseq, NH_R, HEAD_R, HEAD_R), f32),
        ],
        scratch_shapes=[
            pltpu.VMEM((nb, hb, HEAD_R, HEAD_R), f32),
            pltpu.VMEM((4, 8, SM_W), f32),
            *[scr() for _ in range(11)],
            pltpu.VMEM((rows // chunk, hb, HEAD_R, HEAD_R), f32),
            pltpu.VMEM((rows // chunk, hb, HEAD_R, HEAD_R), f32),
        ],
        compiler_params=_cparams(("parallel", "parallel", "arbitrary")),
        name="rwkv",
    )(z, z, z, z, zprev_rows, zprev_rows, zprev_rows, zprev_rows,
      p["mu"], p["mu"], p["mu"], p["mu"], p["w0"], p["a0"], p["k_k"], p["k_a"], p["r_k"], p["lnx_w"], p["lnx_b"],
      p["w2"], p["a2"], p["g2"], s0)


def _mlstm_kernel(zq_ref, zk_ref, zv_ref, zo_ref, gate_ref, bias_ref, nw_ref, c0_ref, n0_ref, m0_ref,
                  y_ref, co_ref, no_ref, mo_ref,
                  c_scr, n_scr, m_scr, o_scr, *, nb, hm, tb_rows, chunk):
    tb = pl.program_id(2)
    n_tb = pl.num_programs(2)
    per_seq = tb_rows // chunk

    @pl.when(tb == 0)
    def _():
        c_scr[...] = c0_ref[0]
        n_scr[...] = n0_ref[0]
        m_scr[...] = m0_ref[0]

    incl, _, eye = _tri_masks(chunk)
    upper = lax.broadcasted_iota(i32, (chunk, chunk), 0) <= lax.broadcasted_iota(i32, (chunk, chunk), 1)
    pairs = [(s, h) for s in range(nb) for h in range(hm)]
    each = lambda f, *ls: [f(*xs) for xs in zip(*ls)]
    square = lambda row: jnp.broadcast_to(row, (chunk, chunk))
    row_sum = lambda x: jnp.sum(x, axis=-1, keepdims=True)

    def body(ci, carry):
        def load(ref):
            out = []
            for s, h in pairs:
                start = s * tb_rows + ci * chunk
                start = start if isinstance(start, int) else pl.multiple_of(start, chunk)
                out.append(ref[pl.ds(start, chunk), h * DK:(h + 1) * DK])
            return out

        q, v, o_gate = load(zq_ref), load(zv_ref), load(zo_ref)
        k = [x * (DK ** -0.5) for x in load(zk_ref)]
        gch = [gate_ref[h, s * per_seq + ci] for s, h in pairs]
        ic_row = [g[0:1, :] + bias_ref[h, 0:1, 0:1] for g, (s, h) in zip(gch, pairs)]
        fc_row = [-_softplus(-(g[1:2, :] + bias_ref[h, 1:2, 0:1])) for g, (s, h) in zip(gch, pairs)]
        c_mat = [c_scr[s, h] for s, h in pairs]
        n_row = [n_scr[s, h] for s, h in pairs]
        m_prev = [m_scr[s, h][:, 0:1] for s, h in pairs]
        f_col = each(lambda fr: row_sum(jnp.where(incl, square(fr), 0.0)), fc_row)
        fc_col = each(lambda fr: row_sum(jnp.where(eye, square(fr), 0.0)), fc_row)
        ic_col = each(lambda ir: row_sum(jnp.where(eye, square(ir), 0.0)), ic_row)
        f_row = each(lambda fc: jnp.sum(jnp.where(upper, square(fc), 0.0), axis=0, keepdims=True), fc_col)
        dm = each(lambda fc, fr, ir: jnp.where(incl, fc - fr + ir, -jnp.inf), f_col, f_row, ic_row)
        inter = each(lambda fc, m: fc + m, f_col, m_prev)
        m_t = each(lambda it_, d: jnp.maximum(it_, jnp.max(d, axis=-1, keepdims=True)), inter, dm)
        w_intra = each(lambda d, m: jnp.exp(d - m), dm, m_t)
        w_inter = each(lambda it_, m: jnp.exp(it_ - m), inter, m_t)
        s_mat = each(lambda q_, k_, w: _bdot(q_, k_, NT) * w, q, k, w_intra)
        qc = each(_bdot, q, c_mat)
        num = each(lambda s_, v_, w, x: _bdot(s_, v_) + w * x, s_mat, v, w_inter, qc)
        den = each(lambda s_, w, q_, n_: row_sum(s_) + w * row_sum(q_ * n_), s_mat, w_inter, q, n_row)
        hid = each(lambda a, d, m: a / jnp.maximum(jnp.abs(d), jnp.exp(-m)), num, den, m_t)
        m_new = each(lambda m: m[chunk - 1:chunk, :], m_t)
        w_s = each(lambda fc, ic, m: jnp.exp(fc[chunk - 1:chunk, :] - fc + ic - m), f_col, ic_col, m_new)
        w_c = each(lambda it_, m: jnp.exp(it_[chunk - 1:chunk, :] - m), inter, m_new)
        kw = each(lambda k_, w: k_ * w, k, w_s)
        c_new = each(lambda w, c, kw_, v_: w * c + _bdot(kw_, v_, TN), w_c, c_mat, kw, v)
        n_new = each(lambda w, n_, kw_: w * n_ + jnp.sum(kw_, axis=0, keepdims=True), w_c, n_row, kw)
        mean = each(lambda x: jnp.mean(x, axis=-1, keepdims=True), hid)
        var = each(lambda x, mu: jnp.mean(jnp.square(x - mu), axis=-1, keepdims=True), hid, mean)
        for i, (s, h) in enumerate(pairs):
            c_scr[s, h] = c_new[i]
            n_scr[s, h] = n_new[i]
            m_scr[s, h] = jnp.broadcast_to(m_new[i], (1, 128))
            hn = (hid[i] - mean[i]) * lax.rsqrt(var[i] + MH_EPS) * nw_ref[:, h * DK:(h + 1) * DK]
            start = s * tb_rows + ci * chunk
            start = start if isinstance(start, int) else pl.multiple_of(start, chunk)
            o_scr[pl.ds(start, chunk), h * DK:(h + 1) * DK] = hn * jax.nn.sigmoid(o_gate[i])
        return carry

    if per_seq == 1:
        body(0, 0)
    else:
        lax.fori_loop(0, per_seq, body, 0)
    y_ref[...] = o_scr[...].astype(y_ref.dtype)

    @pl.when(tb == n_tb - 1)
    def _():
        co_ref[...] = c_scr[...]
        no_ref[...] = n_scr[...]
        mo_ref[...] = m_scr[...]


def _mlstm_call(z, gates, bias, norm_w, c0, n0, m0, layer, *, nseq, seq_len, row0, nb, hm, tb_rows, chunk):
    rows = nb * tb_rows
    n_tb = seq_len // tb_rows
    rb0 = row0 // rows
    cpb = rows // chunk
    wm = hm * DK
    zspec = lambda col0: pl.BlockSpec((rows, wm), lambda sb, h, tb: (rb0 + sb * n_tb + tb, col0 // wm + h))
    st4 = lambda a, b: pl.BlockSpec((nb, hm, a, b), lambda sb, h, tb: (sb, h, 0, 0))
    st5 = lambda a, b: pl.BlockSpec((1, nb, hm, a, b), lambda sb, h, tb: (layer, sb, h, 0, 0))
    kern = functools.partial(_mlstm_kernel, nb=nb, hm=hm, tb_rows=tb_rows, chunk=chunk)
    return pl.pallas_call(
        kern,
        grid=(nseq // nb, NH_M // hm, n_tb),
        in_specs=[
            zspec(Z_MQ), zspec(Z_MK), zspec(Z_MV), zspec(Z_MO),
            pl.BlockSpec((hm, cpb, 2, chunk), lambda sb, h, tb: (h, sb * n_tb + tb, 0, 0)),
            pl.BlockSpec((hm, 2, 128), lambda sb, h, tb: (h, 0, 0)),
            pl.BlockSpec((1, wm), lambda sb, h, tb: (0, h)),
            st5(DK, DK), st5(1, DK), st5(1, 128),
        ],
        out_specs=[
            pl.BlockSpec((rows, wm), lambda sb, h, tb: (sb * n_tb + tb, h)),
            st4(DK, DK), st4(1, DK), st4(1, 128),
        ],
        out_shape=[
            jax.ShapeDtypeStruct((nseq * seq_len, D_MLSTM), bf16),
            jax.ShapeDtypeStruct((nseq, NH_M, DK, DK), f32),
            jax.ShapeDtypeStruct((nseq, NH_M, 1, DK), f32),
            jax.ShapeDtypeStruct((nseq, NH_M, 1, 128), f32),
        ],
        scratch_shapes=[
            pltpu.VMEM((nb, hm, DK, DK), f32),
            pltpu.VMEM((nb, hm, 1, DK), f32),
            pltpu.VMEM((nb, hm, 1, 128), f32),
            pltpu.VMEM((rows, wm), f32),
        ],
        compiler_params=_cparams(("parallel", "parallel", "arbitrary")),
        name="mlstm",
    )(z, z, z, z, gates, bias, norm_w, c0, n0, m0)


def _norm_route_kernel(x_ref, w_ref, scp_ref, scs_ref, shp_ref, shs_ref, rw_ref, rb_ref,
                       hn_ref, ei_ref, ew_ref, *, n_prompt_tiles):
    is_prompt = pl.program_id(0) < n_prompt_tiles
    y = _rms(x_ref[...], w_ref[...]).reshape(SEQ_PER_TILE, 8, D)
    sc = _pick_mod(is_prompt, scp_ref, scs_ref)
    sh = _pick_mod(is_prompt, shp_ref, shs_ref)
    hn = (y * (1.0 + sc) + sh).reshape(ROW_TILE, D)
    hn_ref[...] = hn
    logits = _bdot(hn, rw_ref[...]) + rb_ref[...]
    lane = lax.broadcasted_iota(i32, logits.shape, 1)
    lane_f = lane.astype(f32)
    valid = lane < N_EXPERTS
    lg = jnp.where(valid, logits, -1e30)
    ex = jnp.where(valid, jnp.exp(lg - jnp.max(lg, axis=-1, keepdims=True)), 0.0)
    probs = ex / jnp.sum(ex, axis=-1, keepdims=True)
    best = None
    for g in range(N_GROUPS):
        in_g = (lane >= g * EXPERTS_PER_GROUP) & (lane < (g + 1) * EXPERTS_PER_GROUP)
        pg = jnp.where(in_g, probs, -1.0)
        m1 = jnp.max(pg, axis=-1, keepdims=True)
        i1 = jnp.min(jnp.where(pg == m1, lane_f, 1e4), axis=-1, keepdims=True)
        pg2 = jnp.where(lane_f == i1, -1.0, pg)
        m2 = jnp.max(pg2, axis=-1, keepdims=True)
        i2 = jnp.min(jnp.where(pg2 == m2, lane_f, 1e4), axis=-1, keepdims=True)
        cand = (m1 + m2, m1, m2, i1, i2)
        if best is None:
            best = cand
        else:
            upd = cand[0] > best[0]
            best = tuple(jnp.where(upd, c, b) for c, b in zip(cand, best))
    _, m1, m2, i1, i2 = best
    tot = m1 + m2
    ei_ref[...] = jnp.where(lane == 0, i1, jnp.where(lane == 1, i2, 0.0)).astype(i32)
    ew_ref[...] = jnp.where(lane == 0, m1 / tot, jnp.where(lane == 1, m2 / tot, 0.0))


def _norm_route_call(x, w, sc_p, sc_s, sh_p, sh_s, rw, rb, prompt_len):
    m = x.shape[0]
    n_prompt_seq = sc_p.shape[0]
    npt = n_prompt_seq * prompt_len // ROW_TILE
    pspec, sspec = _mod_specs(npt, n_prompt_seq, prompt_len, D)
    row = lambda wd: pl.BlockSpec((ROW_TILE, wd), lambda i: (i, 0))
    return pl.pallas_call(
        functools.partial(_norm_route_kernel, n_prompt_tiles=npt),
        grid=(m // ROW_TILE,),
        in_specs=[row(D), pl.BlockSpec((1, D), lambda i: (0, 0)), pspec, sspec, pspec, sspec,
                  pl.BlockSpec((D, 128), lambda i: (0, 0)), pl.BlockSpec((1, 128), lambda i: (0, 0))],
        out_specs=[row(D), row(128), row(128)],
        out_shape=[jax.ShapeDtypeStruct((m, D), f32), jax.ShapeDtypeStruct((m, 128), i32),
                   jax.ShapeDtypeStruct((m, 128), f32)],
        compiler_params=_cparams(("parallel",)),
        name="norm2_route",
    )(x, w.reshape(1, D), sc_p, sc_s, sh_p, sh_s, rw, rb)


def _row_copy(src_hbm, row, dst, r, sem):
    return pltpu.make_async_copy(src_hbm.at[pl.ds(row, 1), :], dst.at[pl.ds(r, 1), :], sem)


def _gather_kernel(tok_ref, valid_ref, hn_hbm, o_ref, buf, sem):
    j = pl.program_id(0)
    n_j = pl.num_programs(0)

    def issue(block, slot):
        def one(r, c):
            _row_copy(hn_hbm, tok_ref[block * MOE_TM + r], buf.at[slot], r, sem.at[slot]).start()
            return c
        lax.fori_loop(0, MOE_TM, one, 0, unroll=8)

    def wait(slot):
        def one(r, c):
            _row_copy(hn_hbm, 0, buf.at[slot], r, sem.at[slot]).wait()
            return c
        lax.fori_loop(0, MOE_TM, one, 0, unroll=8)

    @pl.when((j == 0) & (valid_ref[0] == 1))
    def _():
        issue(0, 0)

    nxt = jnp.minimum(j + 1, n_j - 1)

    @pl.when((j + 1 < n_j) & (valid_ref[nxt] == 1))
    def _():
        issue(j + 1, (j + 1) % 2)

    @pl.when(valid_ref[j] == 1)
    def _():
        wait(j % 2)
        o_ref[...] = buf[j % 2].astype(o_ref.dtype)

    @pl.when(valid_ref[j] == 0)
    def _():
        o_ref[...] = jnp.zeros_like(o_ref)


def _gather_call(hn, slot_tok, valid):
    p = slot_tok.shape[0]
    return pl.pallas_call(
        _gather_kernel,
        grid_spec=pltpu.PrefetchScalarGridSpec(
            num_scalar_prefetch=2,
            grid=(p // MOE_TM,),
            in_specs=[pl.BlockSpec(memory_space=pl.ANY)],
            out_specs=pl.BlockSpec((MOE_TM, D), lambda j, tok, va: (j, 0)),
            scratch_shapes=[pltpu.VMEM((2, MOE_TM, D), f32), pltpu.SemaphoreType.DMA((2,))],
        ),
        out_shape=jax.ShapeDtypeStruct((p, D), bf16),
        compiler_params=_cparams(("arbitrary",)),
        name="moe_gather",
    )(slot_tok, valid, hn)


def _stream_expert_tiles(be_ref, first_ref, next_ref, w_hbm, wbuf, w_bf, sem, run_ref, *, layer, tn, col_tiles):
    n = pl.program_id(0)
    j = pl.program_id(1)
    n_pass = pl.num_programs(0)

    def copies(e, nn, slot):
        return [pltpu.make_async_copy(w_hbm.at[layer, e, :, pl.ds(pl.multiple_of(ct(nn) * tn, tn), tn)],
                                      wbuf.at[slot, c], sem.at[slot]) for c, ct in enumerate(col_tiles)]

    @pl.when((n == 0) & (j == 0))
    def _():
        run_ref[0] = 0
        for cp in copies(be_ref[0], 0, 0):
            cp.start()

    @pl.when(first_ref[j] == 1)
    def _():
        slot = run_ref[0] % 2
        for cp in copies(be_ref[j], n, slot):
            cp.wait()
        nxt = next_ref[j]

        @pl.when(nxt >= 0)
        def _():
            for cp in copies(nxt, n, 1 - slot):
                cp.start()

        @pl.when((nxt < 0) & (n + 1 < n_pass))
        def _():
            for cp in copies(be_ref[0], n + 1, 1 - slot):
                cp.start()

        for c in range(len(col_tiles)):
            w_bf[c] = wbuf[slot, c].astype(bf16)
        run_ref[0] = run_ref[0] + 1


def _moe1_kernel(be_ref, first_ref, next_ref, valid_ref, x_ref, w_hbm, act_ref, wbuf, w_bf, sem, run_ref,
                 *, layer, tn, n_t):
    j = pl.program_id(1)
    _stream_expert_tiles(be_ref, first_ref, next_ref, w_hbm, wbuf, w_bf, sem, run_ref, layer=layer, tn=tn,
                         col_tiles=(lambda n: n, lambda n: n_t + n))

    @pl.when(valid_ref[j] == 1)
    def _():
        x = x_ref[...]
        g = jnp.dot(x, w_bf[0], preferred_element_type=f32)
        u = jnp.dot(x, w_bf[1], preferred_element_type=f32)
        act_ref[...] = (g * jax.nn.sigmoid(g) * u).astype(act_ref.dtype)

    @pl.when(valid_ref[j] == 0)
    def _():
        act_ref[...] = jnp.zeros_like(act_ref)


def _moe1_call(xs, w_in, layer, block_e, first, nxt, valid):
    p = xs.shape[0]
    tn = 512
    n_t = D_EXPERT // tn
    return pl.pallas_call(
        functools.partial(_moe1_kernel, layer=layer, tn=tn, n_t=n_t),
        grid_spec=pltpu.PrefetchScalarGridSpec(
            num_scalar_prefetch=4,
            grid=(n_t, p // MOE_TM),
            in_specs=[
                pl.BlockSpec((MOE_TM, D), lambda n, j, *_: (j, 0)),
                pl.BlockSpec(memory_space=pl.ANY),
            ],
            out_specs=pl.BlockSpec((MOE_TM, tn), lambda n, j, *_: (j, n)),
            scratch_shapes=[pltpu.VMEM((2, 2, D, tn), f32), pltpu.VMEM((2, D, tn), bf16),
                            pltpu.SemaphoreType.DMA((2,)), pltpu.SMEM((1,), i32)],
        ),
        out_shape=jax.ShapeDtypeStruct((p, D_EXPERT), bf16),
        compiler_params=_cparams(("arbitrary", "arbitrary"), MOE_UP_VMEM),
        name="moe_up",
    )(block_e, first, nxt, valid, xs, w_in)


def _moe2_kernel(be_ref, first_ref, next_ref, valid_ref, a_ref, w_hbm, o_ref, wbuf, w_bf, sem, run_ref,
                 *, layer, tn):
    j = pl.program_id(1)
    _stream_expert_tiles(be_ref, first_ref, next_ref, w_hbm, wbuf, w_bf, sem, run_ref, layer=layer, tn=tn,
                         col_tiles=(lambda n: n,))

    @pl.when(valid_ref[j] == 1)
    def _():
        o_ref[...] = jnp.dot(a_ref[...], w_bf[0], preferred_element_type=f32)

    @pl.when(valid_ref[j] == 0)
    def _():
        o_ref[...] = jnp.zeros_like(o_ref)


def _moe2_call(act, w_out, layer, block_e, first, nxt, valid):
    p = act.shape[0]
    tn = 2048
    return pl.pallas_call(
        functools.partial(_moe2_kernel, layer=layer, tn=tn),
        grid_spec=pltpu.PrefetchScalarGridSpec(
            num_scalar_prefetch=4,
            grid=(D // tn, p // MOE_TM),
            in_specs=[
                pl.BlockSpec((MOE_TM, D_EXPERT), lambda n, j, *_: (j, 0)),
                pl.BlockSpec(memory_space=pl.ANY),
            ],
            out_specs=pl.BlockSpec((MOE_TM, tn), lambda n, j, *_: (j, n)),
            scratch_shapes=[pltpu.VMEM((2, 1, D_EXPERT, tn), f32), pltpu.VMEM((1, D_EXPERT, tn), bf16),
                            pltpu.SemaphoreType.DMA((2,)), pltpu.SMEM((1,), i32)],
        ),
        out_shape=jax.ShapeDtypeStruct((p, D), f32),
        compiler_params=_cparams(("arbitrary", "arbitrary")),
        name="moe_down",
    )(block_e, first, nxt, valid, act, w_out)


def _combine_kernel(pos_ref, x_ref, ew_ref, gp_ref, gs_ref, yb_hbm, o_ref, buf, sem, *, n_prompt_tiles):
    i = pl.program_id(0)
    n_i = pl.num_programs(0)

    def issue(tile, slot):
        def one(r, c):
            for k in range(2):
                _row_copy(yb_hbm, pos_ref[2 * (tile * ROW_TILE + r) + k], buf.at[slot, k], r, sem.at[slot]).start()
            return c
        lax.fori_loop(0, ROW_TILE, one, 0, unroll=4)

    def wait(slot):
        def one(r, c):
            for k in range(2):
                _row_copy(yb_hbm, 0, buf.at[slot, k], r, sem.at[slot]).wait()
            return c
        lax.fori_loop(0, ROW_TILE, one, 0, unroll=4)

    @pl.when(i == 0)
    def _():
        issue(0, 0)

    @pl.when(i + 1 < n_i)
    def _():
        issue(i + 1, (i + 1) % 2)

    wait(i % 2)
    ew = ew_ref[...]
    y = buf[i % 2, 0] * ew[:, 0:1] + buf[i % 2, 1] * ew[:, 1:2]
    g = _pick_mod(i < n_prompt_tiles, gp_ref, gs_ref)
    o_ref[...] = (x_ref[...].reshape(SEQ_PER_TILE, 8, D) + g * y.reshape(SEQ_PER_TILE, 8, D)).reshape(ROW_TILE, D)


def _combine_call(x, yb, pos, ew, g_p, g_s, prompt_len):
    m = x.shape[0]
    n_prompt_seq = g_p.shape[0]
    npt = n_prompt_seq * prompt_len // ROW_TILE
    tiles_per_seq = prompt_len // ROW_TILE
    return pl.pallas_call(
        functools.partial(_combine_kernel, n_prompt_tiles=npt),
        grid_spec=pltpu.PrefetchScalarGridSpec(
            num_scalar_prefetch=1,
            grid=(m // ROW_TILE,),
            in_specs=[
                pl.BlockSpec((ROW_TILE, D), lambda i, pos: (i, 0)),
                pl.BlockSpec((ROW_TILE, 128), lambda i, pos: (i, 0)),
                pl.BlockSpec((1, 1, D), lambda i, pos: (jnp.minimum(i // tiles_per_seq, n_prompt_seq - 1), 0, 0)),
                pl.BlockSpec((SEQ_PER_TILE, 1, D), lambda i, pos: (jnp.maximum(i - npt, 0), 0, 0)),
                pl.BlockSpec(memory_space=pl.ANY),
            ],
            out_specs=pl.BlockSpec((ROW_TILE, D), lambda i, pos: (i, 0)),
            scratch_shapes=[pltpu.VMEM((2, 2, ROW_TILE, D), f32), pltpu.SemaphoreType.DMA((2,))],
        ),
        out_shape=jax.ShapeDtypeStruct((m, D), f32),
        compiler_params=_cparams(("arbitrary",)),
        name="moe_combine",
    )(pos, x, ew, g_p, g_s, yb)


def _moe_plan(eidx, n_blocks):
    m = eidx.shape[0]
    a = 2 * m
    flat_e = eidx.reshape(a)
    order = jnp.argsort(flat_e).astype(i32)
    sorted_e = flat_e[order]
    bounds = jnp.searchsorted(sorted_e, jnp.arange(N_EXPERTS + 1, dtype=i32), side="left").astype(i32)
    start, counts = bounds[:-1], bounds[1:] - bounds[:-1]
    padded = (counts + MOE_TM - 1) // MOE_TM * MOE_TM
    pad_end = jnp.cumsum(padded)
    pad_start = pad_end - padded
    dest = pad_start[sorted_e] + jnp.arange(a, dtype=i32) - start[sorted_e]
    pos = dest[jnp.argsort(order)]
    blk = jnp.arange(n_blocks, dtype=i32) * MOE_TM
    block_e = jnp.minimum(jnp.searchsorted(pad_end, blk, side="right"), N_EXPERTS - 1).astype(i32)
    valid = (blk < pad_end[-1]).astype(i32)
    slot = jnp.arange(n_blocks * MOE_TM, dtype=i32)
    slot_e = jnp.repeat(block_e, MOE_TM)
    within = slot - pad_start[slot_e]
    slot_tok = jnp.where(within < counts[slot_e], order[jnp.clip(start[slot_e] + within, 0, a - 1)] // 2, 0)
    first = jnp.concatenate([jnp.ones((1,), i32), (block_e[1:] != block_e[:-1]).astype(i32)]) * valid
    first_at = jnp.where(first == 1, jnp.arange(n_blocks, dtype=i32), n_blocks)
    later = jnp.concatenate([lax.cummin(first_at[::-1])[::-1][1:], jnp.full((1,), n_blocks, i32)])
    nxt = jnp.where(later < n_blocks, block_e[jnp.minimum(later, n_blocks - 1)], -1).astype(i32)
    return slot_tok, pos, block_e, first, nxt, valid


def _prep_w_in(w):
    wt = jnp.swapaxes(w, 0, 1)
    zeros = lambda n: jnp.zeros((n, wt.shape[1]), wt.dtype)
    mb = RWKV_COLS
    gb = RWKV_COLS + MLSTM_COLS
    parts = [
        wt[0:2048], wt[2144:4192], wt[4192:6240],
        wt[mb:mb + 2048], wt[mb + 2048:mb + 4096], wt[mb + 4096:mb + 6144], wt[mb + 6160:mb + 8208],
        wt[gb:gb + 8192],
        wt[6336:6592], wt[2048:2144], zeros(32), wt[6240:6336], zeros(32),
        wt[mb + 6144:mb + 6160], zeros(112), zeros(NZ - Z_IF - 128),
    ]
    return jnp.concatenate(parts, axis=0).astype(bf16)


def _prep_mu(mu):
    zeros = jnp.zeros((32,), mu.dtype)
    return jnp.concatenate([mu[0:2048], mu[2144:4192], mu[4192:6240], mu[6336:6592], mu[2048:2144], zeros,
                            mu[6240:6336], zeros]).reshape(1, ZP_COLS)


def _pad_rows(w, rows):
    return jnp.concatenate([w, jnp.zeros((rows - w.shape[0], w.shape[1]), w.dtype)], axis=0).astype(bf16)


def _gate_rows(z, row0, rows, chunk):
    zif = z[row0:row0 + rows, Z_IF:Z_IF + 2 * NH_M].reshape(rows // chunk, chunk, 2, NH_M)
    return jnp.transpose(zif, (3, 0, 2, 1))


def kernel(x_prompt, x_sample, c_prompt, c_sample, state_wkv, state_shift, state_C, state_n, state_m, ada_w, ada_b, norm1_w, norm2_w, w_in, rwkv_mu, rwkv_w0, rwkv_w2, rwkv_a0, rwkv_a2, rwkv_g2, rwkv_k_k, rwkv_k_a, rwkv_r_k, rwkv_lnx_w, rwkv_lnx_b, mlstm_i_b, mlstm_f_b, mlstm_norm_w, w_branch_r, w_branch_m, w_out, router_w, router_b, moe_w_in, moe_w_out, final_norm_w):
    bp, tp, _ = x_prompt.shape
    bs, ts, _ = x_sample.shape
    mp, ms = bp * tp, bs * ts
    m = mp + ms
    assert ts == 8 and tp % ROW_TILE == 0 and ms % ROW_TILE == 0 and bs % SEQ_PER_TILE == 0 and m % 512 == 0
    n_seq = bp + bs
    n_seq_pad = -(-n_seq // 8) * 8

    x = jnp.concatenate([x_prompt.reshape(mp, D), x_sample.reshape(ms, D)], axis=0)
    c_all = jnp.concatenate([c_prompt, c_sample, jnp.zeros((n_seq_pad - n_seq, D), f32)], axis=0)
    ada = _ada_call(c_all, ada_w, ada_b).reshape(DEPTH, n_seq_pad, N_ADA, D)
    last_rows = jnp.concatenate([jnp.arange(bp, dtype=i32) * tp + tp - 1,
                                 mp + jnp.arange(bs, dtype=i32) * ts + ts - 1,
                                 jnp.zeros((n_seq_pad - n_seq,), i32)])
    router_w_p = jnp.concatenate([router_w, jnp.zeros((D, 128 - N_EXPERTS), f32)], axis=1).astype(bf16)
    router_b_p = jnp.concatenate([router_b, jnp.zeros((128 - N_EXPERTS,), f32)]).reshape(1, 128)
    n_blocks = -(-(2 * m + N_EXPERTS * (MOE_TM - 1)) // MOE_TM)

    prompt_geo = dict(nseq=bp, seq_len=tp, row0=0, nb=1, tb_rows=256, chunk=64)
    sample_geo = dict(nseq=bs, seq_len=ts, row0=mp, tb_rows=8, chunk=8)
    rwkv_p, rwkv_s = dict(prompt_geo, local=4), dict(sample_geo, nb=16, local=4)
    mlstm_p, mlstm_s = dict(prompt_geo, hm=4), dict(sample_geo, nb=4, hm=4)

    outs_p = [[] for _ in range(5)]
    outs_s = [[] for _ in range(5)]
    for l in range(DEPTH):
        mod = lambda idx: (ada[l, :bp, idx][:, None, :], ada[l, bp:n_seq, idx][:, None, :])
        sh1, sc1, g1, sh2, sc2, g2 = (mod(i) for i in range(N_ADA))
        wz = _prep_w_in(w_in[l])

        xn = _norm_mod_call(x, norm1_w[l], sc1[0], sc1[1], sh1[0], sh1[1], tp)
        shift_rows = _norm_rows_call(x[last_rows], norm1_w[l], ada[l, :, 1], ada[l, :, 0])
        z = _matmul_call(xn, wz, 512, 1024, name="w_in")
        zprev_s = _matmul_call(state_shift[l].astype(bf16), wz, bs, 512, n_out=ZP_COLS,
                               col_map=lambda j: jnp.where(j < 12, j, Z_SM // 512), name="w_in_prev")
        rp = dict(mu=_prep_mu(rwkv_mu[l]), w0=rwkv_w0[l].reshape(1, -1), a0=rwkv_a0[l].reshape(1, -1),
                  k_k=rwkv_k_k[l].reshape(1, -1), k_a=rwkv_k_a[l].reshape(1, -1), r_k=rwkv_r_k[l].reshape(1, -1),
                  lnx_w=rwkv_lnx_w[l].reshape(1, -1), lnx_b=rwkv_lnx_b[l].reshape(1, -1),
                  w2=_pad_rows(rwkv_w2[l], 128), a2=_pad_rows(rwkv_a2[l], 128), g2=rwkv_g2[l].astype(bf16))
        yr_p, wkv_p = _rwkv_call(z, jnp.zeros((bp * 8, ZP_COLS), f32),
                                 jnp.zeros((1, bp, NH_R, HEAD_R, HEAD_R), f32), 0, rp, **rwkv_p)
        yr_s, wkv_s = _rwkv_call(z, jnp.repeat(zprev_s, 8, axis=0), state_wkv, l, rp, **rwkv_s)
        bias = jnp.broadcast_to(jnp.stack([mlstm_i_b[l], mlstm_f_b[l]], axis=1)[:, :, None], (NH_M, 2, 128))
        nw = mlstm_norm_w[l].reshape(1, -1)
        ym_p, c_p, n_p, m_p = _mlstm_call(
            z, _gate_rows(z, 0, mp, 64), bias, nw, jnp.zeros((1, bp, NH_M, DK, DK), f32),
            jnp.zeros((1, bp, NH_M, 1, DK), f32), jnp.zeros((1, bp, NH_M, 1, 128), f32), 0, **mlstm_p)
        ym_s, c_s, n_s, m_s = _mlstm_call(
            z, _gate_rows(z, mp, ms, 8), bias, nw, state_C, state_n[:, :, :, None, :],
            jnp.broadcast_to(state_m[:, :, :, None, None], (DEPTH, bs, NH_M, 1, 128)), l, **mlstm_s)
        merged = _merge_call(jnp.concatenate([yr_p, yr_s], axis=0), jnp.concatenate([ym_p, ym_s], axis=0),
                             w_branch_r[l].astype(bf16), w_branch_m[l].astype(bf16), z)
        x = _wout_call(merged, w_out[l].astype(bf16), x, g1[0], g1[1], tp)

        hn, eidx, ew = _norm_route_call(x, norm2_w[l], sc2[0], sc2[1], sh2[0], sh2[1], router_w_p, router_b_p, tp)
        slot_tok, pos, block_e, first, nxt, valid = _moe_plan(eidx[:, :2], n_blocks)
        xs = _gather_call(hn, slot_tok, valid)
        act = _moe1_call(xs, moe_w_in, l, block_e, first, nxt, valid)
        yb = _moe2_call(act, moe_w_out, l, block_e, first, nxt, valid)
        x = _combine_call(x, yb, pos, ew, g2[0], g2[1], tp)

        for acc, vals in ((outs_p, (wkv_p, shift_rows[:bp], c_p, n_p[:, :, 0], m_p[:, :, 0, 0])),
                          (outs_s, (wkv_s, shift_rows[bp:n_seq], c_s, n_s[:, :, 0], m_s[:, :, 0, 0]))):
            for lst, v in zip(acc, vals):
                lst.append(v)

    y_prompt = _final_norm_call(x, final_norm_w, 0, mp).reshape(bp, tp, D)
    y_sample = _final_norm_call(x, final_norm_w, mp, ms).reshape(bs, ts, D)
    return (y_prompt, y_sample, *(jnp.stack(v) for v in outs_p), *(jnp.stack(v) for v in outs_s))
```

```python
import functools
import math

import jax
import jax.numpy as jnp
from jax import lax
from jax.experimental import pallas as pl
from jax.experimental.pallas import tpu as pltpu

f32 = jnp.float32
bf16 = jnp.bfloat16
i32 = jnp.int32

D = 4096
DEPTH = 2
D_RWKV = 2048
HEAD_R = 64
NH_R = D_RWKV // HEAD_R
W_LORA = 96
A_LORA = 96
G_LORA = 256
LNX_EPS = 64e-5
D_MLSTM = 2048
NH_M = 8
DK = D_MLSTM // NH_M
MH_EPS = 1e-6
RWKV_COLS = 3 * D_RWKV + W_LORA + A_LORA + G_LORA
MLSTM_COLS = 4 * D_MLSTM + 2 * NH_M
N_EXPERTS = 32
EXPERTS_PER_GROUP = 8
N_GROUPS = 4
D_EXPERT = D // 4
N_ADA = 6
RMS_EPS = 1e-6

Z_RR, Z_RK, Z_RV = 0, 2048, 4096
Z_MQ, Z_MK, Z_MV, Z_MO = 6144, 8192, 10240, 12288
Z_GR, Z_GM = 14336, 18432
Z_SM = 22528
Z_IF = 23040
NZ = 23552
SM_W = 512
ZP_COLS = 3 * D_RWKV + SM_W

ROW_TILE = 256
SEQ_PER_TILE = 32
MOE_TM = 256
VMEM_LIMIT = 48 * 1024 * 1024
MOE_UP_VMEM = 58 * 1024 * 1024

NN = (((1,), (0,)), ((), ()))
NT = (((1,), (1,)), ((), ()))
TN = (((0,), (0,)), ((), ()))


def _cparams(sem, vmem=VMEM_LIMIT):
    return pltpu.CompilerParams(dimension_semantics=sem, vmem_limit_bytes=vmem)


def _bdot(a, b, dims=NN):
    return lax.dot_general(a.astype(bf16), b.astype(bf16), dims, preferred_element_type=f32)


def _softplus(x):
    return jnp.maximum(x, 0.0) + jnp.log1p(jnp.exp(-jnp.abs(x)))


def _ada_kernel(c_ref, w_ref, b_ref, o_ref):
    c = c_ref[...]
    a = c * jax.nn.sigmoid(c)
    o_ref[0] = _bdot(a, w_ref[0]) + b_ref[0]


def _ada_call(c_all, ada_w, ada_b):
    nb = c_all.shape[0]
    n = ada_w.shape[2]
    tn = 512
    return pl.pallas_call(
        _ada_kernel,
        grid=(DEPTH, n // tn),
        in_specs=[
            pl.BlockSpec((nb, D), lambda l, j: (0, 0)),
            pl.BlockSpec((1, D, tn), lambda l, j: (l, 0, j)),
            pl.BlockSpec((1, 1, tn), lambda l, j: (l, 0, j)),
        ],
        out_specs=pl.BlockSpec((1, nb, tn), lambda l, j: (l, 0, j)),
        out_shape=jax.ShapeDtypeStruct((DEPTH, nb, n), f32),
        compiler_params=_cparams(("parallel", "parallel")),
        name="ada",
    )(c_all, ada_w, ada_b.reshape(DEPTH, 1, n))


def _mod_specs(n_prompt_tiles, n_prompt_seq, prompt_len, width, col=None):
    tiles_per_seq = prompt_len // ROW_TILE
    if col is None:
        pmap = lambda i: (jnp.minimum(i // tiles_per_seq, n_prompt_seq - 1), 0, 0)
        smap = lambda i: (jnp.maximum(i - n_prompt_tiles, 0), 0, 0)
    else:
        pmap = lambda j, i: (jnp.minimum(i // tiles_per_seq, n_prompt_seq - 1), 0, j)
        smap = lambda j, i: (jnp.maximum(i - n_prompt_tiles, 0), 0, j)
    return pl.BlockSpec((1, 1, width), pmap), pl.BlockSpec((SEQ_PER_TILE, 1, width), smap)


def _pick_mod(is_prompt, p_ref, s_ref):
    return jnp.where(is_prompt, jnp.broadcast_to(p_ref[...], s_ref.shape), s_ref[...])


def _rms(x, w):
    return x * lax.rsqrt(jnp.mean(x * x, axis=-1, keepdims=True) + RMS_EPS) * w


def _norm_mod_kernel(x_ref, w_ref, scp_ref, scs_ref, shp_ref, shs_ref, o_ref, *, n_prompt_tiles):
    is_prompt = pl.program_id(0) < n_prompt_tiles
    y = _rms(x_ref[...], w_ref[...]).reshape(SEQ_PER_TILE, 8, D)
    sc = _pick_mod(is_prompt, scp_ref, scs_ref)
    sh = _pick_mod(is_prompt, shp_ref, shs_ref)
    o_ref[...] = (y * (1.0 + sc) + sh).reshape(ROW_TILE, D).astype(o_ref.dtype)


def _norm_mod_call(x, w, sc_p, sc_s, sh_p, sh_s, prompt_len):
    m = x.shape[0]
    n_prompt_seq = sc_p.shape[0]
    npt = n_prompt_seq * prompt_len // ROW_TILE
    pspec, sspec = _mod_specs(npt, n_prompt_seq, prompt_len, D)
    return pl.pallas_call(
        functools.partial(_norm_mod_kernel, n_prompt_tiles=npt),
        grid=(m // ROW_TILE,),
        in_specs=[pl.BlockSpec((ROW_TILE, D), lambda i: (i, 0)), pl.BlockSpec((1, D), lambda i: (0, 0)),
                  pspec, sspec, pspec, sspec],
        out_specs=pl.BlockSpec((ROW_TILE, D), lambda i: (i, 0)),
        out_shape=jax.ShapeDtypeStruct((m, D), bf16),
        compiler_params=_cparams(("parallel",)),
        name="norm1",
    )(x, w.reshape(1, D), sc_p, sc_s, sh_p, sh_s)


def _norm_rows_kernel(x_ref, w_ref, sc_ref, sh_ref, o_ref):
    o_ref[...] = _rms(x_ref[...], w_ref[...]) * (1.0 + sc_ref[...]) + sh_ref[...]


def _norm_rows_call(x_rows, w, sc_rows, sh_rows):
    return pl.pallas_call(
        _norm_rows_kernel,
        out_shape=jax.ShapeDtypeStruct(x_rows.shape, f32),
        name="norm_rows",
    )(x_rows, w.reshape(1, D), sc_rows, sh_rows)


def _final_norm_kernel(x_ref, w_ref, o_ref):
    o_ref[...] = _rms(x_ref[...], w_ref[...])


def _final_norm_call(x, w, row0, rows):
    off = row0 // ROW_TILE
    return pl.pallas_call(
        _final_norm_kernel,
        grid=(rows // ROW_TILE,),
        in_specs=[pl.BlockSpec((ROW_TILE, D), lambda i: (i + off, 0)), pl.BlockSpec((1, D), lambda i: (0, 0))],
        out_specs=pl.BlockSpec((ROW_TILE, D), lambda i: (i, 0)),
        out_shape=jax.ShapeDtypeStruct((rows, D), f32),
        compiler_params=_cparams(("parallel",)),
        name="final_norm",
    )(x, w.reshape(1, D))


def _mm_kernel(a_ref, wt_ref, o_ref):
    o_ref[...] = lax.dot_general(a_ref[...], wt_ref[...], NT, preferred_element_type=f32)


def _matmul_call(a, wt, tm, tn, n_out=None, col_map=None, name="matmul"):
    m, k = a.shape
    n_out = wt.shape[0] if n_out is None else n_out
    wmap = (lambda j, i: (j, 0)) if col_map is None else (lambda j, i: (col_map(j), 0))
    return pl.pallas_call(
        _mm_kernel,
        grid=(n_out // tn, m // tm),
        in_specs=[pl.BlockSpec((tm, k), lambda j, i: (i, 0)), pl.BlockSpec((tn, k), wmap)],
        out_specs=pl.BlockSpec((tm, tn), lambda j, i: (i, j)),
        out_shape=jax.ShapeDtypeStruct((m, n_out), f32),
        compiler_params=_cparams(("parallel", "parallel")),
        name=name,
    )(a, wt)


def _merge_kernel(yr_ref, ym_ref, wr_ref, wm_ref, gr_ref, gm_ref, o_ref):
    pr = jnp.dot(yr_ref[...], wr_ref[...], preferred_element_type=f32)
    pm = jnp.dot(ym_ref[...], wm_ref[...], preferred_element_type=f32)
    o_ref[...] = (jax.nn.sigmoid(gr_ref[...]) * pr + jax.nn.sigmoid(gm_ref[...]) * pm).astype(o_ref.dtype)


def _merge_call(yr, ym, wr, wm, z):
    m = yr.shape[0]
    tm, tn = 512, 1024
    return pl.pallas_call(
        _merge_kernel,
        grid=(D // tn, m // tm),
        in_specs=[
            pl.BlockSpec((tm, D_RWKV), lambda j, i: (i, 0)),
            pl.BlockSpec((tm, D_MLSTM), lambda j, i: (i, 0)),
            pl.BlockSpec((D_RWKV, tn), lambda j, i: (0, j)),
            pl.BlockSpec((D_MLSTM, tn), lambda j, i: (0, j)),
            pl.BlockSpec((tm, tn), lambda j, i: (i, Z_GR // tn + j)),
            pl.BlockSpec((tm, tn), lambda j, i: (i, Z_GM // tn + j)),
        ],
        out_specs=pl.BlockSpec((tm, tn), lambda j, i: (i, j)),
        out_shape=jax.ShapeDtypeStruct((m, D), bf16),
        compiler_params=_cparams(("parallel", "parallel")),
        name="merge",
    )(yr, ym, wr, wm, z, z)


def _wout_kernel(a_ref, w_ref, x_ref, gp_ref, gs_ref, o_ref, *, n_prompt_tiles):
    is_prompt = pl.program_id(1) < n_prompt_tiles
    tn = o_ref.shape[1]
    acc = jnp.dot(a_ref[...], w_ref[...], preferred_element_type=f32).reshape(SEQ_PER_TILE, 8, tn)
    g = _pick_mod(is_prompt, gp_ref, gs_ref)
    o_ref[...] = (x_ref[...].reshape(SEQ_PER_TILE, 8, tn) + g * acc).reshape(ROW_TILE, tn)


def _wout_call(merged, w, x, g_p, g_s, prompt_len):
    m = x.shape[0]
    tn = 1024
    n_prompt_seq = g_p.shape[0]
    npt = n_prompt_seq * prompt_len // ROW_TILE
    pspec, sspec = _mod_specs(npt, n_prompt_seq, prompt_len, tn, col=True)
    return pl.pallas_call(
        functools.partial(_wout_kernel, n_prompt_tiles=npt),
        grid=(D // tn, m // ROW_TILE),
        in_specs=[
            pl.BlockSpec((ROW_TILE, D), lambda j, i: (i, 0)),
            pl.BlockSpec((D, tn), lambda j, i: (0, j)),
            pl.BlockSpec((ROW_TILE, tn), lambda j, i: (i, j)),
            pspec, sspec,
        ],
        out_specs=pl.BlockSpec((ROW_TILE, tn), lambda j, i: (i, j)),
        out_shape=jax.ShapeDtypeStruct((m, D), f32),
        compiler_params=_cparams(("parallel", "parallel")),
        name="wout",
    )(merged, w, x, g_p, g_s)


def _tri_masks(n):
    t = lax.broadcasted_iota(i32, (n, n), 0)
    s = lax.broadcasted_iota(i32, (n, n), 1)
    return s <= t, s < t, s == t


def _rwkv_local(at, rt, bt, kt, v, bh, kh, lam):
    n = at[0].shape[0]
    incl, strict, eye = _tri_masks(n)
    eye_k = _tri_masks(HEAD_R)[2]
    each = lambda f, *ls: [f(*xs) for xs in zip(*ls)]
    g = each(lambda a_, r_, b_, k_: _bdot(jnp.concatenate([a_, r_], axis=0), jnp.concatenate([b_, k_], axis=0), NT),
             at, rt, bt, kt)
    n_mat = each(lambda x: jnp.where(strict, x[:n, :n], 0.0), g)
    m_mat = each(lambda x: jnp.where(strict, x[:n, n:], 0.0), g)
    pb = each(lambda x: jnp.where(incl, x[n:, :n], 0.0), g)
    pk = each(lambda x: jnp.where(incl, x[n:, n:], 0.0), g)
    t_mat = each(lambda x: jnp.where(eye, 1.0, x), n_mat)
    n_pow = n_mat
    lv = 2
    while lv < n:
        n_pow = each(lambda x: _bdot(x, x), n_pow)
        t_mat = each(lambda t_, p_: t_ + _bdot(t_, p_), t_mat, n_pow)
        lv *= 2
    tam = each(lambda t_, a_, m_: _bdot(t_, jnp.concatenate([a_, m_], axis=1)), t_mat, at, m_mat)
    qw = each(_bdot, pb, tam)
    q = each(lambda r_, x: r_ + x[:, :HEAD_R], rt, qw)
    w_loc = each(lambda x, p_, v_: _bdot(x[:, HEAD_R:] + p_, v_), qw, pk, v)
    ab = each(lambda x, b_: _bdot(x, b_, TN), tam, bh)
    a_tr = each(lambda l_, x: jnp.where(eye_k, l_, 0.0) + x[:HEAD_R], lam, ab)
    s_inc = each(lambda v_, k_, x: _bdot(v_, k_ + x[HEAD_R:], TN), v, kh, ab)
    return q, w_loc, a_tr, s_inc


def _split3(x):
    hi = x.astype(bf16)
    r1 = x - hi.astype(f32)
    mid = r1.astype(bf16)
    lo = (r1 - mid.astype(f32)).astype(bf16)
    return hi, mid, lo


def _dot3_left(w, x):
    return sum(jnp.dot(w, p, preferred_element_type=f32) for p in _split3(x))


def _dot3_right(x, w):
    return sum(jnp.dot(p, w, preferred_element_type=f32) for p in _split3(x))


def _rwkv_kernel(zr_ref, zk_ref, zv_ref, zs_ref, pr_ref, pk_ref, pv_ref, ps_ref,
                 mur_ref, muk_ref, muv_ref, mus_ref, w0_ref, a0_ref, kk_ref, ka_ref, rk_ref, lw_ref, lb_ref,
                 w2_ref, a2_ref, g2_ref, s0_ref,
                 y_ref, so_ref,
                 s_scr, prev_scr, at_scr, rt_scr, bt_scr, kt_scr, v_scr, bh_scr, kh_scr, lam_scr,
                 q_scr, wl_scr, y_scr, atr_scr, sinc_scr,
                 *, nb, tb_rows, chunk, hb, local):
    tb = pl.program_id(2)
    n_tb = pl.num_programs(2)
    rows = nb * tb_rows
    per_seq = tb_rows // chunk
    n_chunks = rows // chunk
    wc = hb * HEAD_R

    @pl.when(tb == 0)
    def _():
        if nb == 1:
            s_scr[...] = s0_ref[0]
            for slot, (p_ref, width) in enumerate(((pr_ref, wc), (pk_ref, wc), (pv_ref, wc), (ps_ref, SM_W))):
                prev_scr[slot, 0:1, 0:width] = p_ref[0:1, :]

    def shifted(z_ref, p_ref, slot, width):
        z = z_ref[...]
        rolled = pltpu.roll(z, 1, 0)
        row = lax.broadcasted_iota(i32, z.shape, 0)
        if nb == 1:
            zs = jnp.where(row == 0, prev_scr[slot, 0:1, 0:width], rolled)
            prev_scr[slot, 0:1, 0:width] = z[rows - 1:rows, :]
        else:
            zs = jnp.where(row % tb_rows == 0, p_ref[...], rolled)
        return z, zs

    def lerp(z_ref, p_ref, mu_ref, slot, width):
        z, zs = shifted(z_ref, p_ref, slot, width)
        return z + (zs - z) * mu_ref[...]

    r = lerp(zr_ref, pr_ref, mur_ref, 0, wc)
    k = lerp(zk_ref, pk_ref, muk_ref, 1, wc)
    v = lerp(zv_ref, pv_ref, muv_ref, 2, wc)
    sm = lerp(zs_ref, ps_ref, mus_ref, 3, SM_W)
    gl = sm[:, 0:G_LORA]
    wl = sm[:, G_LORA:G_LORA + 128]
    al = sm[:, G_LORA + 128:G_LORA + 256]
    w_raw = -_softplus(-(w0_ref[...] + _bdot(jnp.tanh(wl), w2_ref[...]))) - 0.5
    eta = jax.nn.sigmoid(a0_ref[...] + _bdot(al, a2_ref[...]))
    lw = -jnp.exp(w_raw)
    k2 = k * (1.0 + (eta - 1.0) * ka_ref[...])

    shift = chunk.bit_length() - 1
    ti = lax.broadcasted_iota(i32, (rows, rows), 0)
    si = lax.broadcasted_iota(i32, (rows, rows), 1)
    same_chunk = (ti >> shift) == (si >> shift)
    tri_sel = jnp.where(same_chunk & (si <= ti), 1.0, 0.0).astype(bf16)
    hshift = HEAD_R.bit_length() - 1
    head_sel = (lax.broadcasted_iota(i32, (wc, wc), 0) >> hshift) == (lax.broadcasted_iota(i32, (wc, wc), 1) >> hshift)
    head_sum = jnp.where(head_sel, 1.0, 0.0).astype(bf16)
    head_mean = jnp.where(head_sel, 1.0 / HEAD_R, 0.0).astype(bf16)

    c = k * kk_ref[...]
    c = c / jnp.maximum(jnp.sqrt(_dot3_right(c * c, head_sum)), 1e-12)
    b_vec = c * eta
    cum = _dot3_left(tri_sel, lw)
    tot = jnp.concatenate([jnp.broadcast_to(cum[(ci + 1) * chunk - 1:(ci + 1) * chunk, :], (chunk, wc))
                           for ci in range(n_chunks)], axis=0)
    e_out = jnp.exp(-cum)
    tail = jnp.exp(tot - cum)
    at_scr[...] = -c * jnp.exp(cum - lw)
    rt_scr[...] = r * jnp.exp(cum)
    bt_scr[...] = b_vec * e_out
    kt_scr[...] = k2 * e_out
    bh_scr[...] = b_vec * tail
    kh_scr[...] = k2 * tail
    lam_scr[...] = jnp.exp(tot)
    v_scr[...] = v
    bonus = _dot3_right(r * k2 * rk_ref[...], head_sum) * v
    gate = _bdot(jax.nn.sigmoid(gl), g2_ref[...])

    def local_body(it, carry):
        where = []
        for j in range(local):
            ci = it * local + j
            sl = pl.ds(pl.multiple_of(ci * chunk, chunk), chunk)
            where += [(ci, sl, h, slice(h * HEAD_R, (h + 1) * HEAD_R)) for h in range(hb)]
        load = lambda ref: [ref[sl, ls] for _, sl, _, ls in where]
        lam = [lam_scr[pl.ds(ci * chunk, 1), ls] for ci, _, _, ls in where]
        q, w_loc, a_tr, s_inc = _rwkv_local(load(at_scr), load(rt_scr), load(bt_scr), load(kt_scr), load(v_scr),
                                            load(bh_scr), load(kh_scr), lam)
        for i, (ci, sl, h, ls) in enumerate(where):
            q_scr[sl, ls] = q[i]
            wl_scr[sl, ls] = w_loc[i]
            atr_scr[ci, h] = a_tr[i]
            sinc_scr[ci, h] = s_inc[i]
        return carry

    lax.fori_loop(0, n_chunks // local, local_body, 0)

    def advance(ci, state):
        sl = pl.ds(ci * chunk if isinstance(ci, int) else pl.multiple_of(ci * chunk, chunk), chunk)
        ys, new_state = [], []
        for h in range(hb):
            ls = slice(h * HEAD_R, (h + 1) * HEAD_R)
            ys.append(_bdot(q_scr[sl, ls], state[h], NT) + wl_scr[sl, ls])
            new_state.append(_bdot(state[h], atr_scr[ci, h]) + sinc_scr[ci, h])
        for h in range(hb):
            y_scr[sl, h * HEAD_R:(h + 1) * HEAD_R] = ys[h]
        return new_state

    if nb == 1:
        state = [s_scr[0, h] for h in range(hb)]
        for ci in range(n_chunks):
            state = advance(ci, state)
        for h in range(hb):
            s_scr[0, h] = state[h]

        @pl.when(tb == n_tb - 1)
        def _():
            so_ref[...] = s_scr[...]
    else:
        def seq_body(it, carry):
            seqs = [it * local + j for j in range(local)]
            states = [[s0_ref[0, ci, h] for h in range(hb)] for ci in seqs]
            states = [advance(ci, st) for ci, st in zip(seqs, states)]
            for ci, st in zip(seqs, states):
                for h in range(hb):
                    so_ref[ci, h] = st[h]
            return carry

        lax.fori_loop(0, n_chunks // local, seq_body, 0)

    y = y_scr[...]
    dev = y - _dot3_right(y, head_mean)
    var = _dot3_right(dev * dev, head_mean)
    yn = dev * lax.rsqrt(var + LNX_EPS) * lw_ref[...] + lb_ref[...]
    y_ref[...] = ((yn + bonus) * gate).astype(y_ref.dtype)


def _rwkv_call(z, zprev_rows, s0, layer, p, *, nseq, seq_len, row0, nb, tb_rows, chunk, local):
    hb = 4
    wc = hb * HEAD_R
    rows = nb * tb_rows
    n_tb = seq_len // tb_rows
    assert nb == 1 or (tb_rows == chunk and n_tb == 1)
    rb0 = row0 // rows
    zrow = lambda sb, h, tb: rb0 + sb * n_tb + tb

    def zspec(col0, width):
        cb = col0 // width
        if width == wc:
            return pl.BlockSpec((rows, width), lambda sb, h, tb: (zrow(sb, h, tb), cb + h))
        return pl.BlockSpec((rows, width), lambda sb, h, tb: (zrow(sb, h, tb), cb))

    def pspec(col0, width):
        cb = col0 // width
        if width == wc:
            return pl.BlockSpec((nb * 8, width), lambda sb, h, tb: (sb, cb + h))
        return pl.BlockSpec((nb * 8, width), lambda sb, h, tb: (sb, cb))

    vec = lambda: pl.BlockSpec((1, wc), lambda sb, h, tb: (0, h))
    vec_at = lambda col0: pl.BlockSpec((1, wc), lambda sb, h, tb: (0, col0 // wc + h))
    lora = lambda kdim: pl.BlockSpec((kdim, wc), lambda sb, h, tb: (0, h))
    kern = functools.partial(_rwkv_kernel, nb=nb, tb_rows=tb_rows, chunk=chunk, hb=hb, local=local)
    scr = lambda: pltpu.VMEM((rows, wc), f32)
    return pl.pallas_call(
        kern,
        grid=(nseq // nb, NH_R // hb, n_tb),
        in_specs=[
            zspec(Z_RR, wc), zspec(Z_RK, wc), zspec(Z_RV, wc), zspec(Z_SM, SM_W),
            pspec(0, wc), pspec(D_RWKV, wc), pspec(2 * D_RWKV, wc), pspec(3 * D_RWKV, SM_W),
            vec_at(0), vec_at(D_RWKV), vec_at(2 * D_RWKV),
            pl.BlockSpec((1, SM_W), lambda sb, h, tb: (0, 3 * D_RWKV // SM_W)),
            vec(), vec(), vec(), vec(), vec(), vec(), vec(),
            lora(128), lora(128), lora(G_LORA),
            pl.BlockSpec((1, nb, hb, HEAD_R, HEAD_R), lambda sb, h, tb: (layer, sb, h, 0, 0)),
        ],
        out_specs=[
            pl.BlockSpec((rows, wc), lambda sb, h, tb: (sb * n_tb + tb, h)),
            pl.BlockSpec((nb, hb, HEAD_R, HEAD_R), lambda sb, h, tb: (sb, h, 0, 0)),
        ],
        out_shape=[
            jax.ShapeDtypeStruct((nseq * seq_len, D_RWKV), bf16),
            jax.ShapeDtypeStruct((nseq, NH_R, HEAD_R, HEAD_R), f32),
        ],
        scratch_shapes=[
            pltpu.VMEM((nb, hb, HEAD_R, HEAD_R), f32),
            pltpu.VMEM((4, 8, SM_W), f32),
            *[scr() for _ in range(11)],
            pltpu.VMEM((rows // chunk, hb, HEAD_R, HEAD_R), f32),
            pltpu.VMEM((rows // chunk, hb, HEAD_R, HEAD_R), f32),
        ],
        compiler_params=_cparams(("parallel", "parallel", "arbitrary")),
        name="rwkv",
    )(z, z, z, z, zprev_rows, zprev_rows, zprev_rows, zprev_rows,
      p["mu"], p["mu"], p["mu"], p["mu"], p["w0"], p["a0"], p["k_k"], p["k_a"], p["r_k"], p["lnx_w"], p["lnx_b"],
      p["w2"], p["a2"], p["g2"], s0)


def _mlstm_kernel(zq_ref, zk_ref, zv_ref, zo_ref, gate_ref, bias_ref, nw_ref, c0_ref, n0_ref, m0_ref,
                  y_ref, co_ref, no_ref, mo_ref,
                  c_scr, n_scr, m_scr, o_scr, *, nb, hm, tb_rows, chunk):
    tb = pl.program_id(2)
    n_tb = pl.num_programs(2)
    per_seq = tb_rows // chunk

    @pl.when(tb == 0)
    def _():
        c_scr[...] = c0_ref[0]
        n_scr[...] = n0_ref[0]
        m_scr[...] = m0_ref[0]

    incl, _, eye = _tri_masks(chunk)
    upper = lax.broadcasted_iota(i32, (chunk, chunk), 0) <= lax.broadcasted_iota(i32, (chunk, chunk), 1)
    pairs = [(s, h) for s in range(nb) for h in range(hm)]
    each = lambda f, *ls: [f(*xs) for xs in zip(*ls)]
    square = lambda row: jnp.broadcast_to(row, (chunk, chunk))
    row_sum = lambda x: jnp.sum(x, axis=-1, keepdims=True)

    def body(ci, carry):
        def load(ref):
            out = []
            for s, h in pairs:
                start = s * tb_rows + ci * chunk
                start = start if isinstance(start, int) else pl.multiple_of(start, chunk)
                out.append(ref[pl.ds(start, chunk), h * DK:(h + 1) * DK])
            return out

        q, v, o_gate = load(zq_ref), load(zv_ref), load(zo_ref)
        k = [x * (DK ** -0.5) for x in load(zk_ref)]
        gch = [gate_ref[h, s * per_seq + ci] for s, h in pairs]
        ic_row = [g[0:1, :] + bias_ref[h, 0:1, 0:1] for g, (s, h) in zip(gch, pairs)]
        fc_row = [-_softplus(-(g[1:2, :] + bias_ref[h, 1:2, 0:1])) for g, (s, h) in zip(gch, pairs)]
        c_mat = [c_scr[s, h] for s, h in pairs]
        n_row = [n_scr[s, h] for s, h in pairs]
        m_prev = [m_scr[s, h][:, 0:1] for s, h in pairs]
        f_col = each(lambda fr: row_sum(jnp.where(incl, square(fr), 0.0)), fc_row)
        fc_col = each(lambda fr: row_sum(jnp.where(eye, square(fr), 0.0)), fc_row)
        ic_col = each(lambda ir: row_sum(jnp.where(eye, square(ir), 0.0)), ic_row)
        f_row = each(lambda fc: jnp.sum(jnp.where(upper, square(fc), 0.0), axis=0, keepdims=True), fc_col)
        dm = each(lambda fc, fr, ir: jnp.where(incl, fc - fr + ir, -jnp.inf), f_col, f_row, ic_row)
        inter = each(lambda fc, m: fc + m, f_col, m_prev)
        m_t = each(lambda it_, d: jnp.maximum(it_, jnp.max(d, axis=-1, keepdims=True)), inter, dm)
        w_intra = each(lambda d, m: jnp.exp(d - m), dm, m_t)
        w_inter = each(lambda it_, m: jnp.exp(it_ - m), inter, m_t)
        s_mat = each(lambda q_, k_, w: _bdot(q_, k_, NT) * w, q, k, w_intra)
        qc = each(_bdot, q, c_mat)
        num = each(lambda s_, v_, w, x: _bdot(s_, v_) + w * x, s_mat, v, w_inter, qc)
        den = each(lambda s_, w, q_, n_: row_sum(s_) + w * row_sum(q_ * n_), s_mat, w_inter, q, n_row)
        hid = each(lambda a, d, m: a / jnp.maximum(jnp.abs(d), jnp.exp(-m)), num, den, m_t)
        m_new = each(lambda m: m[chunk - 1:chunk, :], m_t)
        w_s = each(lambda fc, ic, m: jnp.exp(fc[chunk - 1:chunk, :] - fc + ic - m), f_col, ic_col, m_new)
        w_c = each(lambda it_, m: jnp.exp(it_[chunk - 1:chunk, :] - m), inter, m_new)
        kw = each(lambda k_, w: k_ * w, k, w_s)
        c_new = each(lambda w, c, kw_, v_: w * c + _bdot(kw_, v_, TN), w_c, c_mat, kw, v)
        n_new = each(lambda w, n_, kw_: w * n_ + jnp.sum(kw_, axis=0, keepdims=True), w_c, n_row, kw)
        mean = each(lambda x: jnp.mean(x, axis=-1, keepdims=True), hid)
        var = each(lambda x, mu: jnp.mean(jnp.square(x - mu), axis=-1, keepdims=True), hid, mean)
        for i, (s, h) in enumerate(pairs):
            c_scr[s, h] = c_new[i]
            n_scr[s, h] = n_new[i]
            m_scr[s, h] = jnp.broadcast_to(m_new[i], (1, 128))
            hn = (hid[i] - mean[i]) * lax.rsqrt(var[i] + MH_EPS) * nw_ref[:, h * DK:(h + 1) * DK]
            start = s * tb_rows + ci * chunk
            start = start if isinstance(start, int) else pl.multiple_of(start, chunk)
            o_scr[pl.ds(start, chunk), h * DK:(h + 1) * DK] = hn * jax.nn.sigmoid(o_gate[i])
        return carry

    if per_seq == 1:
        body(0, 0)
    else:
        lax.fori_loop(0, per_seq, body, 0)
    y_ref[...] = o_scr[...].astype(y_ref.dtype)

    @pl.when(tb == n_tb - 1)
    def _():
        co_ref[...] = c_scr[...]
        no_ref[...] = n_scr[...]
        mo_ref[...] = m_scr[...]


def _mlstm_call(z, gates, bias, norm_w, c0, n0, m0, layer, *, nseq, seq_len, row0, nb, hm, tb_rows, chunk):
    rows = nb * tb_rows
    n_tb = seq_len // tb_rows
    rb0 = row0 // rows
    cpb = rows // chunk
    wm = hm * DK
    zspec = lambda col0: pl.BlockSpec((rows, wm), lambda sb, h, tb: (rb0 + sb * n_tb + tb, col0 // wm + h))
    st4 = lambda a, b: pl.BlockSpec((nb, hm, a, b), lambda sb, h, tb: (sb, h, 0, 0))
    st5 = lambda a, b: pl.BlockSpec((1, nb, hm, a, b), lambda sb, h, tb: (layer, sb, h, 0, 0))
    kern = functools.partial(_mlstm_kernel, nb=nb, hm=hm, tb_rows=tb_rows, chunk=chunk)
    return pl.pallas_call(
        kern,
        grid=(nseq // nb, NH_M // hm, n_tb),
        in_specs=[
            zspec(Z_MQ), zspec(Z_MK), zspec(Z_MV), zspec(Z_MO),
            pl.BlockSpec((hm, cpb, 2, chunk), lambda sb, h, tb: (h, sb * n_tb + tb, 0, 0)),
            pl.BlockSpec((hm, 2, 128), lambda sb, h, tb: (h, 0, 0)),
            pl.BlockSpec((1, wm), lambda sb, h, tb: (0, h)),
            st5(DK, DK), st5(1, DK), st5(1, 128),
        ],
        out_specs=[
            pl.BlockSpec((rows, wm), lambda sb, h, tb: (sb * n_tb + tb, h)),
            st4(DK, DK), st4(1, DK), st4(1, 128),
        ],
        out_shape=[
            jax.ShapeDtypeStruct((nseq * seq_len, D_MLSTM), bf16),
            jax.ShapeDtypeStruct((nseq, NH_M, DK, DK), f32),
            jax.ShapeDtypeStruct((nseq, NH_M, 1, DK), f32),
            jax.ShapeDtypeStruct((nseq, NH_M, 1, 128), f32),
        ],
        scratch_shapes=[
            pltpu.VMEM((nb, hm, DK, DK), f32),
            pltpu.VMEM((nb, hm, 1, DK), f32),
            pltpu.VMEM((nb, hm, 1, 128), f32),
            pltpu.VMEM((rows, wm), f32),
        ],
        compiler_params=_cparams(("parallel", "parallel", "arbitrary")),
        name="mlstm",
    )(z, z, z, z, gates, bias, norm_w, c0, n0, m0)


def _norm_route_kernel(x_ref, w_ref, scp_ref, scs_ref, shp_ref, shs_ref, rw_ref, rb_ref,
                       hn_ref, ei_ref, ew_ref, *, n_prompt_tiles):
    is_prompt = pl.program_id(0) < n_prompt_tiles
    y = _rms(x_ref[...], w_ref[...]).reshape(SEQ_PER_TILE, 8, D)
    sc = _pick_mod(is_prompt, scp_ref, scs_ref)
    sh = _pick_mod(is_prompt, shp_ref, shs_ref)
    hn = (y * (1.0 + sc) + sh).reshape(ROW_TILE, D)
    hn_ref[...] = hn
    logits = _bdot(hn, rw_ref[...]) + rb_ref[...]
    lane = lax.broadcasted_iota(i32, logits.shape, 1)
    lane_f = lane.astype(f32)
    valid = lane < N_EXPERTS
    lg = jnp.where(valid, logits, -1e30)
    ex = jnp.where(valid, jnp.exp(lg - jnp.max(lg, axis=-1, keepdims=True)), 0.0)
    probs = ex / jnp.sum(ex, axis=-1, keepdims=True)
    best = None
    for g in range(N_GROUPS):
        in_g = (lane >= g * EXPERTS_PER_GROUP) & (lane < (g + 1) * EXPERTS_PER_GROUP)
        pg = jnp.where(in_g, probs, -1.0)
        m1 = jnp.max(pg, axis=-1, keepdims=True)
        i1 = jnp.min(jnp.where(pg == m1, lane_f, 1e4), axis=-1, keepdims=True)
        pg2 = jnp.where(lane_f == i1, -1.0, pg)
        m2 = jnp.max(pg2, axis=-1, keepdims=True)
        i2 = jnp.min(jnp.where(pg2 == m2, lane_f, 1e4), axis=-1, keepdims=True)
        cand = (m1 + m2, m1, m2, i1, i2)
        if best is None:
            best = cand
        else:
            upd = cand[0] > best[0]
            best = tuple(jnp.where(upd, c, b) for c, b in zip(cand, best))
    _, m1, m2, i1, i2 = best
    tot = m1 + m2
    ei_ref[...] = jnp.where(lane == 0, i1, jnp.where(lane == 1, i2, 0.0)).astype(i32)
    ew_ref[...] = jnp.where(lane == 0, m1 / tot, jnp.where(lane == 1, m2 / tot, 0.0))


def _norm_route_call(x, w, sc_p, sc_s, sh_p, sh_s, rw, rb, prompt_len):
    m = x.shape[0]
    n_prompt_seq = sc_p.shape[0]
    npt = n_prompt_seq * prompt_len // ROW_TILE
    pspec, sspec = _mod_specs(npt, n_prompt_seq, prompt_len, D)
    row = lambda wd: pl.BlockSpec((ROW_TILE, wd), lambda i: (i, 0))
    return pl.pallas_call(
        functools.partial(_norm_route_kernel, n_prompt_tiles=npt),
        grid=(m // ROW_TILE,),
        in_specs=[row(D), pl.BlockSpec((1, D), lambda i: (0, 0)), pspec, sspec, pspec, sspec,
                  pl.BlockSpec((D, 128), lambda i: (0, 0)), pl.BlockSpec((1, 128), lambda i: (0, 0))],
        out_specs=[row(D), row(128), row(128)],
        out_shape=[jax.ShapeDtypeStruct((m, D), f32), jax.ShapeDtypeStruct((m, 128), i32),
                   jax.ShapeDtypeStruct((m, 128), f32)],
        compiler_params=_cparams(("parallel",)),
        name="norm2_route",
    )(x, w.reshape(1, D), sc_p, sc_s, sh_p, sh_s, rw, rb)


def _row_copy(src_hbm, row, dst, r, sem):
    return pltpu.make_async_copy(src_hbm.at[pl.ds(row, 1), :], dst.at[pl.ds(r, 1), :], sem)


def _gather_kernel(tok_ref, valid_ref, hn_hbm, o_ref, buf, sem):
    j = pl.program_id(0)
    n_j = pl.num_programs(0)

    def issue(block, slot):
        def one(r, c):
            _row_copy(hn_hbm, tok_ref[block * MOE_TM + r], buf.at[slot], r, sem.at[slot]).start()
            return c
        lax.fori_loop(0, MOE_TM, one, 0, unroll=8)

    def wait(slot):
        def one(r, c):
            _row_copy(hn_hbm, 0, buf.at[slot], r, sem.at[slot]).wait()
            return c
        lax.fori_loop(0, MOE_TM, one, 0, unroll=8)

    @pl.when((j == 0) & (valid_ref[0] == 1))
    def _():
        issue(0, 0)

    nxt = jnp.minimum(j + 1, n_j - 1)

    @pl.when((j + 1 < n_j) & (valid_ref[nxt] == 1))
    def _():
        issue(j + 1, (j + 1) % 2)

    @pl.when(valid_ref[j] == 1)
    def _():
        wait(j % 2)
        o_ref[...] = buf[j % 2].astype(o_ref.dtype)

    @pl.when(valid_ref[j] == 0)
    def _():
        o_ref[...] = jnp.zeros_like(o_ref)


def _gather_call(hn, slot_tok, valid):
    p = slot_tok.shape[0]
    return pl.pallas_call(
        _gather_kernel,
        grid_spec=pltpu.PrefetchScalarGridSpec(
            num_scalar_prefetch=2,
            grid=(p // MOE_TM,),
            in_specs=[pl.BlockSpec(memory_space=pl.ANY)],
            out_specs=pl.BlockSpec((MOE_TM, D), lambda j, tok, va: (j, 0)),
            scratch_shapes=[pltpu.VMEM((2, MOE_TM, D), f32), pltpu.SemaphoreType.DMA((2,))],
        ),
        out_shape=jax.ShapeDtypeStruct((p, D), bf16),
        compiler_params=_cparams(("arbitrary",)),
        name="moe_gather",
    )(slot_tok, valid, hn)


def _stream_expert_tiles(be_ref, first_ref, next_ref, w_hbm, wbuf, w_bf, sem, run_ref, *, layer, tn, col_tiles):
    n = pl.program_id(0)
    j = pl.program_id(1)
    n_pass = pl.num_programs(0)

    def copies(e, nn, slot):
        return [pltpu.make_async_copy(w_hbm.at[layer, e, :, pl.ds(pl.multiple_of(ct(nn) * tn, tn), tn)],
                                      wbuf.at[slot, c], sem.at[slot]) for c, ct in enumerate(col_tiles)]

    @pl.when((n == 0) & (j == 0))
    def _():
        run_ref[0] = 0
        for cp in copies(be_ref[0], 0, 0):
            cp.start()

    @pl.when(first_ref[j] == 1)
    def _():
        slot = run_ref[0] % 2
        for cp in copies(be_ref[j], n, slot):
            cp.wait()
        nxt = next_ref[j]

        @pl.when(nxt >= 0)
        def _():
            for cp in copies(nxt, n, 1 - slot):
                cp.start()

        @pl.when((nxt < 0) & (n + 1 < n_pass))
        def _():
            for cp in copies(be_ref[0], n + 1, 1 - slot):
                cp.start()

        for c in range(len(col_tiles)):
            w_bf[c] = wbuf[slot, c].astype(bf16)
        run_ref[0] = run_ref[0] + 1


def _moe1_kernel(be_ref, first_ref, next_ref, valid_ref, x_ref, w_hbm, act_ref, wbuf, w_bf, sem, run_ref,
                 *, layer, tn, n_t):
    j = pl.program_id(1)
    _stream_expert_tiles(be_ref, first_ref, next_ref, w_hbm, wbuf, w_bf, sem, run_ref, layer=layer, tn=tn,
                         col_tiles=(lambda n: n, lambda n: n_t + n))

    @pl.when(valid_ref[j] == 1)
    def _():
        x = x_ref[...]
        g = jnp.dot(x, w_bf[0], preferred_element_type=f32)
        u = jnp.dot(x, w_bf[1], preferred_element_type=f32)
        act_ref[...] = (g * jax.nn.sigmoid(g) * u).astype(act_ref.dtype)

    @pl.when(valid_ref[j] == 0)
    def _():
        act_ref[...] = jnp.zeros_like(act_ref)


def _moe1_call(xs, w_in, layer, block_e, first, nxt, valid):
    p = xs.shape[0]
    tn = 512
    n_t = D_EXPERT // tn
    return pl.pallas_call(
        functools.partial(_moe1_kernel, layer=layer, tn=tn, n_t=n_t),
        grid_spec=pltpu.PrefetchScalarGridSpec(
            num_scalar_prefetch=4,
            grid=(n_t, p // MOE_TM),
            in_specs=[
                pl.BlockSpec((MOE_TM, D), lambda n, j, *_: (j, 0)),
                pl.BlockSpec(memory_space=pl.ANY),
            ],
            out_specs=pl.BlockSpec((MOE_TM, tn), lambda n, j, *_: (j, n)),
            scratch_shapes=[pltpu.VMEM((2, 2, D, tn), f32), pltpu.VMEM((2, D, tn), bf16),
                            pltpu.SemaphoreType.DMA((2,)), pltpu.SMEM((1,), i32)],
        ),
        out_shape=jax.ShapeDtypeStruct((p, D_EXPERT), bf16),
        compiler_params=_cparams(("arbitrary", "arbitrary"), MOE_UP_VMEM),
        name="moe_up",
    )(block_e, first, nxt, valid, xs, w_in)


def _moe2_kernel(be_ref, first_ref, next_ref, valid_ref, a_ref, w_hbm, o_ref, wbuf, w_bf, sem, run_ref,
                 *, layer, tn):
    j = pl.program_id(1)
    _stream_expert_tiles(be_ref, first_ref, next_ref, w_hbm, wbuf, w_bf, sem, run_ref, layer=layer, tn=tn,
                         col_tiles=(lambda n: n,))

    @pl.when(valid_ref[j] == 1)
    def _():
        o_ref[...] = jnp.dot(a_ref[...], w_bf[0], preferred_element_type=f32)

    @pl.when(valid_ref[j] == 0)
    def _():
        o_ref[...] = jnp.zeros_like(o_ref)


def _moe2_call(act, w_out, layer, block_e, first, nxt, valid):
    p = act.shape[0]
    tn = 2048
    return pl.pallas_call(
        functools.partial(_moe2_kernel, layer=layer, tn=tn),
        grid_spec=pltpu.PrefetchScalarGridSpec(
            num_scalar_prefetch=4,
            grid=(D // tn, p // MOE_TM),
            in_specs=[
                pl.BlockSpec((MOE_TM, D_EXPERT), lambda n, j, *_: (j, 0)),
                pl.BlockSpec(memory_space=pl.ANY),
            ],
            out_specs=pl.BlockSpec((MOE_TM, tn), lambda n, j, *_: (j, n)),
            scratch_shapes=[pltpu.VMEM((2, 1, D_EXPERT, tn), f32), pltpu.VMEM((1, D_EXPERT, tn), bf16),
                            pltpu.SemaphoreType.DMA((2,)), pltpu.SMEM((1,), i32)],
        ),
        out_shape=jax.ShapeDtypeStruct((p, D), f32),
        compiler_params=_cparams(("arbitrary", "arbitrary")),
        name="moe_down",
    )(block_e, first, nxt, valid, act, w_out)


def _combine_kernel(pos_ref, x_ref, ew_ref, gp_ref, gs_ref, yb_hbm, o_ref, buf, sem, *, n_prompt_tiles):
    i = pl.program_id(0)
    n_i = pl.num_programs(0)

    def issue(tile, slot):
        def one(r, c):
            for k in range(2):
                _row_copy(yb_hbm, pos_ref[2 * (tile * ROW_TILE + r) + k], buf.at[slot, k], r, sem.at[slot]).start()
            return c
        lax.fori_loop(0, ROW_TILE, one, 0, unroll=4)

    def wait(slot):
        def one(r, c):
            for k in range(2):
                _row_copy(yb_hbm, 0, buf.at[slot, k], r, sem.at[slot]).wait()
            return c
        lax.fori_loop(0, ROW_TILE, one, 0, unroll=4)

    @pl.when(i == 0)
    def _():
        issue(0, 0)

    @pl.when(i + 1 < n_i)
    def _():
        issue(i + 1, (i + 1) % 2)

    wait(i % 2)
    ew = ew_ref[...]
    y = buf[i % 2, 0] * ew[:, 0:1] + buf[i % 2, 1] * ew[:, 1:2]
    g = _pick_mod(i < n_prompt_tiles, gp_ref, gs_ref)
    o_ref[...] = (x_ref[...].reshape(SEQ_PER_TILE, 8, D) + g * y.reshape(SEQ_PER_TILE, 8, D)).reshape(ROW_TILE, D)


def _combine_call(x, yb, pos, ew, g_p, g_s, prompt_len):
    m = x.shape[0]
    n_prompt_seq = g_p.shape[0]
    npt = n_prompt_seq * prompt_len // ROW_TILE
    tiles_per_seq = prompt_len // ROW_TILE
    return pl.pallas_call(
        functools.partial(_combine_kernel, n_prompt_tiles=npt),
        grid_spec=pltpu.PrefetchScalarGridSpec(
            num_scalar_prefetch=1,
            grid=(m // ROW_TILE,),
            in_specs=[
                pl.BlockSpec((ROW_TILE, D), lambda i, pos: (i, 0)),
                pl.BlockSpec((ROW_TILE, 128), lambda i, pos: (i, 0)),
                pl.BlockSpec((1, 1, D), lambda i, pos: (jnp.minimum(i // tiles_per_seq, n_prompt_seq - 1), 0, 0)),
                pl.BlockSpec((SEQ_PER_TILE, 1, D), lambda i, pos: (jnp.maximum(i - npt, 0), 0, 0)),
                pl.BlockSpec(memory_space=pl.ANY),
            ],
            out_specs=pl.BlockSpec((ROW_TILE, D), lambda i, pos: (i, 0)),
            scratch_shapes=[pltpu.VMEM((2, 2, ROW_TILE, D), f32), pltpu.SemaphoreType.DMA((2,))],
        ),
        out_shape=jax.ShapeDtypeStruct((m, D), f32),
        compiler_params=_cparams(("arbitrary",)),
        name="moe_combine",
    )(pos, x, ew, g_p, g_s, yb)


def _moe_plan(eidx, n_blocks):
    m = eidx.shape[0]
    a = 2 * m
    flat_e = eidx.reshape(a)
    order = jnp.argsort(flat_e).astype(i32)
    sorted_e = flat_e[order]
    bounds = jnp.searchsorted(sorted_e, jnp.arange(N_EXPERTS + 1, dtype=i32), side="left").astype(i32)
    start, counts = bounds[:-1], bounds[1:] - bounds[:-1]
    padded = (counts + MOE_TM - 1) // MOE_TM * MOE_TM
    pad_end = jnp.cumsum(padded)
    pad_start = pad_end - padded
    dest = pad_start[sorted_e] + jnp.arange(a, dtype=i32) - start[sorted_e]
    pos = dest[jnp.argsort(order)]
    blk = jnp.arange(n_blocks, dtype=i32) * MOE_TM
    block_e = jnp.minimum(jnp.searchsorted(pad_end, blk, side="right"), N_EXPERTS - 1).astype(i32)
    valid = (blk < pad_end[-1]).astype(i32)
    slot = jnp.arange(n_blocks * MOE_TM, dtype=i32)
    slot_e = jnp.repeat(block_e, MOE_TM)
    within = slot - pad_start[slot_e]
    slot_tok = jnp.where(within < counts[slot_e], order[jnp.clip(start[slot_e] + within, 0, a - 1)] // 2, 0)
    first = jnp.concatenate([jnp.ones((1,), i32), (block_e[1:] != block_e[:-1]).astype(i32)]) * valid
    first_at = jnp.where(first == 1, jnp.arange(n_blocks, dtype=i32), n_blocks)
    later = jnp.concatenate([lax.cummin(first_at[::-1])[::-1][1:], jnp.full((1,), n_blocks, i32)])
    nxt = jnp.where(later < n_blocks, block_e[jnp.minimum(later, n_blocks - 1)], -1).astype(i32)
    return slot_tok, pos, block_e, first, nxt, valid


def _prep_w_in(w):
    wt = jnp.swapaxes(w, 0, 1)
    zeros = lambda n: jnp.zeros((n, wt.shape[1]), wt.dtype)
    mb = RWKV_COLS
    gb = RWKV_COLS + MLSTM_COLS
    parts = [
        wt[0:2048], wt[2144:4192], wt[4192:6240],
        wt[mb:mb + 2048], wt[mb + 2048:mb + 4096], wt[mb + 4096:mb + 6144], wt[mb + 6160:mb + 8208],
        wt[gb:gb + 8192],
        wt[6336:6592], wt[2048:2144], zeros(32), wt[6240:6336], zeros(32),
        wt[mb + 6144:mb + 6160], zeros(112), zeros(NZ - Z_IF - 128),
    ]
    return jnp.concatenate(parts, axis=0).astype(bf16)


def _prep_mu(mu):
    zeros = jnp.zeros((32,), mu.dtype)
    return jnp.concatenate([mu[0:2048], mu[2144:4192], mu[4192:6240], mu[6336:6592], mu[2048:2144], zeros,
                            mu[6240:6336], zeros]).reshape(1, ZP_COLS)


def _pad_rows(w, rows):
    return jnp.concatenate([w, jnp.zeros((rows - w.shape[0], w.shape[1]), w.dtype)], axis=0).astype(bf16)


def _gate_rows(z, row0, rows, chunk):
    zif = z[row0:row0 + rows, Z_IF:Z_IF + 2 * NH_M].reshape(rows // chunk, chunk, 2, NH_M)
    return jnp.transpose(zif, (3, 0, 2, 1))


def kernel(x_prompt, x_sample, c_prompt, c_sample, state_wkv, state_shift, state_C, state_n, state_m, ada_w, ada_b, norm1_w, norm2_w, w_in, rwkv_mu, rwkv_w0, rwkv_w2, rwkv_a0, rwkv_a2, rwkv_g2, rwkv_k_k, rwkv_k_a, rwkv_r_k, rwkv_lnx_w, rwkv_lnx_b, mlstm_i_b, mlstm_f_b, mlstm_norm_w, w_branch_r, w_branch_m, w_out, router_w, router_b, moe_w_in, moe_w_out, final_norm_w):
    bp, tp, _ = x_prompt.shape
    bs, ts, _ = x_sample.shape
    mp, ms = bp * tp, bs * ts
    m = mp + ms
    assert ts == 8 and tp % ROW_TILE == 0 and ms % ROW_TILE == 0 and bs % SEQ_PER_TILE == 0 and m % 512 == 0
    n_seq = bp + bs
    n_seq_pad = -(-n_seq // 8) * 8

    x = jnp.concatenate([x_prompt.reshape(mp, D), x_sample.reshape(ms, D)], axis=0)
    c_all = jnp.concatenate([c_prompt, c_sample, jnp.zeros((n_seq_pad - n_seq, D), f32)], axis=0)
    ada = _ada_call(c_all, ada_w, ada_b).reshape(DEPTH, n_seq_pad, N_ADA, D)
    last_rows = jnp.concatenate([jnp.arange(bp, dtype=i32) * tp + tp - 1,
                                 mp + jnp.arange(bs, dtype=i32) * ts + ts - 1,
                                 jnp.zeros((n_seq_pad - n_seq,), i32)])
    router_w_p = jnp.concatenate([router_w, jnp.zeros((D, 128 - N_EXPERTS), f32)], axis=1).astype(bf16)
    router_b_p = jnp.concatenate([router_b, jnp.zeros((128 - N_EXPERTS,), f32)]).reshape(1, 128)
    n_blocks = -(-(2 * m + N_EXPERTS * (MOE_TM - 1)) // MOE_TM)

    prompt_geo = dict(nseq=bp, seq_len=tp, row0=0, nb=1, tb_rows=256, chunk=64)
    sample_geo = dict(nseq=bs, seq_len=ts, row0=mp, tb_rows=8, chunk=8)
    rwkv_p, rwkv_s = dict(prompt_geo, local=4), dict(sample_geo, nb=32, local=8)
    mlstm_p, mlstm_s = dict(prompt_geo, hm=8), dict(sample_geo, nb=4, hm=4)

    outs_p = [[] for _ in range(5)]
    outs_s = [[] for _ in range(5)]
    for l in range(DEPTH):
        mod = lambda idx: (ada[l, :bp, idx][:, None, :], ada[l, bp:n_seq, idx][:, None, :])
        sh1, sc1, g1, sh2, sc2, g2 = (mod(i) for i in range(N_ADA))
        wz = _prep_w_in(w_in[l])

        xn = _norm_mod_call(x, norm1_w[l], sc1[0], sc1[1], sh1[0], sh1[1], tp)
        shift_rows = _norm_rows_call(x[last_rows], norm1_w[l], ada[l, :, 1], ada[l, :, 0])
        z = _matmul_call(xn, wz, 512, 1024, name="w_in")
        zprev_s = _matmul_call(state_shift[l].astype(bf16), wz, bs, 512, n_out=ZP_COLS,
                               col_map=lambda j: jnp.where(j < 12, j, Z_SM // 512), name="w_in_prev")
        rp = dict(mu=_prep_mu(rwkv_mu[l]), w0=rwkv_w0[l].reshape(1, -1), a0=rwkv_a0[l].reshape(1, -1),
                  k_k=rwkv_k_k[l].reshape(1, -1), k_a=rwkv_k_a[l].reshape(1, -1), r_k=rwkv_r_k[l].reshape(1, -1),
                  lnx_w=rwkv_lnx_w[l].reshape(1, -1), lnx_b=rwkv_lnx_b[l].reshape(1, -1),
                  w2=_pad_rows(rwkv_w2[l], 128), a2=_pad_rows(rwkv_a2[l], 128), g2=rwkv_g2[l].astype(bf16))
        yr_p, wkv_p = _rwkv_call(z, jnp.zeros((bp * 8, ZP_COLS), f32),
                                 jnp.zeros((1, bp, NH_R, HEAD_R, HEAD_R), f32), 0, rp, **rwkv_p)
        yr_s, wkv_s = _rwkv_call(z, jnp.repeat(zprev_s, 8, axis=0), state_wkv, l, rp, **rwkv_s)
        bias = jnp.broadcast_to(jnp.stack([mlstm_i_b[l], mlstm_f_b[l]], axis=1)[:, :, None], (NH_M, 2, 128))
        nw = mlstm_norm_w[l].reshape(1, -1)
        ym_p, c_p, n_p, m_p = _mlstm_call(
            z, _gate_rows(z, 0, mp, 64), bias, nw, jnp.zeros((1, bp, NH_M, DK, DK), f32),
            jnp.zeros((1, bp, NH_M, 1, DK), f32), jnp.zeros((1, bp, NH_M, 1, 128), f32), 0, **mlstm_p)
        ym_s, c_s, n_s, m_s = _mlstm_call(
            z, _gate_rows(z, mp, ms, 8), bias, nw, state_C, state_n[:, :, :, None, :],
            jnp.broadcast_to(state_m[:, :, :, None, None], (DEPTH, bs, NH_M, 1, 128)), l, **mlstm_s)
        merged = _merge_call(jnp.concatenate([yr_p, yr_s], axis=0), jnp.concatenate([ym_p, ym_s], axis=0),
                             w_branch_r[l].astype(bf16), w_branch_m[l].astype(bf16), z)
        x = _wout_call(merged, w_out[l].astype(bf16), x, g1[0], g1[1], tp)

        hn, eidx, ew = _norm_route_call(x, norm2_w[l], sc2[0], sc2[1], sh2[0], sh2[1], router_w_p, router_b_p, tp)
        slot_tok, pos, block_e, first, nxt, valid = _moe_plan(eidx[:, :2], n_blocks)
        xs = _gather_call(hn, slot_tok, valid)
        act = _moe1_call(xs, moe_w_in, l, block_e, first, nxt, valid)
        yb = _moe2_call(act, moe_w_out, l, block_e, first, nxt, valid)
        x = _combine_call(x, yb, pos, ew, g2[0], g2[1], tp)

        for acc, vals in ((outs_p, (wkv_p, shift_rows[:bp], c_p, n_p[:, :, 0], m_p[:, :, 0, 0])),
                          (outs_s, (wkv_s, shift_rows[bp:n_seq], c_s, n_s[:, :, 0], m_s[:, :, 0, 0]))):
            for lst, v in zip(acc, vals):
                lst.append(v)

    y_prompt = _final_norm_call(x, final_norm_w, 0, mp).reshape(bp, tp, D)
    y_sample = _final_norm_call(x, final_norm_w, mp, ms).reshape(bs, ts, D)
    return (y_prompt, y_sample, *(jnp.stack(v) for v in outs_p), *(jnp.stack(v) for v in outs_s))
```

```python
import functools
import math

import jax
import jax.numpy as jnp
from jax import lax
from jax.experimental import pallas as pl
from jax.experimental.pallas import tpu as pltpu

f32 = jnp.float32
bf16 = jnp.bfloat16
i32 = jnp.int32

D = 4096
DEPTH = 2
D_RWKV = 2048
HEAD_R = 64
NH_R = D_RWKV // HEAD_R
W_LORA = 96
A_LORA = 96
G_LORA = 256
LNX_EPS = 64e-5
D_MLSTM = 2048
NH_M = 8
DK = D_MLSTM // NH_M
MH_EPS = 1e-6
RWKV_COLS = 3 * D_RWKV + W_LORA + A_LORA + G_LORA
MLSTM_COLS = 4 * D_MLSTM + 2 * NH_M
N_EXPERTS = 32
EXPERTS_PER_GROUP = 8
N_GROUPS = 4
D_EXPERT = D // 4
N_ADA = 6
RMS_EPS = 1e-6

Z_RR, Z_RK, Z_RV = 0, 2048, 4096
Z_MQ, Z_MK, Z_MV, Z_MO = 6144, 8192, 10240, 12288
Z_GR, Z_GM = 14336, 18432
Z_SM = 22528
Z_IF = 23040
NZ = 23552
SM_W = 512
ZP_COLS = 3 * D_RWKV + SM_W

ROW_TILE = 256
SEQ_PER_TILE = 32
MOE_TM = 256
VMEM_LIMIT = 48 * 1024 * 1024
MOE_UP_VMEM = 58 * 1024 * 1024

NN = (((1,), (0,)), ((), ()))
NT = (((1,), (1,)), ((), ()))
TN = (((0,), (0,)), ((), ()))


def _cparams(sem, vmem=VMEM_LIMIT):
    return pltpu.CompilerParams(dimension_semantics=sem, vmem_limit_bytes=vmem)


def _bdot(a, b, dims=NN):
    return lax.dot_general(a.astype(bf16), b.astype(bf16), dims, preferred_element_type=f32)


def _softplus(x):
    return jnp.maximum(x, 0.0) + jnp.log1p(jnp.exp(-jnp.abs(x)))


def _ada_kernel(c_ref, w_ref, b_ref, o_ref):
    c = c_ref[...]
    a = c * jax.nn.sigmoid(c)
    o_ref[0] = _bdot(a, w_ref[0]) + b_ref[0]


def _ada_call(c_all, ada_w, ada_b):
    nb = c_all.shape[0]
    n = ada_w.shape[2]
    tn = 512
    return pl.pallas_call(
        _ada_kernel,
        grid=(DEPTH, n // tn),
        in_specs=[
            pl.BlockSpec((nb, D), lambda l, j: (0, 0)),
            pl.BlockSpec((1, D, tn), lambda l, j: (l, 0, j)),
            pl.BlockSpec((1, 1, tn), lambda l, j: (l, 0, j)),
        ],
        out_specs=pl.BlockSpec((1, nb, tn), lambda l, j: (l, 0, j)),
        out_shape=jax.ShapeDtypeStruct((DEPTH, nb, n), f32),
        compiler_params=_cparams(("parallel", "parallel")),
        name="ada",
    )(c_all, ada_w, ada_b.reshape(DEPTH, 1, n))


def _mod_specs(n_prompt_tiles, n_prompt_seq, prompt_len, width, col=None):
    tiles_per_seq = prompt_len // ROW_TILE
    if col is None:
        pmap = lambda i: (jnp.minimum(i // tiles_per_seq, n_prompt_seq - 1), 0, 0)
        smap = lambda i: (jnp.maximum(i - n_prompt_tiles, 0), 0, 0)
    else:
        pmap = lambda j, i: (jnp.minimum(i // tiles_per_seq, n_prompt_seq - 1), 0, j)
        smap = lambda j, i: (jnp.maximum(i - n_prompt_tiles, 0), 0, j)
    return pl.BlockSpec((1, 1, width), pmap), pl.BlockSpec((SEQ_PER_TILE, 1, width), smap)


def _pick_mod(is_prompt, p_ref, s_ref):
    return jnp.where(is_prompt, jnp.broadcast_to(p_ref[...], s_ref.shape), s_ref[...])


def _rms(x, w):
    return x * lax.rsqrt(jnp.mean(x * x, axis=-1, keepdims=True) + RMS_EPS) * w


def _norm_mod_kernel(x_ref, w_ref, scp_ref, scs_ref, shp_ref, shs_ref, o_ref, *, n_prompt_tiles):
    is_prompt = pl.program_id(0) < n_prompt_tiles
    y = _rms(x_ref[...], w_ref[...]).reshape(SEQ_PER_TILE, 8, D)
    sc = _pick_mod(is_prompt, scp_ref, scs_ref)
    sh = _pick_mod(is_prompt, shp_ref, shs_ref)
    o_ref[...] = (y * (1.0 + sc) + sh).reshape(ROW_TILE, D).astype(o_ref.dtype)


def _norm_mod_call(x, w, sc_p, sc_s, sh_p, sh_s, prompt_len):
    m = x.shape[0]
    n_prompt_seq = sc_p.shape[0]
    npt = n_prompt_seq * prompt_len // ROW_TILE
    pspec, sspec = _mod_specs(npt, n_prompt_seq, prompt_len, D)
    return pl.pallas_call(
        functools.partial(_norm_mod_kernel, n_prompt_tiles=npt),
        grid=(m // ROW_TILE,),
        in_specs=[pl.BlockSpec((ROW_TILE, D), lambda i: (i, 0)), pl.BlockSpec((1, D), lambda i: (0, 0)),
                  pspec, sspec, pspec, sspec],
        out_specs=pl.BlockSpec((ROW_TILE, D), lambda i: (i, 0)),
        out_shape=jax.ShapeDtypeStruct((m, D), bf16),
        compiler_params=_cparams(("parallel",)),
        name="norm1",
    )(x, w.reshape(1, D), sc_p, sc_s, sh_p, sh_s)


def _norm_rows_kernel(x_ref, w_ref, sc_ref, sh_ref, o_ref):
    o_ref[...] = _rms(x_ref[...], w_ref[...]) * (1.0 + sc_ref[...]) + sh_ref[...]


def _norm_rows_call(x_rows, w, sc_rows, sh_rows):
    return pl.pallas_call(
        _norm_rows_kernel,
        out_shape=jax.ShapeDtypeStruct(x_rows.shape, f32),
        name="norm_rows",
    )(x_rows, w.reshape(1, D), sc_rows, sh_rows)


def _final_norm_kernel(x_ref, w_ref, o_ref):
    o_ref[...] = _rms(x_ref[...], w_ref[...])


def _final_norm_call(x, w, row0, rows):
    off = row0 // ROW_TILE
    return pl.pallas_call(
        _final_norm_kernel,
        grid=(rows // ROW_TILE,),
        in_specs=[pl.BlockSpec((ROW_TILE, D), lambda i: (i + off, 0)), pl.BlockSpec((1, D), lambda i: (0, 0))],
        out_specs=pl.BlockSpec((ROW_TILE, D), lambda i: (i, 0)),
        out_shape=jax.ShapeDtypeStruct((rows, D), f32),
        compiler_params=_cparams(("parallel",)),
        name="final_norm",
    )(x, w.reshape(1, D))


def _mm_kernel(a_ref, wt_ref, o_ref):
    o_ref[...] = lax.dot_general(a_ref[...], wt_ref[...], NT, preferred_element_type=f32)


def _matmul_call(a, wt, tm, tn, n_out=None, col_map=None, name="matmul"):
    m, k = a.shape
    n_out = wt.shape[0] if n_out is None else n_out
    wmap = (lambda j, i: (j, 0)) if col_map is None else (lambda j, i: (col_map(j), 0))
    return pl.pallas_call(
        _mm_kernel,
        grid=(n_out // tn, m // tm),
        in_specs=[pl.BlockSpec((tm, k), lambda j, i: (i, 0)), pl.BlockSpec((tn, k), wmap)],
        out_specs=pl.BlockSpec((tm, tn), lambda j, i: (i, j)),
        out_shape=jax.ShapeDtypeStruct((m, n_out), f32),
        compiler_params=_cparams(("parallel", "parallel")),
        name=name,
    )(a, wt)


def _merge_kernel(yr_ref, ym_ref, wr_ref, wm_ref, gr_ref, gm_ref, o_ref):
    pr = jnp.dot(yr_ref[...], wr_ref[...], preferred_element_type=f32)
    pm = jnp.dot(ym_ref[...], wm_ref[...], preferred_element_type=f32)
    o_ref[...] = (jax.nn.sigmoid(gr_ref[...]) * pr + jax.nn.sigmoid(gm_ref[...]) * pm).astype(o_ref.dtype)


def _merge_call(yr, ym, wr, wm, z):
    m = yr.shape[0]
    tm, tn = 512, 1024
    return pl.pallas_call(
        _merge_kernel,
        grid=(D // tn, m // tm),
        in_specs=[
            pl.BlockSpec((tm, D_RWKV), lambda j, i: (i, 0)),
            pl.BlockSpec((tm, D_MLSTM), lambda j, i: (i, 0)),
            pl.BlockSpec((D_RWKV, tn), lambda j, i: (0, j)),
            pl.BlockSpec((D_MLSTM, tn), lambda j, i: (0, j)),
            pl.BlockSpec((tm, tn), lambda j, i: (i, Z_GR // tn + j)),
            pl.BlockSpec((tm, tn), lambda j, i: (i, Z_GM // tn + j)),
        ],
        out_specs=pl.BlockSpec((tm, tn), lambda j, i: (i, j)),
        out_shape=jax.ShapeDtypeStruct((m, D), bf16),
        compiler_params=_cparams(("parallel", "parallel")),
        name="merge",
    )(yr, ym, wr, wm, z, z)


def _wout_kernel(a_ref, w_ref, x_ref, gp_ref, gs_ref, o_ref, *, n_prompt_tiles):
    is_prompt = pl.program_id(1) < n_prompt_tiles
    tn = o_ref.shape[1]
    acc = jnp.dot(a_ref[...], w_ref[...], preferred_element_type=f32).reshape(SEQ_PER_TILE, 8, tn)
    g = _pick_mod(is_prompt, gp_ref, gs_ref)
    o_ref[...] = (x_ref[...].reshape(SEQ_PER_TILE, 8, tn) + g * acc).reshape(ROW_TILE, tn)


def _wout_call(merged, w, x, g_p, g_s, prompt_len):
    m = x.shape[0]
    tn = 1024
    n_prompt_seq = g_p.shape[0]
    npt = n_prompt_seq * prompt_len // ROW_TILE
    pspec, sspec = _mod_specs(npt, n_prompt_seq, prompt_len, tn, col=True)
    return pl.pallas_call(
        functools.partial(_wout_kernel, n_prompt_tiles=npt),
        grid=(D // tn, m // ROW_TILE),
        in_specs=[
            pl.BlockSpec((ROW_TILE, D), lambda j, i: (i, 0)),
            pl.BlockSpec((D, tn), lambda j, i: (0, j)),
            pl.BlockSpec((ROW_TILE, tn), lambda j, i: (i, j)),
            pspec, sspec,
        ],
        out_specs=pl.BlockSpec((ROW_TILE, tn), lambda j, i: (i, j)),
        out_shape=jax.ShapeDtypeStruct((m, D), f32),
        compiler_params=_cparams(("parallel", "parallel")),
        name="wout",
    )(merged, w, x, g_p, g_s)


def _tri_masks(n):
    t = lax.broadcasted_iota(i32, (n, n), 0)
    s = lax.broadcasted_iota(i32, (n, n), 1)
    return s <= t, s < t, s == t


def _rwkv_local(at, rt, bt, kt, v, bh, kh, lam):
    n = at[0].shape[0]
    incl, strict, eye = _tri_masks(n)
    eye_k = _tri_masks(HEAD_R)[2]
    each = lambda f, *ls: [f(*xs) for xs in zip(*ls)]
    g = each(lambda a_, r_, b_, k_: _bdot(jnp.concatenate([a_, r_], axis=0), jnp.concatenate([b_, k_], axis=0), NT),
             at, rt, bt, kt)
    n_mat = each(lambda x: jnp.where(strict, x[:n, :n], 0.0), g)
    m_mat = each(lambda x: jnp.where(strict, x[:n, n:], 0.0), g)
    pb = each(lambda x: jnp.where(incl, x[n:, :n], 0.0), g)
    pk = each(lambda x: jnp.where(incl, x[n:, n:], 0.0), g)
    t_mat = each(lambda x: jnp.where(eye, 1.0, x), n_mat)
    n_pow = n_mat
    lv = 2
    while lv < n:
        n_pow = each(lambda x: _bdot(x, x), n_pow)
        t_mat = each(lambda t_, p_: t_ + _bdot(t_, p_), t_mat, n_pow)
        lv *= 2
    tam = each(lambda t_, a_, m_: _bdot(t_, jnp.concatenate([a_, m_], axis=1)), t_mat, at, m_mat)
    qw = each(_bdot, pb, tam)
    q = each(lambda r_, x: r_ + x[:, :HEAD_R], rt, qw)
    w_loc = each(lambda x, p_, v_: _bdot(x[:, HEAD_R:] + p_, v_), qw, pk, v)
    ab = each(lambda x, b_: _bdot(x, b_, TN), tam, bh)
    a_tr = each(lambda l_, x: jnp.where(eye_k, l_, 0.0) + x[:HEAD_R], lam, ab)
    s_inc = each(lambda v_, k_, x: _bdot(v_, k_ + x[HEAD_R:], TN), v, kh, ab)
    return q, w_loc, a_tr, s_inc


def _split3(x):
    hi = x.astype(bf16)
    r1 = x - hi.astype(f32)
    mid = r1.astype(bf16)
    lo = (r1 - mid.astype(f32)).astype(bf16)
    return hi, mid, lo


def _dot3_left(w, x):
    return sum(jnp.dot(w, p, preferred_element_type=f32) for p in _split3(x))


def _dot3_right(x, w):
    return sum(jnp.dot(p, w, preferred_element_type=f32) for p in _split3(x))


def _rwkv_kernel(zr_ref, zk_ref, zv_ref, zs_ref, pr_ref, pk_ref, pv_ref, ps_ref,
                 mur_ref, muk_ref, muv_ref, mus_ref, w0_ref, a0_ref, kk_ref, ka_ref, rk_ref, lw_ref, lb_ref,
                 w2_ref, a2_ref, g2_ref, s0_ref,
                 y_ref, so_ref,
                 s_scr, prev_scr, at_scr, rt_scr, bt_scr, kt_scr, v_scr, bh_scr, kh_scr, lam_scr,
                 q_scr, wl_scr, y_scr, atr_scr, sinc_scr,
                 *, nb, tb_rows, chunk, hb, local):
    tb = pl.program_id(2)
    n_tb = pl.num_programs(2)
    rows = nb * tb_rows
    per_seq = tb_rows // chunk
    n_chunks = rows // chunk
    wc = hb * HEAD_R

    @pl.when(tb == 0)
    def _():
        if nb == 1:
            s_scr[...] = s0_ref[0]
            for slot, (p_ref, width) in enumerate(((pr_ref, wc), (pk_ref, wc), (pv_ref, wc), (ps_ref, SM_W))):
                prev_scr[slot, 0:1, 0:width] = p_ref[0:1, :]

    def shifted(z_ref, p_ref, slot, width):
        z = z_ref[...]
        rolled = pltpu.roll(z, 1, 0)
        row = lax.broadcasted_iota(i32, z.shape, 0)
        if nb == 1:
            zs = jnp.where(row == 0, prev_scr[slot, 0:1, 0:width], rolled)
            prev_scr[slot, 0:1, 0:width] = z[rows - 1:rows, :]
        else:
            zs = jnp.where(row % tb_rows == 0, p_ref[...], rolled)
        return z, zs

    def lerp(z_ref, p_ref, mu_ref, slot, width):
        z, zs = shifted(z_ref, p_ref, slot, width)
        return z + (zs - z) * mu_ref[...]

    r = lerp(zr_ref, pr_ref, mur_ref, 0, wc)
    k = lerp(zk_ref, pk_ref, muk_ref, 1, wc)
    v = lerp(zv_ref, pv_ref, muv_ref, 2, wc)
    sm = lerp(zs_ref, ps_ref, mus_ref, 3, SM_W)
    gl = sm[:, 0:G_LORA]
    wl = sm[:, G_LORA:G_LORA + 128]
    al = sm[:, G_LORA + 128:G_LORA + 256]
    w_raw = -_softplus(-(w0_ref[...] + _bdot(jnp.tanh(wl), w2_ref[...]))) - 0.5
    eta = jax.nn.sigmoid(a0_ref[...] + _bdot(al, a2_ref[...]))
    lw = -jnp.exp(w_raw)
    k2 = k * (1.0 + (eta - 1.0) * ka_ref[...])

    shift = chunk.bit_length() - 1
    ti = lax.broadcasted_iota(i32, (rows, rows), 0)
    si = lax.broadcasted_iota(i32, (rows, rows), 1)
    same_chunk = (ti >> shift) == (si >> shift)
    tri_sel = jnp.where(same_chunk & (si <= ti), 1.0, 0.0).astype(bf16)
    hshift = HEAD_R.bit_length() - 1
    head_sel = (lax.broadcasted_iota(i32, (wc, wc), 0) >> hshift) == (lax.broadcasted_iota(i32, (wc, wc), 1) >> hshift)
    head_sum = jnp.where(head_sel, 1.0, 0.0).astype(bf16)
    head_mean = jnp.where(head_sel, 1.0 / HEAD_R, 0.0).astype(bf16)

    c = k * kk_ref[...]
    c = c / jnp.maximum(jnp.sqrt(_dot3_right(c * c, head_sum)), 1e-12)
    b_vec = c * eta
    cum = _dot3_left(tri_sel, lw)
    tot = jnp.concatenate([jnp.broadcast_to(cum[(ci + 1) * chunk - 1:(ci + 1) * chunk, :], (chunk, wc))
                           for ci in range(n_chunks)], axis=0)
    e_out = jnp.exp(-cum)
    tail = jnp.exp(tot - cum)
    at_scr[...] = -c * jnp.exp(cum - lw)
    rt_scr[...] = r * jnp.exp(cum)
    bt_scr[...] = b_vec * e_out
    kt_scr[...] = k2 * e_out
    bh_scr[...] = b_vec * tail
    kh_scr[...] = k2 * tail
    lam_scr[...] = jnp.exp(tot)
    v_scr[...] = v
    bonus = _dot3_right(r * k2 * rk_ref[...], head_sum) * v
    gate = _bdot(jax.nn.sigmoid(gl), g2_ref[...])

    def local_body(it, carry):
        where = []
        for j in range(local):
            ci = it * local + j
            sl = pl.ds(pl.multiple_of(ci * chunk, chunk), chunk)
            where += [(ci, sl, h, slice(h * HEAD_R, (h + 1) * HEAD_R)) for h in range(hb)]
        load = lambda ref: [ref[sl, ls] for _, sl, _, ls in where]
        lam = [lam_scr[pl.ds(ci * chunk, 1), ls] for ci, _, _, ls in where]
        q, w_loc, a_tr, s_inc = _rwkv_local(load(at_scr), load(rt_scr), load(bt_scr), load(kt_scr), load(v_scr),
                                            load(bh_scr), load(kh_scr), lam)
        for i, (ci, sl, h, ls) in enumerate(where):
            q_scr[sl, ls] = q[i]
            wl_scr[sl, ls] = w_loc[i]
            atr_scr[ci, h] = a_tr[i]
            sinc_scr[ci, h] = s_inc[i]
        return carry

    lax.fori_loop(0, n_chunks // local, local_body, 0)

    def advance(ci, state):
        sl = pl.ds(ci * chunk if isinstance(ci, int) else pl.multiple_of(ci * chunk, chunk), chunk)
        ys, new_state = [], []
        for h in range(hb):
            ls = slice(h * HEAD_R, (h + 1) * HEAD_R)
            ys.append(_bdot(q_scr[sl, ls], state[h], NT) + wl_scr[sl, ls])
            new_state.append(_bdot(state[h], atr_scr[ci, h]) + sinc_scr[ci, h])
        for h in range(hb):
            y_scr[sl, h * HEAD_R:(h + 1) * HEAD_R] = ys[h]
        return new_state

    if nb == 1:
        state = [s_scr[0, h] for h in range(hb)]
        for ci in range(n_chunks):
            state = advance(ci, state)
        for h in range(hb):
            s_scr[0, h] = state[h]

        @pl.when(tb == n_tb - 1)
        def _():
            so_ref[...] = s_scr[...]
    else:
        def seq_body(it, carry):
            seqs = [it * local + j for j in range(local)]
            states = [[s0_ref[0, ci, h] for h in range(hb)] for ci in seqs]
            states = [advance(ci, st) for ci, st in zip(seqs, states)]
            for ci, st in zip(seqs, states):
                for h in range(hb):
                    so_ref[ci, h] = st[h]
            return carry

        lax.fori_loop(0, n_chunks // local, seq_body, 0)

    y = y_scr[...]
    dev = y - _dot3_right(y, head_mean)
    var = _dot3_right(dev * dev, head_mean)
    yn = dev * lax.rsqrt(var + LNX_EPS) * lw_ref[...] + lb_ref[...]
    y_ref[...] = ((yn + bonus) * gate).astype(y_ref.dtype)


def _rwkv_call(z, zprev_rows, s0, layer, p, *, nseq, seq_len, row0, nb, tb_rows, chunk, local):
    hb = 4
    wc = hb * HEAD_R
    rows = nb * tb_rows
    n_tb = seq_len // tb_rows
    assert nb == 1 or (tb_rows == chunk and n_tb == 1)
    rb0 = row0 // rows
    zrow = lambda sb, h, tb: rb0 + sb * n_tb + tb

    def zspec(col0, width):
        cb = col0 // width
        if width == wc:
            return pl.BlockSpec((rows, width), lambda sb, h, tb: (zrow(sb, h, tb), cb + h))
        return pl.BlockSpec((rows, width), lambda sb, h, tb: (zrow(sb, h, tb), cb))

    def pspec(col0, width):
        cb = col0 // width
        if width == wc:
            return pl.BlockSpec((nb * 8, width), lambda sb, h, tb: (sb, cb + h))
        return pl.BlockSpec((nb * 8, width), lambda sb, h, tb: (sb, cb))

    vec = lambda: pl.BlockSpec((1, wc), lambda sb, h, tb: (0, h))
    vec_at = lambda col0: pl.BlockSpec((1, wc), lambda sb, h, tb: (0, col0 // wc + h))
    lora = lambda kdim: pl.BlockSpec((kdim, wc), lambda sb, h, tb: (0, h))
    kern = functools.partial(_rwkv_kernel, nb=nb, tb_rows=tb_rows, chunk=chunk, hb=hb, local=local)
    scr = lambda: pltpu.VMEM((rows, wc), f32)
    return pl.pallas_call(
        kern,
        grid=(nseq // nb, NH_R // hb, n_tb),
        in_specs=[
            zspec(Z_RR, wc), zspec(Z_RK, wc), zspec(Z_RV, wc), zspec(Z_SM, SM_W),
            pspec(0, wc), pspec(D_RWKV, wc), pspec(2 * D_RWKV, wc), pspec(3 * D_RWKV, SM_W),
            vec_at(0), vec_at(D_RWKV), vec_at(2 * D_RWKV),
            pl.BlockSpec((1, SM_W), lambda sb, h, tb: (0, 3 * D_RWKV // SM_W)),
            vec(), vec(), vec(), vec(), vec(), vec(), vec(),
            lora(128), lora(128), lora(G_LORA),
            pl.BlockSpec((1, nb, hb, HEAD_R, HEAD_R), lambda sb, h, tb: (layer, sb, h, 0, 0)),
        ],
        out_specs=[
            pl.BlockSpec((rows, wc), lambda sb, h, tb: (sb * n_tb + tb, h)),
            pl.BlockSpec((nb, hb, HEAD_R, HEAD_R), lambda sb, h, tb: (sb, h, 0, 0)),
        ],
        out_shape=[
            jax.ShapeDtypeStruct((nseq * seq_len, D_RWKV), bf16),
            jax.ShapeDtypeStruct((nseq, NH_R, HEAD_R, HEAD_R), f32),
        ],
        scratch_shapes=[
            pltpu.VMEM((nb, hb, HEAD_R, HEAD_R), f32),
            pltpu.VMEM((4, 8, SM_W), f32),
            *[scr() for _ in range(11)],
            pltpu.VMEM((rows // chunk, hb, HEAD_R, HEAD_R), f32),
            pltpu.VMEM((rows // chunk, hb, HEAD_R, HEAD_R), f32),
        ],
        compiler_params=_cparams(("parallel", "parallel", "arbitrary")),
        name="rwkv",
    )(z, z, z, z, zprev_rows, zprev_rows, zprev_rows, zprev_rows,
      p["mu"], p["mu"], p["mu"], p["mu"], p["w0"], p["a0"], p["k_k"], p["k_a"], p["r_k"], p["lnx_w"], p["lnx_b"],
      p["w2"], p["a2"], p["g2"], s0)


def _mlstm_kernel(zq_ref, zk_ref, zv_ref, zo_ref, gate_ref, bias_ref, nw_ref, c0_ref, n0_ref, m0_ref,
                  y_ref, co_ref, no_ref, mo_ref,
                  c_scr, n_scr, m_scr, o_scr, *, nb, hm, tb_rows, chunk):
    tb = pl.program_id(2)
    n_tb = pl.num_programs(2)
    per_seq = tb_rows // chunk

    @pl.when(tb == 0)
    def _():
        c_scr[...] = c0_ref[0]
        n_scr[...] = n0_ref[0]
        m_scr[...] = m0_ref[0]

    incl, _, eye = _tri_masks(chunk)
    upper = lax.broadcasted_iota(i32, (chunk, chunk), 0) <= lax.broadcasted_iota(i32, (chunk, chunk), 1)
    pairs = [(s, h) for s in range(nb) for h in range(hm)]
    each = lambda f, *ls: [f(*xs) for xs in zip(*ls)]
    square = lambda row: jnp.broadcast_to(row, (chunk, chunk))
    row_sum = lambda x: jnp.sum(x, axis=-1, keepdims=True)

    def body(ci, carry):
        def load(ref):
            out = []
            for s, h in pairs:
                start = s * tb_rows + ci * chunk
                start = start if isinstance(start, int) else pl.multiple_of(start, chunk)
                out.append(ref[pl.ds(start, chunk), h * DK:(h + 1) * DK])
            return out

        q, v, o_gate = load(zq_ref), load(zv_ref), load(zo_ref)
        k = [x * (DK ** -0.5) for x in load(zk_ref)]
        gch = [gate_ref[h, s * per_seq + ci] for s, h in pairs]
        ic_row = [g[0:1, :] + bias_ref[h, 0:1, 0:1] for g, (s, h) in zip(gch, pairs)]
        fc_row = [-_softplus(-(g[1:2, :] + bias_ref[h, 1:2, 0:1])) for g, (s, h) in zip(gch, pairs)]
        c_mat = [c_scr[s, h] for s, h in pairs]
        n_row = [n_scr[s, h] for s, h in pairs]
        m_prev = [m_scr[s, h][:, 0:1] for s, h in pairs]
        f_col = each(lambda fr: row_sum(jnp.where(incl, square(fr), 0.0)), fc_row)
        fc_col = each(lambda fr: row_sum(jnp.where(eye, square(fr), 0.0)), fc_row)
        ic_col = each(lambda ir: row_sum(jnp.where(eye, square(ir), 0.0)), ic_row)
        f_row = each(lambda fc: jnp.sum(jnp.where(upper, square(fc), 0.0), axis=0, keepdims=True), fc_col)
        dm = each(lambda fc, fr, ir: jnp.where(incl, fc - fr + ir, -jnp.inf), f_col, f_row, ic_row)
        inter = each(lambda fc, m: fc + m, f_col, m_prev)
        m_t = each(lambda it_, d: jnp.maximum(it_, jnp.max(d, axis=-1, keepdims=True)), inter, dm)
        w_intra = each(lambda d, m: jnp.exp(d - m), dm, m_t)
        w_inter = each(lambda it_, m: jnp.exp(it_ - m), inter, m_t)
        s_mat = each(lambda q_, k_, w: _bdot(q_, k_, NT) * w, q, k, w_intra)
        qc = each(_bdot, q, c_mat)
        num = each(lambda s_, v_, w, x: _bdot(s_, v_) + w * x, s_mat, v, w_inter, qc)
        den = each(lambda s_, w, q_, n_: row_sum(s_) + w * row_sum(q_ * n_), s_mat, w_inter, q, n_row)
        hid = each(lambda a, d, m: a / jnp.maximum(jnp.abs(d), jnp.exp(-m)), num, den, m_t)
        m_new = each(lambda m: m[chunk - 1:chunk, :], m_t)
        w_s = each(lambda fc, ic, m: jnp.exp(fc[chunk - 1:chunk, :] - fc + ic - m), f_col, ic_col, m_new)
        w_c = each(lambda it_, m: jnp.exp(it_[chunk - 1:chunk, :] - m), inter, m_new)
        kw = each(lambda k_, w: k_ * w, k, w_s)
        c_new = each(lambda w, c, kw_, v_: w * c + _bdot(kw_, v_, TN), w_c, c_mat, kw, v)
        n_new = each(lambda w, n_, kw_: w * n_ + jnp.sum(kw_, axis=0, keepdims=True), w_c, n_row, kw)
        mean = each(lambda x: jnp.mean(x, axis=-1, keepdims=True), hid)
        var = each(lambda x, mu: jnp.mean(jnp.square(x - mu), axis=-1, keepdims=True), hid, mean)
        for i, (s, h) in enumerate(pairs):
            c_scr[s, h] = c_new[i]
            n_scr[s, h] = n_new[i]
            m_scr[s, h] = jnp.broadcast_to(m_new[i], (1, 128))
            hn = (hid[i] - mean[i]) * lax.rsqrt(var[i] + MH_EPS) * nw_ref[:, h * DK:(h + 1) * DK]
            start = s * tb_rows + ci * chunk
            start = start if isinstance(start, int) else pl.multiple_of(start, chunk)
            o_scr[pl.ds(start, chunk), h * DK:(h + 1) * DK] = hn * jax.nn.sigmoid(o_gate[i])
        return carry

    if per_seq == 1:
        body(0, 0)
    else:
        lax.fori_loop(0, per_seq, body, 0)
    y_ref[...] = o_scr[...].astype(y_ref.dtype)

    @pl.when(tb == n_tb - 1)
    def _():
        co_ref[...] = c_scr[...]
        no_ref[...] = n_scr[...]
        mo_ref[...] = m_scr[...]


def _mlstm_call(z, gates, bias, norm_w, c0, n0, m0, layer, *, nseq, seq_len, row0, nb, hm, tb_rows, chunk):
    rows = nb * tb_rows
    n_tb = seq_len // tb_rows
    rb0 = row0 // rows
    cpb = rows // chunk
    wm = hm * DK
    zspec = lambda col0: pl.BlockSpec((rows, wm), lambda sb, h, tb: (rb0 + sb * n_tb + tb, col0 // wm + h))
    st4 = lambda a, b: pl.BlockSpec((nb, hm, a, b), lambda sb, h, tb: (sb, h, 0, 0))
    st5 = lambda a, b: pl.BlockSpec((1, nb, hm, a, b), lambda sb, h, tb: (layer, sb, h, 0, 0))
    kern = functools.partial(_mlstm_kernel, nb=nb, hm=hm, tb_rows=tb_rows, chunk=chunk)
    return pl.pallas_call(
        kern,
        grid=(nseq // nb, NH_M // hm, n_tb),
        in_specs=[
            zspec(Z_MQ), zspec(Z_MK), zspec(Z_MV), zspec(Z_MO),
            pl.BlockSpec((hm, cpb, 2, chunk), lambda sb, h, tb: (h, sb * n_tb + tb, 0, 0)),
            pl.BlockSpec((hm, 2, 128), lambda sb, h, tb: (h, 0, 0)),
            pl.BlockSpec((1, wm), lambda sb, h, tb: (0, h)),
            st5(DK, DK), st5(1, DK), st5(1, 128),
        ],
        out_specs=[
            pl.BlockSpec((rows, wm), lambda sb, h, tb: (sb * n_tb + tb, h)),
            st4(DK, DK), st4(1, DK), st4(1, 128),
        ],
        out_shape=[
            jax.ShapeDtypeStruct((nseq * seq_len, D_MLSTM), bf16),
            jax.ShapeDtypeStruct((nseq, NH_M, DK, DK), f32),
            jax.ShapeDtypeStruct((nseq, NH_M, 1, DK), f32),
            jax.ShapeDtypeStruct((nseq, NH_M, 1, 128), f32),
        ],
        scratch_shapes=[
            pltpu.VMEM((nb, hm, DK, DK), f32),
            pltpu.VMEM((nb, hm, 1, DK), f32),
            pltpu.VMEM((nb, hm, 1, 128), f32),
            pltpu.VMEM((rows, wm), f32),
        ],
        compiler_params=_cparams(("parallel", "parallel", "arbitrary")),
        name="mlstm",
    )(z, z, z, z, gates, bias, norm_w, c0, n0, m0)


def _norm_route_kernel(x_ref, w_ref, scp_ref, scs_ref, shp_ref, shs_ref, rw_ref, rb_ref,
                       hn_ref, ei_ref, ew_ref, *, n_prompt_tiles):
    is_prompt = pl.program_id(0) < n_prompt_tiles
    y = _rms(x_ref[...], w_ref[...]).reshape(SEQ_PER_TILE, 8, D)
    sc = _pick_mod(is_prompt, scp_ref, scs_ref)
    sh = _pick_mod(is_prompt, shp_ref, shs_ref)
    hn = (y * (1.0 + sc) + sh).reshape(ROW_TILE, D)
    hn_ref[...] = hn
    logits = _bdot(hn, rw_ref[...]) + rb_ref[...]
    lane = lax.broadcasted_iota(i32, logits.shape, 1)
    lane_f = lane.astype(f32)
    valid = lane < N_EXPERTS
    lg = jnp.where(valid, logits, -1e30)
    ex = jnp.where(valid, jnp.exp(lg - jnp.max(lg, axis=-1, keepdims=True)), 0.0)
    probs = ex / jnp.sum(ex, axis=-1, keepdims=True)
    best = None
    for g in range(N_GROUPS):
        in_g = (lane >= g * EXPERTS_PER_GROUP) & (lane < (g + 1) * EXPERTS_PER_GROUP)
        pg = jnp.where(in_g, probs, -1.0)
        m1 = jnp.max(pg, axis=-1, keepdims=True)
        i1 = jnp.min(jnp.where(pg == m1, lane_f, 1e4), axis=-1, keepdims=True)
        pg2 = jnp.where(lane_f == i1, -1.0, pg)
        m2 = jnp.max(pg2, axis=-1, keepdims=True)
        i2 = jnp.min(jnp.where(pg2 == m2, lane_f, 1e4), axis=-1, keepdims=True)
        cand = (m1 + m2, m1, m2, i1, i2)
        if best is None:
            best = cand
        else:
            upd = cand[0] > best[0]
            best = tuple(jnp.where(upd, c, b) for c, b in zip(cand, best))
    _, m1, m2, i1, i2 = best
    tot = m1 + m2
    ei_ref[...] = jnp.where(lane == 0, i1, jnp.where(lane == 1, i2, 0.0)).astype(i32)
    ew_ref[...] = jnp.where(lane == 0, m1 / tot, jnp.where(lane == 1, m2 / tot, 0.0))


def _norm_route_call(x, w, sc_p, sc_s, sh_p, sh_s, rw, rb, prompt_len):
    m = x.shape[0]
    n_prompt_seq = sc_p.shape[0]
    npt = n_prompt_seq * prompt_len // ROW_TILE
    pspec, sspec = _mod_specs(npt, n_prompt_seq, prompt_len, D)
    row = lambda wd: pl.BlockSpec((ROW_TILE, wd), lambda i: (i, 0))
    return pl.pallas_call(
        functools.partial(_norm_route_kernel, n_prompt_tiles=npt),
        grid=(m // ROW_TILE,),
        in_specs=[row(D), pl.BlockSpec((1, D), lambda i: (0, 0)), pspec, sspec, pspec, sspec,
                  pl.BlockSpec((D, 128), lambda i: (0, 0)), pl.BlockSpec((1, 128), lambda i: (0, 0))],
        out_specs=[row(D), row(128), row(128)],
        out_shape=[jax.ShapeDtypeStruct((m, D), f32), jax.ShapeDtypeStruct((m, 128), i32),
                   jax.ShapeDtypeStruct((m, 128), f32)],
        compiler_params=_cparams(("parallel",)),
        name="norm2_route",
    )(x, w.reshape(1, D), sc_p, sc_s, sh_p, sh_s, rw, rb)


def _row_copy(src_hbm, row, dst, r, sem):
    return pltpu.make_async_copy(src_hbm.at[pl.ds(row, 1), :], dst.at[pl.ds(r, 1), :], sem)


def _gather_kernel(tok_ref, valid_ref, hn_hbm, o_ref, buf, sem):
    j = pl.program_id(0)
    n_j = pl.num_programs(0)

    def issue(block, slot):
        def eight(r8, c):
            for k in range(8):
                r = r8 * 8 + k
                _row_copy(hn_hbm, tok_ref[block * MOE_TM + r], buf.at[slot], r, sem.at[slot]).start(priority=k % 2)
            return c
        lax.fori_loop(0, MOE_TM // 8, eight, 0)

    def wait(slot):
        def one(r, c):
            _row_copy(hn_hbm, 0, buf.at[slot], r, sem.at[slot]).wait()
            return c
        lax.fori_loop(0, MOE_TM, one, 0, unroll=8)

    @pl.when((j == 0) & (valid_ref[0] == 1))
    def _():
        issue(0, 0)

    nxt = jnp.minimum(j + 1, n_j - 1)

    @pl.when((j + 1 < n_j) & (valid_ref[nxt] == 1))
    def _():
        issue(j + 1, (j + 1) % 2)

    @pl.when(valid_ref[j] == 1)
    def _():
        wait(j % 2)
        o_ref[...] = buf[j % 2].astype(o_ref.dtype)

    @pl.when(valid_ref[j] == 0)
    def _():
        o_ref[...] = jnp.zeros_like(o_ref)


def _gather_call(hn, slot_tok, valid):
    p = slot_tok.shape[0]
    return pl.pallas_call(
        _gather_kernel,
        grid_spec=pltpu.PrefetchScalarGridSpec(
            num_scalar_prefetch=2,
            grid=(p // MOE_TM,),
            in_specs=[pl.BlockSpec(memory_space=pl.ANY)],
            out_specs=pl.BlockSpec((MOE_TM, D), lambda j, tok, va: (j, 0)),
            scratch_shapes=[pltpu.VMEM((2, MOE_TM, D), f32), pltpu.SemaphoreType.DMA((2,))],
        ),
        out_shape=jax.ShapeDtypeStruct((p, D), bf16),
        compiler_params=_cparams(("arbitrary",)),
        name="moe_gather",
    )(slot_tok, valid, hn)


def _stream_expert_tiles(be_ref, first_ref, next_ref, w_hbm, wbuf, w_bf, sem, run_ref, *, layer, tn, col_tiles):
    n = pl.program_id(0)
    j = pl.program_id(1)
    n_pass = pl.num_programs(0)

    def copies(e, nn, slot):
        return [pltpu.make_async_copy(w_hbm.at[layer, e, :, pl.ds(pl.multiple_of(ct(nn) * tn, tn), tn)],
                                      wbuf.at[slot, c], sem.at[slot]) for c, ct in enumerate(col_tiles)]

    @pl.when((n == 0) & (j == 0))
    def _():
        run_ref[0] = 0
        for cp in copies(be_ref[0], 0, 0):
            cp.start()

    @pl.when(first_ref[j] == 1)
    def _():
        slot = run_ref[0] % 2
        for cp in copies(be_ref[j], n, slot):
            cp.wait()
        nxt = next_ref[j]

        @pl.when(nxt >= 0)
        def _():
            for cp in copies(nxt, n, 1 - slot):
                cp.start()

        @pl.when((nxt < 0) & (n + 1 < n_pass))
        def _():
            for cp in copies(be_ref[0], n + 1, 1 - slot):
                cp.start()

        for c in range(len(col_tiles)):
            w_bf[c] = wbuf[slot, c].astype(bf16)
        run_ref[0] = run_ref[0] + 1


def _moe1_kernel(be_ref, first_ref, next_ref, valid_ref, x_ref, w_hbm, act_ref, wbuf, w_bf, sem, run_ref,
                 *, layer, tn, n_t):
    j = pl.program_id(1)
    _stream_expert_tiles(be_ref, first_ref, next_ref, w_hbm, wbuf, w_bf, sem, run_ref, layer=layer, tn=tn,
                         col_tiles=(lambda n: n, lambda n: n_t + n))

    @pl.when(valid_ref[j] == 1)
    def _():
        x = x_ref[...]
        g = jnp.dot(x, w_bf[0], preferred_element_type=f32)
        u = jnp.dot(x, w_bf[1], preferred_element_type=f32)
        act_ref[...] = (g * jax.nn.sigmoid(g) * u).astype(act_ref.dtype)

    @pl.when(valid_ref[j] == 0)
    def _():
        act_ref[...] = jnp.zeros_like(act_ref)


def _moe1_call(xs, w_in, layer, block_e, first, nxt, valid):
    p = xs.shape[0]
    tn = 512
    n_t = D_EXPERT // tn
    return pl.pallas_call(
        functools.partial(_moe1_kernel, layer=layer, tn=tn, n_t=n_t),
        grid_spec=pltpu.PrefetchScalarGridSpec(
            num_scalar_prefetch=4,
            grid=(n_t, p // MOE_TM),
            in_specs=[
                pl.BlockSpec((MOE_TM, D), lambda n, j, *_: (j, 0)),
                pl.BlockSpec(memory_space=pl.ANY),
            ],
            out_specs=pl.BlockSpec((MOE_TM, tn), lambda n, j, *_: (j, n)),
            scratch_shapes=[pltpu.VMEM((2, 2, D, tn), f32), pltpu.VMEM((2, D, tn), bf16),
                            pltpu.SemaphoreType.DMA((2,)), pltpu.SMEM((1,), i32)],
        ),
        out_shape=jax.ShapeDtypeStruct((p, D_EXPERT), bf16),
        compiler_params=_cparams(("arbitrary", "arbitrary"), MOE_UP_VMEM),
        name="moe_up",
    )(block_e, first, nxt, valid, xs, w_in)


def _moe2_kernel(be_ref, first_ref, next_ref, valid_ref, a_ref, w_hbm, o_ref, wbuf, w_bf, sem, run_ref,
                 *, layer, tn):
    j = pl.program_id(1)
    _stream_expert_tiles(be_ref, first_ref, next_ref, w_hbm, wbuf, w_bf, sem, run_ref, layer=layer, tn=tn,
                         col_tiles=(lambda n: n,))

    @pl.when(valid_ref[j] == 1)
    def _():
        o_ref[...] = jnp.dot(a_ref[...], w_bf[0], preferred_element_type=f32)

    @pl.when(valid_ref[j] == 0)
    def _():
        o_ref[...] = jnp.zeros_like(o_ref)


def _moe2_call(act, w_out, layer, block_e, first, nxt, valid):
    p = act.shape[0]
    tn = 2048
    return pl.pallas_call(
        functools.partial(_moe2_kernel, layer=layer, tn=tn),
        grid_spec=pltpu.PrefetchScalarGridSpec(
            num_scalar_prefetch=4,
            grid=(D // tn, p // MOE_TM),
            in_specs=[
                pl.BlockSpec((MOE_TM, D_EXPERT), lambda n, j, *_: (j, 0)),
                pl.BlockSpec(memory_space=pl.ANY),
            ],
            out_specs=pl.BlockSpec((MOE_TM, tn), lambda n, j, *_: (j, n)),
            scratch_shapes=[pltpu.VMEM((2, 1, D_EXPERT, tn), f32), pltpu.VMEM((1, D_EXPERT, tn), bf16),
                            pltpu.SemaphoreType.DMA((2,)), pltpu.SMEM((1,), i32)],
        ),
        out_shape=jax.ShapeDtypeStruct((p, D), f32),
        compiler_params=_cparams(("arbitrary", "arbitrary")),
        name="moe_down",
    )(block_e, first, nxt, valid, act, w_out)


def _combine_kernel(pos_ref, x_ref, ew_ref, gp_ref, gs_ref, yb_hbm, o_ref, buf, sem, *, n_prompt_tiles):
    i = pl.program_id(0)
    n_i = pl.num_programs(0)

    def issue(tile, slot):
        def one(r, c):
            for k in range(2):
                _row_copy(yb_hbm, pos_ref[2 * (tile * ROW_TILE + r) + k], buf.at[slot, k], r,
                          sem.at[slot]).start(priority=k)
            return c
        lax.fori_loop(0, ROW_TILE, one, 0, unroll=4)

    def wait(slot):
        def one(r, c):
            for k in range(2):
                _row_copy(yb_hbm, 0, buf.at[slot, k], r, sem.at[slot]).wait()
            return c
        lax.fori_loop(0, ROW_TILE, one, 0, unroll=4)

    @pl.when(i == 0)
    def _():
        issue(0, 0)

    @pl.when(i + 1 < n_i)
    def _():
        issue(i + 1, (i + 1) % 2)

    wait(i % 2)
    ew = ew_ref[...]
    y = buf[i % 2, 0] * ew[:, 0:1] + buf[i % 2, 1] * ew[:, 1:2]
    g = _pick_mod(i < n_prompt_tiles, gp_ref, gs_ref)
    o_ref[...] = (x_ref[...].reshape(SEQ_PER_TILE, 8, D) + g * y.reshape(SEQ_PER_TILE, 8, D)).reshape(ROW_TILE, D)


def _combine_call(x, yb, pos, ew, g_p, g_s, prompt_len):
    m = x.shape[0]
    n_prompt_seq = g_p.shape[0]
    npt = n_prompt_seq * prompt_len // ROW_TILE
    tiles_per_seq = prompt_len // ROW_TILE
    return pl.pallas_call(
        functools.partial(_combine_kernel, n_prompt_tiles=npt),
        grid_spec=pltpu.PrefetchScalarGridSpec(
            num_scalar_prefetch=1,
            grid=(m // ROW_TILE,),
            in_specs=[
                pl.BlockSpec((ROW_TILE, D), lambda i, pos: (i, 0)),
                pl.BlockSpec((ROW_TILE, 128), lambda i, pos: (i, 0)),
                pl.BlockSpec((1, 1, D), lambda i, pos: (jnp.minimum(i // tiles_per_seq, n_prompt_seq - 1), 0, 0)),
                pl.BlockSpec((SEQ_PER_TILE, 1, D), lambda i, pos: (jnp.maximum(i - npt, 0), 0, 0)),
                pl.BlockSpec(memory_space=pl.ANY),
            ],
            out_specs=pl.BlockSpec((ROW_TILE, D), lambda i, pos: (i, 0)),
            scratch_shapes=[pltpu.VMEM((2, 2, ROW_TILE, D), f32), pltpu.SemaphoreType.DMA((2,))],
        ),
        out_shape=jax.ShapeDtypeStruct((m, D), f32),
        compiler_params=_cparams(("arbitrary",)),
        name="moe_combine",
    )(pos, x, ew, g_p, g_s, yb)


def _moe_plan(eidx, n_blocks):
    m = eidx.shape[0]
    a = 2 * m
    flat_e = eidx.reshape(a)
    order = jnp.argsort(flat_e).astype(i32)
    sorted_e = flat_e[order]
    bounds = jnp.searchsorted(sorted_e, jnp.arange(N_EXPERTS + 1, dtype=i32), side="left").astype(i32)
    start, counts = bounds[:-1], bounds[1:] - bounds[:-1]
    padded = (counts + MOE_TM - 1) // MOE_TM * MOE_TM
    pad_end = jnp.cumsum(padded)
    pad_start = pad_end - padded
    dest = pad_start[sorted_e] + jnp.arange(a, dtype=i32) - start[sorted_e]
    pos = dest[jnp.argsort(order)]
    blk = jnp.arange(n_blocks, dtype=i32) * MOE_TM
    block_e = jnp.minimum(jnp.searchsorted(pad_end, blk, side="right"), N_EXPERTS - 1).astype(i32)
    valid = (blk < pad_end[-1]).astype(i32)
    slot = jnp.arange(n_blocks * MOE_TM, dtype=i32)
    slot_e = jnp.repeat(block_e, MOE_TM)
    within = slot - pad_start[slot_e]
    slot_tok = jnp.where(within < counts[slot_e], order[jnp.clip(start[slot_e] + within, 0, a - 1)] // 2, 0)
    first = jnp.concatenate([jnp.ones((1,), i32), (block_e[1:] != block_e[:-1]).astype(i32)]) * valid
    first_at = jnp.where(first == 1, jnp.arange(n_blocks, dtype=i32), n_blocks)
    later = jnp.concatenate([lax.cummin(first_at[::-1])[::-1][1:], jnp.full((1,), n_blocks, i32)])
    nxt = jnp.where(later < n_blocks, block_e[jnp.minimum(later, n_blocks - 1)], -1).astype(i32)
    return slot_tok, pos, block_e, first, nxt, valid


def _prep_w_in(w):
    wt = jnp.swapaxes(w, 0, 1)
    zeros = lambda n: jnp.zeros((n, wt.shape[1]), wt.dtype)
    mb = RWKV_COLS
    gb = RWKV_COLS + MLSTM_COLS
    parts = [
        wt[0:2048], wt[2144:4192], wt[4192:6240],
        wt[mb:mb + 2048], wt[mb + 2048:mb + 4096], wt[mb + 4096:mb + 6144], wt[mb + 6160:mb + 8208],
        wt[gb:gb + 8192],
        wt[6336:6592], wt[2048:2144], zeros(32), wt[6240:6336], zeros(32),
        wt[mb + 6144:mb + 6160], zeros(112), zeros(NZ - Z_IF - 128),
    ]
    return jnp.concatenate(parts, axis=0).astype(bf16)


def _prep_mu(mu):
    zeros = jnp.zeros((32,), mu.dtype)
    return jnp.concatenate([mu[0:2048], mu[2144:4192], mu[4192:6240], mu[6336:6592], mu[2048:2144], zeros,
                            mu[6240:6336], zeros]).reshape(1, ZP_COLS)


def _pad_rows(w, rows):
    return jnp.concatenate([w, jnp.zeros((rows - w.shape[0], w.shape[1]), w.dtype)], axis=0).astype(bf16)


def _gate_rows(z, row0, rows, chunk):
    zif = z[row0:row0 + rows, Z_IF:Z_IF + 2 * NH_M].reshape(rows // chunk, chunk, 2, NH_M)
    return jnp.transpose(zif, (3, 0, 2, 1))


def kernel(x_prompt, x_sample, c_prompt, c_sample, state_wkv, state_shift, state_C, state_n, state_m, ada_w, ada_b, norm1_w, norm2_w, w_in, rwkv_mu, rwkv_w0, rwkv_w2, rwkv_a0, rwkv_a2, rwkv_g2, rwkv_k_k, rwkv_k_a, rwkv_r_k, rwkv_lnx_w, rwkv_lnx_b, mlstm_i_b, mlstm_f_b, mlstm_norm_w, w_branch_r, w_branch_m, w_out, router_w, router_b, moe_w_in, moe_w_out, final_norm_w):
    bp, tp, _ = x_prompt.shape
    bs, ts, _ = x_sample.shape
    mp, ms = bp * tp, bs * ts
    m = mp + ms
    assert ts == 8 and tp % ROW_TILE == 0 and ms % ROW_TILE == 0 and bs % SEQ_PER_TILE == 0 and m % 512 == 0
    n_seq = bp + bs
    n_seq_pad = -(-n_seq // 8) * 8

    x = jnp.concatenate([x_prompt.reshape(mp, D), x_sample.reshape(ms, D)], axis=0)
    c_all = jnp.concatenate([c_prompt, c_sample, jnp.zeros((n_seq_pad - n_seq, D), f32)], axis=0)
    ada = _ada_call(c_all, ada_w, ada_b).reshape(DEPTH, n_seq_pad, N_ADA, D)
    last_rows = jnp.concatenate([jnp.arange(bp, dtype=i32) * tp + tp - 1,
                                 mp + jnp.arange(bs, dtype=i32) * ts + ts - 1,
                                 jnp.zeros((n_seq_pad - n_seq,), i32)])
    router_w_p = jnp.concatenate([router_w, jnp.zeros((D, 128 - N_EXPERTS), f32)], axis=1).astype(bf16)
    router_b_p = jnp.concatenate([router_b, jnp.zeros((128 - N_EXPERTS,), f32)]).reshape(1, 128)
    n_blocks = -(-(2 * m + N_EXPERTS * (MOE_TM - 1)) // MOE_TM)

    prompt_geo = dict(nseq=bp, seq_len=tp, row0=0, nb=1, tb_rows=256, chunk=64)
    sample_geo = dict(nseq=bs, seq_len=ts, row0=mp, tb_rows=8, chunk=8)
    rwkv_p, rwkv_s = dict(prompt_geo, local=4), dict(sample_geo, nb=32, local=8)
    mlstm_p, mlstm_s = dict(prompt_geo, hm=8), dict(sample_geo, nb=4, hm=4)

    outs_p = [[] for _ in range(5)]
    outs_s = [[] for _ in range(5)]
    for l in range(DEPTH):
        mod = lambda idx: (ada[l, :bp, idx][:, None, :], ada[l, bp:n_seq, idx][:, None, :])
        sh1, sc1, g1, sh2, sc2, g2 = (mod(i) for i in range(N_ADA))
        wz = _prep_w_in(w_in[l])

        xn = _norm_mod_call(x, norm1_w[l], sc1[0], sc1[1], sh1[0], sh1[1], tp)
        shift_rows = _norm_rows_call(x[last_rows], norm1_w[l], ada[l, :, 1], ada[l, :, 0])
        z = _matmul_call(xn, wz, 512, 1024, name="w_in")
        zprev_s = _matmul_call(state_shift[l].astype(bf16), wz, bs, 512, n_out=ZP_COLS,
                               col_map=lambda j: jnp.where(j < 12, j, Z_SM // 512), name="w_in_prev")
        rp = dict(mu=_prep_mu(rwkv_mu[l]), w0=rwkv_w0[l].reshape(1, -1), a0=rwkv_a0[l].reshape(1, -1),
                  k_k=rwkv_k_k[l].reshape(1, -1), k_a=rwkv_k_a[l].reshape(1, -1), r_k=rwkv_r_k[l].reshape(1, -1),
                  lnx_w=rwkv_lnx_w[l].reshape(1, -1), lnx_b=rwkv_lnx_b[l].reshape(1, -1),
                  w2=_pad_rows(rwkv_w2[l], 128), a2=_pad_rows(rwkv_a2[l], 128), g2=rwkv_g2[l].astype(bf16))
        yr_p, wkv_p = _rwkv_call(z, jnp.zeros((bp * 8, ZP_COLS), f32),
                                 jnp.zeros((1, bp, NH_R, HEAD_R, HEAD_R), f32), 0, rp, **rwkv_p)
        yr_s, wkv_s = _rwkv_call(z, jnp.repeat(zprev_s, 8, axis=0), state_wkv, l, rp, **rwkv_s)
        bias = jnp.broadcast_to(jnp.stack([mlstm_i_b[l], mlstm_f_b[l]], axis=1)[:, :, None], (NH_M, 2, 128))
        nw = mlstm_norm_w[l].reshape(1, -1)
        ym_p, c_p, n_p, m_p = _mlstm_call(
            z, _gate_rows(z, 0, mp, 64), bias, nw, jnp.zeros((1, bp, NH_M, DK, DK), f32),
            jnp.zeros((1, bp, NH_M, 1, DK), f32), jnp.zeros((1, bp, NH_M, 1, 128), f32), 0, **mlstm_p)
        ym_s, c_s, n_s, m_s = _mlstm_call(
            z, _gate_rows(z, mp, ms, 8), bias, nw, state_C, state_n[:, :, :, None, :],
            jnp.broadcast_to(state_m[:, :, :, None, None], (DEPTH, bs, NH_M, 1, 128)), l, **mlstm_s)
        merged = _merge_call(jnp.concatenate([yr_p, yr_s], axis=0), jnp.concatenate([ym_p, ym_s], axis=0),
                             w_branch_r[l].astype(bf16), w_branch_m[l].astype(bf16), z)
        x = _wout_call(merged, w_out[l].astype(bf16), x, g1[0], g1[1], tp)

        hn, eidx, ew = _norm_route_call(x, norm2_w[l], sc2[0], sc2[1], sh2[0], sh2[1], router_w_p, router_b_p, tp)
        slot_tok, pos, block_e, first, nxt, valid = _moe_plan(eidx[:, :2], n_blocks)
        xs = _gather_call(hn, slot_tok, valid)
        act = _moe1_call(xs, moe_w_in, l, block_e, first, nxt, valid)
        yb = _moe2_call(act, moe_w_out, l, block_e, first, nxt, valid)
        x = _combine_call(x, yb, pos, ew, g2[0], g2[1], tp)

        for acc, vals in ((outs_p, (wkv_p, shift_rows[:bp], c_p, n_p[:, :, 0], m_p[:, :, 0, 0])),
                          (outs_s, (wkv_s, shift_rows[bp:n_seq], c_s, n_s[:, :, 0], m_s[:, :, 0, 0]))):
            for lst, v in zip(acc, vals):
                lst.append(v)

    y_prompt = _final_norm_call(x, final_norm_w, 0, mp).reshape(bp, tp, D)
    y_sample = _final_norm_call(x, final_norm_w, mp, ms).reshape(bs, ts, D)
    return (y_prompt, y_sample, *(jnp.stack(v) for v in outs_p), *(jnp.stack(v) for v in outs_s))
```
